```python
import jax, jax.numpy as jnp
from jax import lax
import numpy as np

D_MODEL = 1024
BATCH = 16
SEQ = 2048
DEPTH = 1

D_FF = 2816
MLSTM_HEADS = 4
MLSTM_QK_DIM = 64
MLSTM_V_DIM = 128
MLSTM_CHUNK = 64
GATE_SOFTCAP = 15.0
ATTN_Q_HEADS = 8
ATTN_KV_HEADS = 2
ATTN_HEAD_DIM = 64
WINDOW = 128
ROPE_DIM = ATTN_HEAD_DIM // 4
ROPE_THETA = 500000.0
NORM_EPS = 1e-6

MLSTM_QK_W = MLSTM_HEADS * MLSTM_QK_DIM
MLSTM_V_W = MLSTM_HEADS * MLSTM_V_DIM
ATTN_Q_W = ATTN_Q_HEADS * ATTN_HEAD_DIM
ATTN_KV_W = ATTN_KV_HEADS * ATTN_HEAD_DIM
IN_WIDTHS = (MLSTM_QK_W, MLSTM_QK_W, MLSTM_V_W, MLSTM_V_W, MLSTM_HEADS, MLSTM_HEADS,
             ATTN_Q_W, ATTN_KV_W, ATTN_KV_W, D_MODEL, D_MODEL)
IN_WIDTH = sum(IN_WIDTHS)

kernel_name = 'hybrid_mlstm_swa_sinks_macaron'


def _rmsnorm(x, g):
    xf = x.astype(jnp.float32)
    y = xf * lax.rsqrt(jnp.mean(xf * xf, axis=-1, keepdims=True) + NORM_EPS)
    return (y * g.astype(jnp.float32)).astype(x.dtype)


def _swiglu(h, w_gate, w_up, w_down):
    return (jax.nn.silu(h @ w_gate) * (h @ w_up)) @ w_down


def _split_cols(a, widths):
    out = []
    start = 0
    for w in widths:
        out.append(a[..., start:start + w])
        start += w
    return out


def _softcap(a):
    return GATE_SOFTCAP * jnp.tanh(a / GATE_SOFTCAP)


def _partial_rope(x, positions):
    half = ROPE_DIM // 2
    inv = ROPE_THETA ** (-jnp.arange(half, dtype=jnp.float32) * 2.0 / ROPE_DIM)
    ang = positions.astype(jnp.float32)[:, None, :, None] * inv
    cos, sin = jnp.cos(ang), jnp.sin(ang)
    xr = x[..., :ROPE_DIM].astype(jnp.float32)
    x1, x2 = xr[..., :half], xr[..., half:]
    rot = jnp.concatenate([x1 * cos - x2 * sin, x2 * cos + x1 * sin], axis=-1).astype(x.dtype)
    return jnp.concatenate([rot, x[..., ROPE_DIM:]], axis=-1)


def _mlstm(q, k, v, i_pre, f_pre):
    B, H, S, dk = q.shape
    dv = v.shape[-1]
    L = MLSTM_CHUNK
    NC = S // L
    q = q * (dk ** -0.5)
    logf = jax.nn.log_sigmoid(f_pre)
    rs = lambda a: a.reshape((B, H, NC, L) + a.shape[3:])
    qc, kc, vc, ic = rs(q), rs(k), rs(v), rs(i_pre)
    b = jnp.cumsum(rs(logf), axis=-1)
    b_last = b[..., -1]
    a = b_last[..., None] - b + ic

    def step(carry, inp):
        C, n, m = carry
        k_c, v_c, a_c, bl = inp
        m_new = jnp.maximum(bl + m, jnp.max(a_c, axis=-1))
        decay = jnp.exp(bl + m - m_new)
        w = jnp.exp(a_c - m_new[..., None])
        C_new = decay[..., None, None] * C + jnp.einsum('bhl,bhlv,bhlk->bhvk', w, v_c, k_c)
        n_new = decay[..., None] * n + jnp.einsum('bhl,bhlk->bhk', w, k_c)
        return (C_new, n_new, m_new), (C, n, m)

    init = (jnp.zeros((B, H, dv, dk), jnp.float32), jnp.zeros((B, H, dk), jnp.float32),
            jnp.zeros((B, H), jnp.float32))
    mv = lambda t: jnp.moveaxis(t, 2, 0)
    _, (C_prev, n_prev, m_prev) = lax.scan(step, init, (mv(kc), mv(vc), mv(a), mv(b_last)))
    C_prev = jnp.moveaxis(C_prev, 0, 2)
    n_prev = jnp.moveaxis(n_prev, 0, 2)
    m_prev = jnp.moveaxis(m_prev, 0, 2)

    causal = jnp.tril(jnp.ones((L, L), dtype=bool))
    logD = b[..., :, None] - b[..., None, :] + ic[..., None, :]
    logD = jnp.where(causal, logD, -jnp.inf)
    inter_log = b + m_prev[..., None]
    m = jnp.maximum(inter_log, jnp.max(logD, axis=-1))
    sqk = jnp.einsum('bhcjk,bhcsk->bhcjs', qc, kc) * jnp.exp(logD - m[..., None])
    inter_scale = jnp.exp(inter_log - m)
    num = (jnp.einsum('bhcjs,bhcsv->bhcjv', sqk, vc)
           + inter_scale[..., None] * jnp.einsum('bhcjk,bhcvk->bhcjv', qc, C_prev))
    den = jnp.sum(sqk, axis=-1) + inter_scale * jnp.einsum('bhcjk,bhck->bhcj', qc, n_prev)
    h = num / jnp.maximum(jnp.abs(den), jnp.exp(-m))[..., None]
    return h.reshape(B, H, S, dv)


def _swa_sinks(q, k, v, sinks):
    B, Hq, S, hd = q.shape
    Hkv = k.shape[1]
    G = Hq // Hkv
    W = WINDOW
    NB = S // W
    qb = q.reshape(B, Hkv, G, NB, W, hd) * (hd ** -0.5)
    pad = lambda t: jnp.pad(t, ((0, 0), (0, 0), (W, 0), (0, 0))).reshape(B, Hkv, NB + 1, W, hd)
    kp, vp = pad(k), pad(v)
    kb = jnp.concatenate([kp[:, :, :-1], kp[:, :, 1:]], axis=3)
    vb = jnp.concatenate([vp[:, :, :-1], vp[:, :, 1:]], axis=3)
    s = jnp.einsum('bkgnqd,bknsd->bkgnqs', qb, kb).astype(jnp.float32)
    qi = jnp.arange(W)[:, None]
    si = jnp.arange(2 * W)[None, :]
    band = (si > qi) & (si <= qi + W)
    valid = (jnp.arange(NB)[:, None, None] > 0) | (si[None] >= W)
    s = jnp.where(band[None] & valid, s, -jnp.inf)
    sink = jnp.broadcast_to(sinks.astype(jnp.float32).reshape(1, Hkv, G, 1, 1, 1),
                            s.shape[:-1] + (1,))
    p = jax.nn.softmax(jnp.concatenate([s, sink], axis=-1), axis=-1)[..., :-1]
    o = jnp.einsum('bkgnqs,bknsd->bkgnqd', p.astype(vb.dtype), vb)
    return o.reshape(B, Hq, S, hd)


def _mixer(h, positions, w_in, b_i, b_f, out_norm_g, sinks, w_br_m, w_br_a, w_out):
    B, S, _ = h.shape
    proj = h @ w_in
    q_m, k_m, v_m, o_m, i_m, f_m, q_a, k_a, v_a, g_m, g_a = _split_cols(proj, IN_WIDTHS)
    heads = lambda t, n: t.reshape(B, S, n, -1).transpose(0, 2, 1, 3)
    f32 = jnp.float32
    i_pre = _softcap((i_m + b_i).astype(f32)).transpose(0, 2, 1)
    f_pre = _softcap((f_m + b_f).astype(f32)).transpose(0, 2, 1)
    hm = _mlstm(heads(q_m, MLSTM_HEADS).astype(f32), heads(k_m, MLSTM_HEADS).astype(f32),
                heads(v_m, MLSTM_HEADS).astype(f32), i_pre, f_pre)
    hm = hm * lax.rsqrt(jnp.mean(hm * hm, axis=-1, keepdims=True) + NORM_EPS)
    hm = hm * out_norm_g.astype(f32).reshape(MLSTM_HEADS, 1, MLSTM_V_DIM)
    hm = hm.transpose(0, 2, 1, 3).reshape(B, S, MLSTM_V_W) * jax.nn.sigmoid(o_m.astype(f32))
    y_m = hm.astype(h.dtype) @ w_br_m
    qa = _partial_rope(heads(q_a, ATTN_Q_HEADS), positions)
    ka = _partial_rope(heads(k_a, ATTN_KV_HEADS), positions)
    va = heads(v_a, ATTN_KV_HEADS)
    oa = _swa_sinks(qa, ka, va, sinks).transpose(0, 2, 1, 3).reshape(B, S, ATTN_Q_W)
    y_a = oa @ w_br_a
    merged = jax.nn.sigmoid(g_m) * y_m + jax.nn.sigmoid(g_a) * y_a
    return merged @ w_out


def setup_inputs(seed: int = 0) -> dict:
    key = jax.random.key(seed)
    ks = jax.random.split(key, 24)
    nrm = lambda k, shape, fan_in: jax.random.normal(k, shape, jnp.float32) * (fan_in ** -0.5)
    gain = lambda k, n: 1.0 + 0.02 * jax.random.normal(k, (DEPTH, n), jnp.float32)
    x = jax.random.normal(ks[0], (BATCH, SEQ, D_MODEL), jnp.float32)
    start = jax.random.randint(ks[1], (BATCH, 1), 0, 4096, dtype=jnp.int32)
    positions = start + jnp.arange(SEQ, dtype=jnp.int32)[None, :]
    return {
        'x': x,
        'positions': positions,
        'ffn1_norm_g': gain(ks[2], D_MODEL),
        'ffn1_w_gate': nrm(ks[3], (DEPTH, D_MODEL, D_FF), D_MODEL),
        'ffn1_w_up': nrm(ks[4], (DEPTH, D_MODEL, D_FF), D_MODEL),
        'ffn1_w_down': nrm(ks[5], (DEPTH, D_FF, D_MODEL), D_FF),
        'mix_norm_g': gain(ks[6], D_MODEL),
        'w_in': nrm(ks[7], (DEPTH, D_MODEL, IN_WIDTH), D_MODEL),
        'mlstm_b_i': 0.1 * jax.random.normal(ks[8], (DEPTH, MLSTM_HEADS), jnp.float32),
        'mlstm_b_f': 3.0 + 0.5 * jax.random.normal(ks[9], (DEPTH, MLSTM_HEADS), jnp.float32),
        'mlstm_out_norm_g': gain(ks[10], MLSTM_V_W),
        'attn_sinks': 0.5 * jax.random.normal(ks[11], (DEPTH, ATTN_Q_HEADS), jnp.float32),
        'w_branch_mlstm': nrm(ks[12], (DEPTH, MLSTM_V_W, D_MODEL), MLSTM_V_W),
        'w_branch_attn': nrm(ks[13], (DEPTH, ATTN_Q_W, D_MODEL), ATTN_Q_W),
        'w_out': nrm(ks[14], (DEPTH, D_MODEL, D_MODEL), D_MODEL),
        'ffn2_norm_g': gain(ks[15], D_MODEL),
        'ffn2_w_gate': nrm(ks[16], (DEPTH, D_MODEL, D_FF), D_MODEL),
        'ffn2_w_up': nrm(ks[17], (DEPTH, D_MODEL, D_FF), D_MODEL),
        'ffn2_w_down': nrm(ks[18], (DEPTH, D_FF, D_MODEL), D_FF),
        'final_norm_g': 1.0 + 0.02 * jax.random.normal(ks[19], (D_MODEL,), jnp.float32),
    }


def reference(x, positions, ffn1_norm_g, ffn1_w_gate, ffn1_w_up, ffn1_w_down, mix_norm_g,
              w_in, mlstm_b_i, mlstm_b_f, mlstm_out_norm_g, attn_sinks, w_branch_mlstm,
              w_branch_attn, w_out, ffn2_norm_g, ffn2_w_gate, ffn2_w_up, ffn2_w_down,
              final_norm_g):
    for l in range(DEPTH):
        h = _rmsnorm(x, ffn1_norm_g[l])
        x = x + 0.5 * _swiglu(h, ffn1_w_gate[l], ffn1_w_up[l], ffn1_w_down[l])
        h = _rmsnorm(x, mix_norm_g[l])
        x = x + _mixer(h, positions, w_in[l], mlstm_b_i[l], mlstm_b_f[l], mlstm_out_norm_g[l],
                       attn_sinks[l], w_branch_mlstm[l], w_branch_attn[l], w_out[l])
        h = _rmsnorm(x, ffn2_norm_g[l])
        x = x + 0.5 * _swiglu(h, ffn2_w_gate[l], ffn2_w_up[l], ffn2_w_down[l])
    return _rmsnorm(x, final_norm_g)
```

```python
import functools

import jax
import jax.numpy as jnp
from jax import lax
from jax.experimental import pallas as pl
from jax.experimental.pallas import tpu as pltpu

D_MODEL = 1024
D_FF = 2816
MLSTM_HEADS = 4
MLSTM_QK_DIM = 64
MLSTM_V_DIM = 128
GATE_SOFTCAP = 15.0
ATTN_Q_HEADS = 8
ATTN_KV_HEADS = 2
ATTN_HEAD_DIM = 64
WINDOW = 128
ROPE_DIM = ATTN_HEAD_DIM // 4
ROPE_THETA = 500000.0
NORM_EPS = 1e-6

MLSTM_QK_W = MLSTM_HEADS * MLSTM_QK_DIM
MLSTM_V_W = MLSTM_HEADS * MLSTM_V_DIM
ATTN_Q_W = ATTN_Q_HEADS * ATTN_HEAD_DIM
ATTN_KV_W = ATTN_KV_HEADS * ATTN_HEAD_DIM
IN_WIDTHS = (MLSTM_QK_W, MLSTM_QK_W, MLSTM_V_W, MLSTM_V_W, MLSTM_HEADS, MLSTM_HEADS,
             ATTN_Q_W, ATTN_KV_W, ATTN_KV_W, D_MODEL, D_MODEL)

LANES = 128
MXU_COLS = 256
VMEM_BYTES_V7X = 64 * 1024 * 1024

FFN_ROWS = 512
FFN_COL_CHUNK = 2 * MXU_COLS
MIX_ROWS = 512
CHUNK = LANES
PAIR_W = 2 * MLSTM_QK_DIM
VEXT_W = 2 * MLSTM_V_DIM

F32 = jnp.float32
BF16 = jnp.bfloat16
_NT = (((1,), (1,)), ((), ()))


def _rms(x, g):
    return x * lax.rsqrt(jnp.mean(x * x, axis=-1, keepdims=True) + NORM_EPS) * g


def _dot(a, b):
    return jnp.dot(a, b, preferred_element_type=F32)


def _dot_nt(a, b):
    return lax.dot_general(a, b, _NT, preferred_element_type=F32)


def _col_chunks(total, width):
    return [(c, min(c + width, total)) for c in range(0, total, width)]


def _const_spec(shape):
    return pl.BlockSpec(shape, lambda *_: (0,) * len(shape), pipeline_mode=pl.Buffered(1))


def _ffn_body(*refs, final_norm):
    if final_norm:
        x_ref, g_ref, wg_ref, wu_ref, wd_ref, fg_ref, o_ref, act_ref = refs
    else:
        x_ref, g_ref, wg_ref, wu_ref, wd_ref, o_ref, act_ref = refs
    x = x_ref[...]
    h = _rms(x, g_ref[...]).astype(BF16)
    for c0, c1 in _col_chunks(D_FF, FFN_COL_CHUNK):
        g = _dot(h, wg_ref[:, c0:c1])
        u = _dot(h, wu_ref[:, c0:c1])
        act_ref[:, c0:c1] = (g * jax.nn.sigmoid(g) * u).astype(BF16)
    r = x + 0.5 * _dot(act_ref[...], wd_ref[...])
    if final_norm:
        r = _rms(r, fg_ref[...])
    o_ref[...] = r


def _ffn(x2, norm_g, w_gate, w_up, w_down, final_g=None):
    n, d = x2.shape
    final_norm = final_g is not None
    row_spec = pl.BlockSpec((FFN_ROWS, d), lambda i: (i, 0))
    in_specs = [row_spec, _const_spec((1, d)), _const_spec((d, D_FF)), _const_spec((d, D_FF)),
                _const_spec((D_FF, d))]
    args = [x2, norm_g.reshape(1, d), w_gate.astype(BF16), w_up.astype(BF16), w_down.astype(BF16)]
    if final_norm:
        in_specs.append(_const_spec((1, d)))
        args.append(final_g.reshape(1, d))
    return pl.pallas_call(
        functools.partial(_ffn_body, final_norm=final_norm),
        grid=(n // FFN_ROWS,),
        in_specs=in_specs,
        out_specs=row_spec,
        out_shape=jax.ShapeDtypeStruct((n, d), F32),
        scratch_shapes=[pltpu.VMEM((FFN_ROWS, D_FF), BF16)],
        compiler_params=pltpu.CompilerParams(
            dimension_semantics=("arbitrary",), vmem_limit_bytes=(3 * VMEM_BYTES_V7X) // 4),
        name="ffn_final" if final_norm else "ffn",
    )(*args)


def _softcap(a):
    return GATE_SOFTCAP * jnp.tanh(a / GATE_SOFTCAP)


def _log_sigmoid(a):
    return jnp.minimum(a, 0.0) - jnp.log1p(jnp.exp(-jnp.abs(a)))


def _lane_cumsum(a):
    lane = lax.broadcasted_iota(jnp.int32, a.shape, 1)
    d = 1
    while d < LANES:
        a = a + jnp.where(lane >= d, pltpu.roll(a, d, 1), 0.0)
        d *= 2
    return a


def _rope_tables(pos_row, inv_col):
    ang = inv_col * pos_row
    c, s = jnp.cos(ang), jnp.sin(ang)
    one, zero = jnp.ones_like(c), jnp.zeros_like(c)
    per_head = ATTN_HEAD_DIM // 8
    def tile(first, second, rest):
        rows = []
        for _ in range(LANES // ATTN_HEAD_DIM):
            rows += [first, second] + [rest] * (per_head - 2)
        return jnp.concatenate(rows, axis=0).T
    return tile(c, c, one), tile(zero, s, zero), tile(-s, zero, zero)


def _rope(a, tables):
    cos_t, sin_up, sin_dn = tables
    half = ROPE_DIM // 2
    return a * cos_t + pltpu.roll(a, half, 1) * sin_up + pltpu.roll(a, LANES - half, 1) * sin_dn


def _mixer_body(x_ref, pos_ref, ng_ref, inv_ref, bif_ref, ong_ref, sink_ref,
                wqk_ref, wkT_ref, wv_ref, wo_ref, wifT_ref, wqa_ref, wkva_ref, wgm_ref, wga_ref,
                wbrm_ref, wbra_ref, wout_ref,
                o_ref,
                h_scr, qk_scr, kT_scr, vext_scr, so_scr, qa_scr, kc_scr, kcs_scr, vc_scr, vcs_scr,
                hm_scr, oa_scr, mg_scr, c_scr, m_scr):
    t = pl.program_id(1)
    rows = MIX_ROWS
    n_chunks = rows // CHUNK
    n_blocks = rows // WINDOW

    @pl.when(t == 0)
    def _():
        c_scr[...] = jnp.zeros_like(c_scr)
        m_scr[...] = jnp.zeros_like(m_scr)
        for ref in (kc_scr, kcs_scr, vc_scr, vcs_scr):
            ref[0:WINDOW, :] = jnp.zeros((WINDOW, LANES), BF16)

    x = x_ref[0]
    h = _rms(x, ng_ref[...]).astype(BF16)
    h_scr[...] = h
    qk_scr[...] = _dot(h, wqk_ref[...]).astype(BF16)
    kT_scr[...] = _dot_nt(wkT_ref[...], h)
    v = _dot(h, wv_ref[...])
    for hd in range(MLSTM_HEADS):
        vext_scr[:, hd * VEXT_W:hd * VEXT_W + MLSTM_V_DIM] = (
            v[:, hd * MLSTM_V_DIM:(hd + 1) * MLSTM_V_DIM].astype(BF16))
        vext_scr[:, hd * VEXT_W + MLSTM_V_DIM:(hd + 1) * VEXT_W] = jnp.ones((rows, MLSTM_V_DIM), BF16)
    so_scr[...] = jax.nn.sigmoid(_dot(h, wo_ref[...]))
    gates = _softcap(_dot_nt(wifT_ref[...], h)[0:2 * MLSTM_HEADS] + bif_ref[...])

    tables = _rope_tables(pos_ref[0].astype(F32), inv_ref[...])
    qa = _dot(h, wqa_ref[...])
    for p in range(ATTN_Q_W // LANES):
        qa_scr[:, p * LANES:(p + 1) * LANES] = _rope(qa[:, p * LANES:(p + 1) * LANES], tables).astype(BF16)
    kva = _dot(h, wkva_ref[...])
    ka = _rope(kva[:, 0:ATTN_KV_W], tables)
    va = kva[:, ATTN_KV_W:2 * ATTN_KV_W]
    kc_scr[WINDOW:WINDOW + rows, :] = ka.astype(BF16)
    kcs_scr[WINDOW:WINDOW + rows, :] = pltpu.roll(ka, ATTN_HEAD_DIM, 1).astype(BF16)
    vc_scr[WINDOW:WINDOW + rows, :] = va.astype(BF16)
    vcs_scr[WINDOW:WINDOW + rows, :] = pltpu.roll(va, ATTN_HEAD_DIM, 1).astype(BF16)

    qi = lax.broadcasted_iota(jnp.int32, (CHUNK, CHUNK), 0)
    si = lax.broadcasted_iota(jnp.int32, (CHUNK, CHUNK), 1)
    tri = si <= qi
    lane_lo = lax.broadcasted_iota(jnp.int32, (CHUNK, PAIR_W), 1) < MLSTM_QK_DIM
    zeros_c = jnp.zeros((MLSTM_QK_DIM, VEXT_W), BF16)
    ong = ong_ref[...]
    m_vec = m_scr[0:MLSTM_HEADS, 0:1]
    for c in range(n_chunks):
        rs = slice(c * CHUNK, (c + 1) * CHUNK)
        g_c = gates[:, rs]
        i_r = g_c[0:MLSTM_HEADS]
        logf_r = _log_sigmoid(g_c[MLSTM_HEADS:2 * MLSTM_HEADS])
        b_r = _lane_cumsum(logf_r)
        b_last = b_r[:, CHUNK - 1:CHUNK]
        a_r = b_last - b_r + i_r
        a_max = jnp.max(a_r, axis=1, keepdims=True)
        m_new = jnp.maximum(b_last + m_vec, a_max)
        decay = jnp.exp(b_last + m_vec - m_new)
        scale = jnp.exp(a_max - m_new)
        w_loc = jnp.exp(a_r - a_max)
        ib_r = i_r - b_r
        for hd in range(MLSTM_HEADS):
            pair, odd = divmod(hd, 2)
            q2 = qk_scr[rs, pair * PAIR_W:(pair + 1) * PAIR_W]
            k2 = qk_scr[rs, MLSTM_QK_W + pair * PAIR_W:MLSTM_QK_W + (pair + 1) * PAIR_W]
            k_h = jnp.where(lane_lo != bool(odd), k2, jnp.zeros_like(k2))
            s_qk = _dot_nt(q2, k_h)
            b_col = jnp.sum(jnp.where(tri, logf_r[hd:hd + 1], 0.0), axis=1, keepdims=True)
            log_d = jnp.where(tri, b_col + ib_r[hd:hd + 1], -jnp.inf)
            inter_log = b_col + m_vec[hd:hd + 1]
            m_j = jnp.maximum(inter_log, jnp.max(log_d, axis=1, keepdims=True))
            sqk = (s_qk * jnp.exp(log_d - m_j)).astype(BF16)
            vext = vext_scr[rs, hd * VEXT_W:(hd + 1) * VEXT_W]
            c_prev = c_scr[hd]
            c_bf = c_prev.astype(BF16)
            c_pad = jnp.concatenate([zeros_c, c_bf] if odd else [c_bf, zeros_c], axis=0)
            tot = _dot(sqk, vext) + jnp.exp(inter_log - m_j) * _dot(q2, c_pad)
            num = tot[:, 0:MLSTM_V_DIM]
            den = tot[:, MLSTM_V_DIM:VEXT_W]
            hh = num / jnp.maximum(jnp.abs(den), jnp.exp(-m_j))
            vs = slice(hd * MLSTM_V_DIM, (hd + 1) * MLSTM_V_DIM)
            hn = hh * lax.rsqrt(jnp.mean(hh * hh, axis=1, keepdims=True) + NORM_EPS) * ong[:, vs]
            hm_scr[rs, vs] = (hn * so_scr[rs, vs]).astype(BF16)
            ktw = (kT_scr[hd * MLSTM_QK_DIM:(hd + 1) * MLSTM_QK_DIM, rs] * w_loc[hd:hd + 1]).astype(BF16)
            c_scr[hd] = decay[hd:hd + 1] * c_prev + scale[hd:hd + 1] * _dot(ktw, vext)
        m_vec = m_new
    m_scr[0:MLSTM_HEADS, :] = jnp.broadcast_to(m_vec, (MLSTM_HEADS, LANES))

    qi2 = lax.broadcasted_iota(jnp.int32, (WINDOW, 2 * WINDOW), 0)
    si2 = lax.broadcasted_iota(jnp.int32, (WINDOW, 2 * WINDOW), 1)
    band = (si2 > qi2) & (si2 <= qi2 + WINDOW)
    band_first = band & ((si2 >= WINDOW) | (t > 0))
    kv_lo = lax.broadcasted_iota(jnp.int32, (2 * WINDOW, LANES), 1) < ATTN_HEAD_DIM
    out_lo = lax.broadcasted_iota(jnp.int32, (2 * WINDOW, LANES), 1) < ATTN_HEAD_DIM
    ones_lo = jnp.where(kv_lo, 1.0, 0.0).astype(BF16)
    ones_hi = jnp.where(kv_lo, 0.0, 1.0).astype(BF16)
    group = ATTN_Q_HEADS // ATTN_KV_HEADS
    for nb in range(n_blocks):
        qs = slice(nb * WINDOW, (nb + 1) * WINDOW)
        ks = slice(nb * WINDOW, (nb + 2) * WINDOW)
        mask = band_first if nb == 0 else band
        for kh in range(ATTN_KV_HEADS):
            own, swapped = (kc_scr, kcs_scr) if kh == 0 else (kcs_scr, kc_scr)
            vown, vswapped = (vc_scr, vcs_scr) if kh == 0 else (vcs_scr, vc_scr)
            zk = jnp.zeros((2 * WINDOW, LANES), BF16)
            k_lo = jnp.where(kv_lo, own[ks, :], zk)
            k_hi = jnp.where(kv_lo, zk, swapped[ks, :])
            v_lo = jnp.where(kv_lo, vown[ks, :], zk)
            v_hi = jnp.where(kv_lo, zk, vswapped[ks, :])
            pair0 = kh * (group // 2)
            lhs = jnp.concatenate([qa_scr[qs, (pair0 + j) * LANES:(pair0 + j + 1) * LANES]
                                   for j in range(group // 2)], axis=0)
            probs, sink_terms = [], []
            for odd, k_sel in enumerate((k_lo, k_hi)):
                s_all = _dot_nt(lhs, k_sel)
                p_rows, sk_rows = [], []
                for j in range(group // 2):
                    sink = sink_ref[kh * group + 2 * j + odd]
                    s = jnp.where(mask, s_all[j * WINDOW:(j + 1) * WINDOW], -jnp.inf)
                    mx = jnp.maximum(jnp.max(s, axis=1, keepdims=True), sink)
                    p_rows.append(jnp.exp(s - mx).astype(BF16))
                    sk_rows.append(jnp.exp(sink - mx))
                probs.append(jnp.concatenate(p_rows, axis=0))
                sink_terms.append(jnp.concatenate(sk_rows, axis=0))
            rhs = jnp.concatenate([jnp.concatenate([v_lo, ones_lo], axis=1),
                                   jnp.concatenate([v_hi, ones_hi], axis=1)], axis=0)
            o = _dot(jnp.concatenate(probs, axis=1), rhs)
            denom = o[:, LANES:2 * LANES] + jnp.where(out_lo, sink_terms[0], sink_terms[1])
            on = (o[:, 0:LANES] / denom).astype(BF16)
            for j in range(group // 2):
                oa_scr[qs, (pair0 + j) * LANES:(pair0 + j + 1) * LANES] = on[j * WINDOW:(j + 1) * WINDOW]
    for ref in (kc_scr, kcs_scr, vc_scr, vcs_scr):
        ref[0:WINDOW, :] = ref[rows:rows + WINDOW, :]

    hm = hm_scr[...]
    oa = oa_scr[...]
    for c0, c1 in _col_chunks(D_MODEL, 2 * MXU_COLS):
        y_m = _dot(hm, wbrm_ref[:, c0:c1])
        y_a = _dot(oa, wbra_ref[:, c0:c1])
        g_m = jax.nn.sigmoid(_dot(h_scr[...], wgm_ref[:, c0:c1]))
        g_a = jax.nn.sigmoid(_dot(h_scr[...], wga_ref[:, c0:c1]))
        mg_scr[:, c0:c1] = (g_m * y_m + g_a * y_a).astype(BF16)
    o_ref[0] = x_ref[0] + _dot(mg_scr[...], wout_ref[...])


def _mixer(x, positions, norm_g, w_in, b_i, b_f, out_norm_g, sinks, w_br_m, w_br_a, w_out):
    b, s, d = x.shape
    starts = [0]
    for w in IN_WIDTHS:
        starts.append(starts[-1] + w)
    col = lambda i: w_in[:, starts[i]:starts[i + 1]]
    q_m, k_m, v_m, o_m, i_m, f_m, q_a, k_a, v_a, g_m, g_a = (col(i) for i in range(len(IN_WIDTHS)))
    w_qk = jnp.concatenate([q_m * (MLSTM_QK_DIM ** -0.5), k_m], axis=1).astype(BF16)
    w_kT = k_m.T.astype(BF16)
    w_ifT = jnp.concatenate([i_m, f_m, jnp.zeros((d, 16 - 2 * MLSTM_HEADS), F32)], axis=1).T.astype(BF16)
    w_qa = (q_a * (ATTN_HEAD_DIM ** -0.5)).astype(BF16)
    w_kva = jnp.concatenate([k_a, v_a], axis=1).astype(BF16)
    b_if = jnp.concatenate([b_i, b_f]).reshape(2 * MLSTM_HEADS, 1)
    half = ROPE_DIM // 2
    inv = (ROPE_THETA ** (-jnp.arange(half, dtype=F32) * 2.0 / ROPE_DIM)).reshape(half, 1)

    rows = MIX_ROWS
    tile = pl.BlockSpec((1, rows, d), lambda bi, ti: (bi, ti, 0))
    in_specs = [
        tile,
        pl.BlockSpec((1, 1, rows), lambda bi, ti: (bi, 0, ti)),
        _const_spec((1, d)), _const_spec((half, 1)), _const_spec((2 * MLSTM_HEADS, 1)),
        _const_spec((1, MLSTM_V_W)),
        pl.BlockSpec(memory_space=pltpu.SMEM),
        _const_spec((d, 2 * MLSTM_QK_W)), _const_spec((MLSTM_QK_W, d)), _const_spec((d, MLSTM_V_W)),
        _const_spec((d, MLSTM_V_W)), _const_spec((16, d)), _const_spec((d, ATTN_Q_W)),
        _const_spec((d, 2 * ATTN_KV_W)), _const_spec((d, d)), _const_spec((d, d)),
        _const_spec((MLSTM_V_W, d)), _const_spec((ATTN_Q_W, d)), _const_spec((d, d)),
    ]
    scratch = [
        pltpu.VMEM((rows, d), BF16),
        pltpu.VMEM((rows, 2 * MLSTM_QK_W), BF16),
        pltpu.VMEM((MLSTM_QK_W, rows), F32),
        pltpu.VMEM((rows, MLSTM_HEADS * VEXT_W), BF16),
        pltpu.VMEM((rows, MLSTM_V_W), F32),
        pltpu.VMEM((rows, ATTN_Q_W), BF16),
        pltpu.VMEM((WINDOW + rows, LANES), BF16),
        pltpu.VMEM((WINDOW + rows, LANES), BF16),
        pltpu.VMEM((WINDOW + rows, LANES), BF16),
        pltpu.VMEM((WINDOW + rows, LANES), BF16),
        pltpu.VMEM((rows, MLSTM_V_W), BF16),
        pltpu.VMEM((rows, ATTN_Q_W), BF16),
        pltpu.VMEM((rows, d), BF16),
        pltpu.VMEM((MLSTM_HEADS, MLSTM_QK_DIM, VEXT_W), F32),
        pltpu.VMEM((8, LANES), F32),
    ]
    return pl.pallas_call(
        _mixer_body,
        grid=(b, s // rows),
        in_specs=in_specs,
        out_specs=tile,
        out_shape=jax.ShapeDtypeStruct((b, s, d), F32),
        scratch_shapes=scratch,
        compiler_params=pltpu.CompilerParams(
            dimension_semantics=("arbitrary", "arbitrary"), vmem_limit_bytes=(3 * VMEM_BYTES_V7X) // 4),
        name="mixer",
    )(x, positions.reshape(b, 1, s), norm_g.reshape(1, d), inv, b_if, out_norm_g.reshape(1, MLSTM_V_W),
      sinks, w_qk, w_kT, v_m.astype(BF16), o_m.astype(BF16), w_ifT, w_qa, w_kva,
      g_m.astype(BF16), g_a.astype(BF16), w_br_m.astype(BF16), w_br_a.astype(BF16), w_out.astype(BF16))


def kernel(x, positions, ffn1_norm_g, ffn1_w_gate, ffn1_w_up, ffn1_w_down, mix_norm_g, w_in, mlstm_b_i, mlstm_b_f, mlstm_out_norm_g, attn_sinks, w_branch_mlstm, w_branch_attn, w_out, ffn2_norm_g, ffn2_w_gate, ffn2_w_up, ffn2_w_down, final_norm_g):
    b, s, d = x.shape
    depth = w_in.shape[0]
    for l in range(depth):
        last = l == depth - 1
        x = _ffn(x.reshape(b * s, d), ffn1_norm_g[l], ffn1_w_gate[l], ffn1_w_up[l], ffn1_w_down[l]).reshape(b, s, d)
        x = _mixer(x, positions, mix_norm_g[l], w_in[l], mlstm_b_i[l], mlstm_b_f[l], mlstm_out_norm_g[l],
                   attn_sinks[l], w_branch_mlstm[l], w_branch_attn[l], w_out[l])
        x = _ffn(x.reshape(b * s, d), ffn2_norm_g[l], ffn2_w_gate[l], ffn2_w_up[l], ffn2_w_down[l],
                 final_g=final_norm_g if last else None).reshape(b, s, d)
    return x
```

```python
import collections
import functools

import jax
import jax.numpy as jnp
from jax import lax
from jax.experimental import pallas as pl
from jax.experimental.pallas import tpu as pltpu

D_MODEL = 1024
D_FF = 2816
MLSTM_HEADS = 4
MLSTM_QK_DIM = 64
MLSTM_V_DIM = 128
GATE_SOFTCAP = 15.0
ATTN_Q_HEADS = 8
ATTN_KV_HEADS = 2
ATTN_HEAD_DIM = 64
WINDOW = 128
ROPE_DIM = ATTN_HEAD_DIM // 4
ROPE_THETA = 500000.0
NORM_EPS = 1e-6

MLSTM_QK_W = MLSTM_HEADS * MLSTM_QK_DIM
MLSTM_V_W = MLSTM_HEADS * MLSTM_V_DIM
ATTN_Q_W = ATTN_Q_HEADS * ATTN_HEAD_DIM
ATTN_KV_W = ATTN_KV_HEADS * ATTN_HEAD_DIM
IN_WIDTHS = (MLSTM_QK_W, MLSTM_QK_W, MLSTM_V_W, MLSTM_V_W, MLSTM_HEADS, MLSTM_HEADS,
             ATTN_Q_W, ATTN_KV_W, ATTN_KV_W, D_MODEL, D_MODEL)

LANES = 128
MXU_COLS = 256
VMEM_BYTES_V7X = 64 * 1024 * 1024

FFN_ROWS = 512
FFN_COL_CHUNK = 2 * MXU_COLS
MIX_ROWS = 512
MIX_COL_CHUNK = 2 * MXU_COLS
CHUNK = LANES
PAIR_W = 2 * MLSTM_QK_DIM
VEXT_W = 2 * MLSTM_V_DIM
GATE_ROWS = 2 * MLSTM_HEADS

F32 = jnp.float32
BF16 = jnp.bfloat16
_NT = (((1,), (1,)), ((), ()))


def _rms(x, g):
    return x * lax.rsqrt(jnp.mean(x * x, axis=-1, keepdims=True) + NORM_EPS) * g


def _col_chunks(total, width):
    return [(c, min(c + width, total)) for c in range(0, total, width)]


def _const_spec(shape):
    return pl.BlockSpec(shape, lambda *_: (0,) * len(shape), pipeline_mode=pl.Buffered(1))


def _ffn_body(*refs, final_norm):
    if final_norm:
        x_ref, g_ref, wg_ref, wu_ref, wd_ref, fg_ref, o_ref, act_ref = refs
    else:
        x_ref, g_ref, wg_ref, wu_ref, wd_ref, o_ref, act_ref = refs
    x = x_ref[...]
    h = _rms(x, g_ref[...]).astype(BF16)
    for c0, c1 in _col_chunks(D_FF, FFN_COL_CHUNK):
        g = jnp.dot(h, wg_ref[:, c0:c1], preferred_element_type=F32)
        u = jnp.dot(h, wu_ref[:, c0:c1], preferred_element_type=F32)
        act_ref[:, c0:c1] = (g * jax.nn.sigmoid(g) * u).astype(BF16)
    r = x + 0.5 * jnp.dot(act_ref[...], wd_ref[...], preferred_element_type=F32)
    if final_norm:
        r = _rms(r, fg_ref[...])
    o_ref[...] = r


def _ffn(x2, norm_g, w_gate, w_up, w_down, final_g=None):
    n, d = x2.shape
    final_norm = final_g is not None
    row_spec = pl.BlockSpec((FFN_ROWS, d), lambda i: (i, 0))
    in_specs = [row_spec, _const_spec((1, d)), _const_spec((d, D_FF)), _const_spec((d, D_FF)),
                _const_spec((D_FF, d))]
    args = [x2, norm_g.reshape(1, d), w_gate.astype(BF16), w_up.astype(BF16), w_down.astype(BF16)]
    if final_norm:
        in_specs.append(_const_spec((1, d)))
        args.append(final_g.reshape(1, d))
    return pl.pallas_call(
        functools.partial(_ffn_body, final_norm=final_norm),
        grid=(n // FFN_ROWS,),
        in_specs=in_specs,
        out_specs=row_spec,
        out_shape=jax.ShapeDtypeStruct((n, d), F32),
        scratch_shapes=[pltpu.VMEM((FFN_ROWS, D_FF), BF16)],
        compiler_params=pltpu.CompilerParams(
            dimension_semantics=("arbitrary",), vmem_limit_bytes=(3 * VMEM_BYTES_V7X) // 4),
        name="ffn_final" if final_norm else "ffn",
    )(*args)


Slot = collections.namedtuple("Slot", "xs h qk kT vext so qa kc kcs vc vcs g")


def _slot_shapes(rows, d):
    return Slot(
        xs=pltpu.VMEM((rows, d), F32),
        h=pltpu.VMEM((rows, d), BF16),
        qk=pltpu.VMEM((rows, 2 * MLSTM_QK_W), BF16),
        kT=pltpu.VMEM((MLSTM_QK_W, rows), F32),
        vext=pltpu.VMEM((rows, MLSTM_HEADS * VEXT_W), BF16),
        so=pltpu.VMEM((rows, MLSTM_V_W), F32),
        qa=pltpu.VMEM((rows, ATTN_Q_W), BF16),
        kc=pltpu.VMEM((rows, LANES), BF16),
        kcs=pltpu.VMEM((rows, LANES), BF16),
        vc=pltpu.VMEM((rows, LANES), BF16),
        vcs=pltpu.VMEM((rows, LANES), BF16),
        g=pltpu.VMEM((GATE_ROWS, rows), F32),
    )


def _softcap(a):
    return GATE_SOFTCAP * jnp.tanh(a / GATE_SOFTCAP)


def _log_sigmoid(a):
    return jnp.minimum(a, 0.0) - jnp.log1p(jnp.exp(-jnp.abs(a)))


def _lane_cumsum(a):
    lane = lax.broadcasted_iota(jnp.int32, a.shape, 1)
    d = 1
    while d < LANES:
        a = a + jnp.where(lane >= d, pltpu.roll(a, d, 1), 0.0)
        d *= 2
    return a


def _rope_tables(pos_row, inv_col):
    ang = inv_col * pos_row
    c, s = jnp.cos(ang), jnp.sin(ang)
    one, zero = jnp.ones_like(c), jnp.zeros_like(c)
    per_head = ATTN_HEAD_DIM // 8

    def tile(first, second, rest):
        rows = []
        for _ in range(LANES // ATTN_HEAD_DIM):
            rows += [first, second] + [rest] * (per_head - 2)
        return jnp.concatenate(rows, axis=0).T
    return tile(c, c, one), tile(zero, s, zero), tile(-s, zero, zero)


def _rope(a, tables):
    cos_t, sin_up, sin_dn = tables
    half = ROPE_DIM // 2
    return a * cos_t + pltpu.roll(a, half, 1) * sin_up + pltpu.roll(a, LANES - half, 1) * sin_dn


def _rep(a, shape):
    return jnp.broadcast_to(a, shape)


def _mixer_step(t_cur, ins, cur, nxt, shared, o_ref):
    (x_ref, pos_ref, ng_ref, inv_ref, bif_ref, ong_ref, sink_ref,
     wqk_ref, wkT_ref, wv_ref, wo_ref, wifT_ref, wqa_ref, wkva_ref, wgm_ref, wga_ref,
     wbrm_ref, wbra_ref, wout_ref) = ins
    pk_scr, pks_scr, pv_scr, pvs_scr, sgm_scr, sga_scr, hm_scr, oa_scr, mg_scr, c_scr, m_scr = shared
    rows = MIX_ROWS
    n_chunks = rows // CHUNK
    n_blocks = rows // WINDOW
    group = ATTN_Q_HEADS // ATTN_KV_HEADS

    shared_vals = {}

    def p_norm():
        x_new = x_ref[0]
        nxt.xs[...] = x_new
        nxt.h[...] = _rms(x_new, ng_ref[...]).astype(BF16)

    def p_qk():
        nxt.qk[...] = jnp.dot(nxt.h[...], wqk_ref[...], preferred_element_type=F32).astype(BF16)

    def p_kT():
        nxt.kT[...] = lax.dot_general(wkT_ref[...], nxt.h[...], _NT, preferred_element_type=F32)

    def p_v():
        v = jnp.dot(nxt.h[...], wv_ref[...], preferred_element_type=F32)
        for hd in range(MLSTM_HEADS):
            nxt.vext[:, hd * VEXT_W:hd * VEXT_W + MLSTM_V_DIM] = (
                v[:, hd * MLSTM_V_DIM:(hd + 1) * MLSTM_V_DIM].astype(BF16))
            nxt.vext[:, hd * VEXT_W + MLSTM_V_DIM:(hd + 1) * VEXT_W] = jnp.ones((rows, MLSTM_V_DIM), BF16)

    def p_so():
        nxt.so[...] = jax.nn.sigmoid(jnp.dot(nxt.h[...], wo_ref[...], preferred_element_type=F32))

    def p_gates():
        gt = lax.dot_general(wifT_ref[...], nxt.h[...], _NT, preferred_element_type=F32)
        nxt.g[...] = _softcap(gt[0:GATE_ROWS] + bif_ref[...])

    def p_qa():
        shared_vals["tables"] = _rope_tables(pos_ref[0].astype(F32), inv_ref[...])
        qa = jnp.dot(nxt.h[...], wqa_ref[...], preferred_element_type=F32)
        for p in range(ATTN_Q_W // LANES):
            nxt.qa[:, p * LANES:(p + 1) * LANES] = _rope(
                qa[:, p * LANES:(p + 1) * LANES], shared_vals["tables"]).astype(BF16)

    def p_kva():
        kva = jnp.dot(nxt.h[...], wkva_ref[...], preferred_element_type=F32)
        ka = _rope(kva[:, 0:ATTN_KV_W], shared_vals["tables"])
        va = kva[:, ATTN_KV_W:2 * ATTN_KV_W]
        nxt.kc[...] = ka.astype(BF16)
        nxt.kcs[...] = pltpu.roll(ka, ATTN_HEAD_DIM, 1).astype(BF16)
        nxt.vc[...] = va.astype(BF16)
        nxt.vcs[...] = pltpu.roll(va, ATTN_HEAD_DIM, 1).astype(BF16)

    def gate_piece(idx):
        gref, wref = (sgm_scr, wgm_ref) if idx < 2 else (sga_scr, wga_ref)
        c0 = (idx % 2) * MIX_COL_CHUNK
        gref[:, c0:c0 + MIX_COL_CHUNK] = jax.nn.sigmoid(
            jnp.dot(cur.h[...], wref[:, c0:c0 + MIX_COL_CHUNK], preferred_element_type=F32))

    qi = lax.broadcasted_iota(jnp.int32, (CHUNK, CHUNK), 0)
    si = lax.broadcasted_iota(jnp.int32, (CHUNK, CHUNK), 1)
    tri = si <= qi
    lane_lo = lax.broadcasted_iota(jnp.int32, (CHUNK, PAIR_W), 1) < MLSTM_QK_DIM
    zeros_c = jnp.zeros((MLSTM_QK_DIM, VEXT_W), BF16)
    ong = ong_ref[...]
    tile_shape = (CHUNK, CHUNK)
    rep_shape = (MLSTM_HEADS, LANES)

    chunk_stats = []
    m_vec = m_scr[0:MLSTM_HEADS, :]
    for c in range(n_chunks):
        g_c = cur.g[:, c * CHUNK:(c + 1) * CHUNK]
        i_r = g_c[0:MLSTM_HEADS]
        logf_r = _log_sigmoid(g_c[MLSTM_HEADS:GATE_ROWS])
        b_r = _lane_cumsum(logf_r)
        b_last = _rep(b_r[:, CHUNK - 1:CHUNK], rep_shape)
        a_r = b_last - b_r + i_r
        a_max = _rep(jnp.max(a_r, axis=1, keepdims=True), rep_shape)
        m_new = jnp.maximum(b_last + m_vec, a_max)
        chunk_stats.append(dict(
            logf_r=logf_r, ib_r=i_r - b_r, m_prev=m_vec,
            decay=jnp.exp(b_last + m_vec - m_new), scale=jnp.exp(a_max - m_new),
            w_loc=jnp.exp(a_r - a_max)))
        m_vec = m_new
    c_state = [c_scr[hd] for hd in range(MLSTM_HEADS)]
    c_next = [None] * MLSTM_HEADS
    s_qk = {}

    def mlstm_scores(c):
        rs = slice(c * CHUNK, (c + 1) * CHUNK)
        for pair in range(MLSTM_HEADS // 2):
            q2 = cur.qk[rs, pair * PAIR_W:(pair + 1) * PAIR_W]
            k2 = cur.qk[rs, MLSTM_QK_W + pair * PAIR_W:MLSTM_QK_W + (pair + 1) * PAIR_W]
            zk = jnp.zeros_like(k2)
            k_both = jnp.concatenate([jnp.where(lane_lo, k2, zk), jnp.where(lane_lo, zk, k2)], axis=0)
            s2 = lax.dot_general(q2, k_both, _NT, preferred_element_type=F32)
            s_qk[c, 2 * pair] = s2[:, 0:CHUNK]
            s_qk[c, 2 * pair + 1] = s2[:, CHUNK:2 * CHUNK]

    def mlstm_state(c):
        rs = slice(c * CHUNK, (c + 1) * CHUNK)
        st = chunk_stats[c]
        for hd in range(MLSTM_HEADS):
            vext = cur.vext[rs, hd * VEXT_W:(hd + 1) * VEXT_W]
            ktw = (cur.kT[hd * MLSTM_QK_DIM:(hd + 1) * MLSTM_QK_DIM, rs] * st["w_loc"][hd:hd + 1]).astype(BF16)
            u = jnp.dot(ktw, vext, preferred_element_type=F32)
            decay = jnp.concatenate([st["decay"][hd:hd + 1]] * 2, axis=1)
            scale = jnp.concatenate([st["scale"][hd:hd + 1]] * 2, axis=1)
            c_next[hd] = decay * c_state[hd] + scale * u

    def mlstm_out(c):
        rs = slice(c * CHUNK, (c + 1) * CHUNK)
        st = chunk_stats[c]
        for hd in range(MLSTM_HEADS):
            pair, odd = divmod(hd, 2)
            q2 = cur.qk[rs, pair * PAIR_W:(pair + 1) * PAIR_W]
            vext = cur.vext[rs, hd * VEXT_W:(hd + 1) * VEXT_W]
            b_col = _rep(jnp.sum(jnp.where(tri, st["logf_r"][hd:hd + 1], 0.0), axis=1, keepdims=True),
                         tile_shape)
            log_d = jnp.where(tri, b_col + st["ib_r"][hd:hd + 1], -jnp.inf)
            inter_log = b_col + st["m_prev"][hd:hd + 1]
            m_j = jnp.maximum(inter_log, _rep(jnp.max(log_d, axis=1, keepdims=True), tile_shape))
            sqk = (s_qk.pop((c, hd)) * jnp.exp(log_d - m_j)).astype(BF16)
            c_bf = c_state[hd].astype(BF16)
            c_pad = jnp.concatenate([zeros_c, c_bf] if odd else [c_bf, zeros_c], axis=0)
            intra = jnp.dot(sqk, vext, preferred_element_type=F32)
            inter = jnp.dot(q2, c_pad, preferred_element_type=F32)
            inter_scale = jnp.exp(inter_log - m_j)
            num = intra[:, 0:MLSTM_V_DIM] + inter_scale * inter[:, 0:MLSTM_V_DIM]
            den = intra[:, MLSTM_V_DIM:VEXT_W] + inter_scale * inter[:, MLSTM_V_DIM:VEXT_W]
            hh = num / jnp.maximum(jnp.abs(den), jnp.exp(-m_j))
            vs = slice(hd * MLSTM_V_DIM, (hd + 1) * MLSTM_V_DIM)
            hn = hh * lax.rsqrt(jnp.mean(hh * hh, axis=1, keepdims=True) + NORM_EPS) * ong[:, vs]
            hm_scr[rs, vs] = (hn * cur.so[rs, vs]).astype(BF16)
            c_state[hd] = c_next[hd]

    qi2 = lax.broadcasted_iota(jnp.int32, (WINDOW, 2 * WINDOW), 0)
    si2 = lax.broadcasted_iota(jnp.int32, (WINDOW, 2 * WINDOW), 1)
    band = (si2 > qi2) & (si2 <= qi2 + WINDOW)
    band_first = band & ((si2 >= WINDOW) | (t_cur > 0))
    kv_lo = lax.broadcasted_iota(jnp.int32, (2 * WINDOW, LANES), 1) < ATTN_HEAD_DIM
    ones_lo = jnp.where(kv_lo, 1.0, 0.0).astype(BF16)
    ones_hi = jnp.where(kv_lo, 0.0, 1.0).astype(BF16)
    zkv = jnp.zeros((2 * WINDOW, LANES), BF16)
    attn = {}

    def window_rows(tile_ref, prev_ref, nb):
        if nb == 0:
            return jnp.concatenate([prev_ref[...], tile_ref[0:WINDOW, :]], axis=0)
        return tile_ref[(nb - 1) * WINDOW:(nb + 1) * WINDOW, :]

    def attn_scores(nb):
        qs = slice(nb * WINDOW, (nb + 1) * WINDOW)
        mask = band_first if nb == 0 else band
        k_own, k_sw = window_rows(cur.kc, pk_scr, nb), window_rows(cur.kcs, pks_scr, nb)
        for kh in range(ATTN_KV_HEADS):
            own, swapped = (k_own, k_sw) if kh == 0 else (k_sw, k_own)
            k_lo = jnp.where(kv_lo, own, zkv)
            k_hi = jnp.where(kv_lo, zkv, swapped)
            pair0 = kh * (group // 2)
            lhs = jnp.concatenate([cur.qa[qs, (pair0 + j) * LANES:(pair0 + j + 1) * LANES]
                                   for j in range(group // 2)], axis=0)
            probs, sink_terms = [], []
            for odd, k_sel in enumerate((k_lo, k_hi)):
                s_all = lax.dot_general(lhs, k_sel, _NT, preferred_element_type=F32)
                p_rows, sk_rows = [], []
                for j in range(group // 2):
                    sink = sink_ref[kh * group + 2 * j + odd]
                    sc = jnp.where(mask, s_all[j * WINDOW:(j + 1) * WINDOW], -jnp.inf)
                    mx = jnp.maximum(jnp.max(sc, axis=1, keepdims=True), sink)
                    p_rows.append(jnp.exp(sc - mx).astype(BF16))
                    sk_rows.append(_rep(jnp.exp(sink - mx), (WINDOW, LANES)))
                probs.append(jnp.concatenate(p_rows, axis=0))
                sink_terms.append(jnp.concatenate(sk_rows, axis=0))
            attn[nb, kh] = (jnp.concatenate(probs, axis=1), jnp.where(kv_lo, sink_terms[0], sink_terms[1]))

    def attn_values(nb):
        qs = slice(nb * WINDOW, (nb + 1) * WINDOW)
        v_own, v_sw = window_rows(cur.vc, pv_scr, nb), window_rows(cur.vcs, pvs_scr, nb)
        for kh in range(ATTN_KV_HEADS):
            own, swapped = (v_own, v_sw) if kh == 0 else (v_sw, v_own)
            v_lo = jnp.where(kv_lo, own, zkv)
            v_hi = jnp.where(kv_lo, zkv, swapped)
            probs, sink_term = attn.pop((nb, kh))
            rhs = jnp.concatenate([jnp.concatenate([v_lo, ones_lo], axis=1),
                                   jnp.concatenate([v_hi, ones_hi], axis=1)], axis=0)
            o = jnp.dot(probs, rhs, preferred_element_type=F32)
            on = (o[:, 0:LANES] / (o[:, LANES:2 * LANES] + sink_term)).astype(BF16)
            pair0 = kh * (group // 2)
            for j in range(group // 2):
                oa_scr[qs, (pair0 + j) * LANES:(pair0 + j + 1) * LANES] = on[j * WINDOW:(j + 1) * WINDOW]

    gate_piece(0)
    p_norm()
    mlstm_scores(0); attn_scores(0); mlstm_state(0)
    gate_piece(1)
    mlstm_out(0)
    p_qk()
    attn_values(0); mlstm_scores(1); mlstm_state(1)
    p_kT()
    attn_scores(1); mlstm_out(1)
    p_v()
    attn_values(1); mlstm_scores(2); mlstm_state(2)
    p_so(); gate_piece(2)
    attn_scores(2); mlstm_out(2)
    p_qa()
    attn_values(2); mlstm_scores(3); mlstm_state(3)
    p_kva(); p_gates(); gate_piece(3)
    attn_scores(3); mlstm_out(3)
    attn_values(3)

    for hd in range(MLSTM_HEADS):
        c_scr[hd] = c_state[hd]
    m_scr[0:MLSTM_HEADS, :] = m_vec
    for prev_ref, tile_ref in ((pk_scr, cur.kc), (pks_scr, cur.kcs), (pv_scr, cur.vc), (pvs_scr, cur.vcs)):
        prev_ref[...] = tile_ref[rows - WINDOW:rows, :]

    hm = hm_scr[...]
    oa = oa_scr[...]
    for c0, c1 in _col_chunks(D_MODEL, MIX_COL_CHUNK):
        y_m = jnp.dot(hm, wbrm_ref[:, c0:c1], preferred_element_type=F32)
        y_a = jnp.dot(oa, wbra_ref[:, c0:c1], preferred_element_type=F32)
        mg_scr[:, c0:c1] = (sgm_scr[:, c0:c1] * y_m + sga_scr[:, c0:c1] * y_a).astype(BF16)
    o_ref[0] = cur.xs[...] + jnp.dot(mg_scr[...], wout_ref[...], preferred_element_type=F32)


def _mixer_body(*refs, tiles_per_seq):
    n_in = 19
    n_slot = len(Slot._fields)
    ins = refs[:n_in]
    o_ref = refs[n_in]
    scr = refs[n_in + 1:]
    slot_a, slot_b = Slot(*scr[:n_slot]), Slot(*scr[n_slot:2 * n_slot])
    shared = scr[2 * n_slot:]
    pk_scr, pks_scr, pv_scr, pvs_scr = shared[0:4]
    c_scr, m_scr = shared[-2:]
    s = pl.program_id(0)
    t_cur = lax.rem(s + tiles_per_seq - 1, tiles_per_seq)

    @pl.when(s == 0)
    def _():
        for ref in list(slot_b) + list(shared):
            ref[...] = jnp.zeros_like(ref)

    @pl.when(t_cur == 0)
    def _():
        c_scr[...] = jnp.zeros_like(c_scr)
        m_scr[...] = jnp.zeros_like(m_scr)
        for ref in (pk_scr, pks_scr, pv_scr, pvs_scr):
            ref[...] = jnp.zeros_like(ref)

    even = lax.rem(s, 2) == 0

    @pl.when(even)
    def _():
        _mixer_step(t_cur, ins, slot_b, slot_a, shared, o_ref)

    @pl.when(jnp.logical_not(even))
    def _():
        _mixer_step(t_cur, ins, slot_a, slot_b, shared, o_ref)


def _mixer(x, positions, norm_g, w_in, b_i, b_f, out_norm_g, sinks, w_br_m, w_br_a, w_out):
    b, s, d = x.shape
    starts = [0]
    for w in IN_WIDTHS:
        starts.append(starts[-1] + w)
    col = lambda i: w_in[:, starts[i]:starts[i + 1]]
    q_m, k_m, v_m, o_m, i_m, f_m, q_a, k_a, v_a, g_m, g_a = (col(i) for i in range(len(IN_WIDTHS)))
    w_qk = jnp.concatenate([q_m * (MLSTM_QK_DIM ** -0.5), k_m], axis=1).astype(BF16)
    w_kT = k_m.T.astype(BF16)
    w_ifT = jnp.concatenate([i_m, f_m, jnp.zeros((d, 16 - GATE_ROWS), F32)], axis=1).T.astype(BF16)
    w_qa = (q_a * (ATTN_HEAD_DIM ** -0.5)).astype(BF16)
    w_kva = jnp.concatenate([k_a, v_a], axis=1).astype(BF16)
    b_if = jnp.concatenate([b_i, b_f]).reshape(GATE_ROWS, 1)
    half = ROPE_DIM // 2
    inv = (ROPE_THETA ** (-jnp.arange(half, dtype=F32) * 2.0 / ROPE_DIM)).reshape(half, 1)

    rows = MIX_ROWS
    tiles_per_seq = s // rows
    n_tiles = b * tiles_per_seq
    def in_tile(i):
        j = jnp.minimum(i, n_tiles - 1)
        return j // tiles_per_seq, j % tiles_per_seq

    def out_tile(i):
        j = jnp.maximum(i - 1, 0)
        return j // tiles_per_seq, j % tiles_per_seq

    in_specs = [
        pl.BlockSpec((1, rows, d), lambda i: (*in_tile(i), 0)),
        pl.BlockSpec((1, 1, rows), lambda i: (in_tile(i)[0], 0, in_tile(i)[1])),
        _const_spec((1, d)), _const_spec((half, 1)), _const_spec((GATE_ROWS, 1)),
        _const_spec((1, MLSTM_V_W)),
        pl.BlockSpec(memory_space=pltpu.SMEM),
        _const_spec((d, 2 * MLSTM_QK_W)), _const_spec((MLSTM_QK_W, d)), _const_spec((d, MLSTM_V_W)),
        _const_spec((d, MLSTM_V_W)), _const_spec((16, d)), _const_spec((d, ATTN_Q_W)),
        _const_spec((d, 2 * ATTN_KV_W)), _const_spec((d, d)), _const_spec((d, d)),
        _const_spec((MLSTM_V_W, d)), _const_spec((ATTN_Q_W, d)), _const_spec((d, d)),
    ]
    slot = list(_slot_shapes(rows, d))
    shared = [
        pltpu.VMEM((WINDOW, LANES), BF16),
        pltpu.VMEM((WINDOW, LANES), BF16),
        pltpu.VMEM((WINDOW, LANES), BF16),
        pltpu.VMEM((WINDOW, LANES), BF16),
        pltpu.VMEM((rows, d), F32),
        pltpu.VMEM((rows, d), F32),
        pltpu.VMEM((rows, MLSTM_V_W), BF16),
        pltpu.VMEM((rows, ATTN_Q_W), BF16),
        pltpu.VMEM((rows, d), BF16),
        pltpu.VMEM((MLSTM_HEADS, MLSTM_QK_DIM, VEXT_W), F32),
        pltpu.VMEM((8, LANES), F32),
    ]
    return pl.pallas_call(
        functools.partial(_mixer_body, tiles_per_seq=tiles_per_seq),
        grid=(n_tiles + 1,),
        in_specs=in_specs,
        out_specs=pl.BlockSpec((1, rows, d), lambda i: (*out_tile(i), 0)),
        out_shape=jax.ShapeDtypeStruct((b, s, d), F32),
        scratch_shapes=slot + slot + shared,
        compiler_params=pltpu.CompilerParams(
            dimension_semantics=("arbitrary",), vmem_limit_bytes=(7 * VMEM_BYTES_V7X) // 8),
        name="mixer",
    )(x, positions.reshape(b, 1, s), norm_g.reshape(1, d), inv, b_if, out_norm_g.reshape(1, MLSTM_V_W),
      sinks, w_qk, w_kT, v_m.astype(BF16), o_m.astype(BF16), w_ifT, w_qa, w_kva,
      g_m.astype(BF16), g_a.astype(BF16), w_br_m.astype(BF16), w_br_a.astype(BF16), w_out.astype(BF16))


def kernel(x, positions, ffn1_norm_g, ffn1_w_gate, ffn1_w_up, ffn1_w_down, mix_norm_g, w_in, mlstm_b_i, mlstm_b_f, mlstm_out_norm_g, attn_sinks, w_branch_mlstm, w_branch_attn, w_out, ffn2_norm_g, ffn2_w_gate, ffn2_w_up, ffn2_w_down, final_norm_g):
    b, s, d = x.shape
    depth = w_in.shape[0]
    for l in range(depth):
        last = l == depth - 1
        x = _ffn(x.reshape(b * s, d), ffn1_norm_g[l], ffn1_w_gate[l], ffn1_w_up[l], ffn1_w_down[l]).reshape(b, s, d)
        x = _mixer(x, positions, mix_norm_g[l], w_in[l], mlstm_b_i[l], mlstm_b_f[l], mlstm_out_norm_g[l],
                   attn_sinks[l], w_branch_mlstm[l], w_branch_attn[l], w_out[l])
        x = _ffn(x.reshape(b * s, d), ffn2_norm_g[l], ffn2_w_gate[l], ffn2_w_up[l], ffn2_w_down[l],
                 final_g=final_norm_g if last else None).reshape(b, s, d)
    return x
```

```python
import collections
import functools

import jax
import jax.numpy as jnp
from jax import lax
from jax.experimental import pallas as pl
from jax.experimental.pallas import tpu as pltpu

D_MODEL = 1024
D_FF = 2816
MLSTM_HEADS = 4
MLSTM_QK_DIM = 64
MLSTM_V_DIM = 128
GATE_SOFTCAP = 15.0
ATTN_Q_HEADS = 8
ATTN_KV_HEADS = 2
ATTN_HEAD_DIM = 64
WINDOW = 128
ROPE_DIM = ATTN_HEAD_DIM // 4
ROPE_THETA = 500000.0
NORM_EPS = 1e-6

MLSTM_QK_W = MLSTM_HEADS * MLSTM_QK_DIM
MLSTM_V_W = MLSTM_HEADS * MLSTM_V_DIM
ATTN_Q_W = ATTN_Q_HEADS * ATTN_HEAD_DIM
ATTN_KV_W = ATTN_KV_HEADS * ATTN_HEAD_DIM
IN_WIDTHS = (MLSTM_QK_W, MLSTM_QK_W, MLSTM_V_W, MLSTM_V_W, MLSTM_HEADS, MLSTM_HEADS,
             ATTN_Q_W, ATTN_KV_W, ATTN_KV_W, D_MODEL, D_MODEL)

LANES = 128
MXU_COLS = 256
VMEM_BYTES_V7X = 64 * 1024 * 1024

FFN_ROWS = 1024
FFN_COL_CHUNK = 2 * MXU_COLS
MIX_ROWS = 512
MIX_COL_CHUNK = 2 * MXU_COLS
CHUNK = LANES
PAIR_W = 2 * MLSTM_QK_DIM
VEXT_W = 2 * MLSTM_V_DIM
GATE_ROWS = 2 * MLSTM_HEADS
KG_ROWS = MLSTM_QK_W + 16

F32 = jnp.float32
BF16 = jnp.bfloat16
_NT = (((1,), (1,)), ((), ()))


def _rms(x, g):
    return x * lax.rsqrt(jnp.mean(x * x, axis=-1, keepdims=True) + NORM_EPS) * g


def _col_chunks(total, width):
    return [(c, min(c + width, total)) for c in range(0, total, width)]


def _const_spec(shape):
    return pl.BlockSpec(shape, lambda *_: (0,) * len(shape), pipeline_mode=pl.Buffered(1))


def _ffn_body(*refs, final_norm):
    if final_norm:
        x_ref, g_ref, wg_ref, wu_ref, wd_ref, fg_ref, o_ref, act_ref = refs
    else:
        x_ref, g_ref, wg_ref, wu_ref, wd_ref, o_ref, act_ref = refs
    x = x_ref[...]
    h = _rms(x, g_ref[...]).astype(BF16)
    for c0, c1 in _col_chunks(D_FF, FFN_COL_CHUNK):
        g = jnp.dot(h, wg_ref[:, c0:c1], preferred_element_type=F32)
        u = jnp.dot(h, wu_ref[:, c0:c1], preferred_element_type=F32)
        act_ref[:, c0:c1] = (g * jax.nn.sigmoid(g) * u).astype(BF16)
    r = x + 0.5 * jnp.dot(act_ref[...], wd_ref[...], preferred_element_type=F32)
    if final_norm:
        r = _rms(r, fg_ref[...])
    o_ref[...] = r


def _ffn(x2, norm_g, w_gate, w_up, w_down, final_g=None):
    n, d = x2.shape
    final_norm = final_g is not None
    row_spec = pl.BlockSpec((FFN_ROWS, d), lambda i: (i, 0))
    in_specs = [row_spec, _const_spec((1, d)), _const_spec((d, D_FF)), _const_spec((d, D_FF)),
                _const_spec((D_FF, d))]
    args = [x2, norm_g.reshape(1, d), w_gate.astype(BF16), w_up.astype(BF16), w_down.astype(BF16)]
    if final_norm:
        in_specs.append(_const_spec((1, d)))
        args.append(final_g.reshape(1, d))
    return pl.pallas_call(
        functools.partial(_ffn_body, final_norm=final_norm),
        grid=(n // FFN_ROWS,),
        in_specs=in_specs,
        out_specs=row_spec,
        out_shape=jax.ShapeDtypeStruct((n, d), F32),
        scratch_shapes=[pltpu.VMEM((FFN_ROWS, D_FF), BF16)],
        compiler_params=pltpu.CompilerParams(
            dimension_semantics=("arbitrary",), vmem_limit_bytes=(3 * VMEM_BYTES_V7X) // 4),
        name="ffn_final" if final_norm else "ffn",
    )(*args)


Slot = collections.namedtuple("Slot", "xs h qk kT vext so qa kc kcs vc vcs g")


def _slot_shapes(rows, d):
    return Slot(
        xs=pltpu.VMEM((rows, d), F32),
        h=pltpu.VMEM((rows, d), BF16),
        qk=pltpu.VMEM((rows, 2 * MLSTM_QK_W), BF16),
        kT=pltpu.VMEM((MLSTM_QK_W, rows), F32),
        vext=pltpu.VMEM((rows, MLSTM_HEADS * VEXT_W), BF16),
        so=pltpu.VMEM((rows, MLSTM_V_W), F32),
        qa=pltpu.VMEM((rows, ATTN_Q_W), BF16),
        kc=pltpu.VMEM((rows, LANES), BF16),
        kcs=pltpu.VMEM((rows, LANES), BF16),
        vc=pltpu.VMEM((rows, LANES), BF16),
        vcs=pltpu.VMEM((rows, LANES), BF16),
        g=pltpu.VMEM((GATE_ROWS, rows), F32),
    )


def _softcap(a):
    return GATE_SOFTCAP * jnp.tanh(a / GATE_SOFTCAP)


def _log_sigmoid(a):
    return jnp.minimum(a, 0.0) - jnp.log1p(jnp.exp(-jnp.abs(a)))


def _lane_cumsum(a):
    lane = lax.broadcasted_iota(jnp.int32, a.shape, 1)
    d = 1
    while d < LANES:
        a = a + jnp.where(lane >= d, pltpu.roll(a, d, 1), 0.0)
        d *= 2
    return a


def _rope_tables(pos_row, inv_col):
    ang = inv_col * pos_row
    c, s = jnp.cos(ang), jnp.sin(ang)
    one, zero = jnp.ones_like(c), jnp.zeros_like(c)
    per_head = ATTN_HEAD_DIM // 8

    def tile(first, second, rest):
        rows = []
        for _ in range(LANES // ATTN_HEAD_DIM):
            rows += [first, second] + [rest] * (per_head - 2)
        return jnp.concatenate(rows, axis=0).T
    return tile(c, c, one), tile(zero, s, zero), tile(-s, zero, zero)


def _rope(a, tables):
    cos_t, sin_up, sin_dn = tables
    half = ROPE_DIM // 2
    return a * cos_t + pltpu.roll(a, half, 1) * sin_up + pltpu.roll(a, LANES - half, 1) * sin_dn


def _rep(a, shape):
    return jnp.broadcast_to(a, shape)


def _mixer_step(t_cur, ins, cur, nxt, shared, o_ref):
    (x_ref, pos_ref, ng_ref, inv_ref, bif_ref, ong_ref, sink_ref,
     wqk_ref, wkgT_ref, wv_ref, wo_ref, wqa_ref, wkva_ref, wgm_ref, wga_ref,
     wbrm_ref, wbra_ref, wout_ref) = ins
    pk_scr, pks_scr, pv_scr, pvs_scr, sgm_scr, sga_scr, hm_scr, oa_scr, mg_scr, c_scr, m_scr = shared
    rows = MIX_ROWS
    n_chunks = rows // CHUNK
    n_blocks = rows // WINDOW
    group = ATTN_Q_HEADS // ATTN_KV_HEADS

    shared_vals = {}

    def p_norm():
        x_new = x_ref[0]
        nxt.xs[...] = x_new
        nxt.h[...] = _rms(x_new, ng_ref[...]).astype(BF16)

    def p_qk():
        nxt.qk[...] = jnp.dot(nxt.h[...], wqk_ref[...], preferred_element_type=F32).astype(BF16)

    def p_kT():
        kg = lax.dot_general(wkgT_ref[...], nxt.h[...], _NT, preferred_element_type=F32)
        nxt.kT[...] = kg[0:MLSTM_QK_W]
        nxt.g[...] = _softcap(kg[MLSTM_QK_W:MLSTM_QK_W + GATE_ROWS] + bif_ref[...])

    def p_v():
        v = jnp.dot(nxt.h[...], wv_ref[...], preferred_element_type=F32)
        for hd in range(MLSTM_HEADS):
            nxt.vext[:, hd * VEXT_W:hd * VEXT_W + MLSTM_V_DIM] = (
                v[:, hd * MLSTM_V_DIM:(hd + 1) * MLSTM_V_DIM].astype(BF16))
            nxt.vext[:, hd * VEXT_W + MLSTM_V_DIM:(hd + 1) * VEXT_W] = jnp.ones((rows, MLSTM_V_DIM), BF16)

    def p_so():
        nxt.so[...] = jax.nn.sigmoid(jnp.dot(nxt.h[...], wo_ref[...], preferred_element_type=F32))

    def p_qa():
        shared_vals["tables"] = _rope_tables(pos_ref[0].astype(F32), inv_ref[...])
        qa = jnp.dot(nxt.h[...], wqa_ref[...], preferred_element_type=F32)
        for p in range(ATTN_Q_W // LANES):
            nxt.qa[:, p * LANES:(p + 1) * LANES] = _rope(
                qa[:, p * LANES:(p + 1) * LANES], shared_vals["tables"]).astype(BF16)

    def p_kva():
        kva = jnp.dot(nxt.h[...], wkva_ref[...], preferred_element_type=F32)
        ka = _rope(kva[:, 0:ATTN_KV_W], shared_vals["tables"])
        va = kva[:, ATTN_KV_W:2 * ATTN_KV_W]
        nxt.kc[...] = ka.astype(BF16)
        nxt.kcs[...] = pltpu.roll(ka, ATTN_HEAD_DIM, 1).astype(BF16)
        nxt.vc[...] = va.astype(BF16)
        nxt.vcs[...] = pltpu.roll(va, ATTN_HEAD_DIM, 1).astype(BF16)

    def gate_piece(idx):
        gref, wref = (sgm_scr, wgm_ref) if idx < 2 else (sga_scr, wga_ref)
        c0 = (idx % 2) * MIX_COL_CHUNK
        gref[:, c0:c0 + MIX_COL_CHUNK] = jax.nn.sigmoid(
            jnp.dot(cur.h[...], wref[:, c0:c0 + MIX_COL_CHUNK], preferred_element_type=F32))

    qi = lax.broadcasted_iota(jnp.int32, (CHUNK, CHUNK), 0)
    si = lax.broadcasted_iota(jnp.int32, (CHUNK, CHUNK), 1)
    tri = si <= qi
    lane_lo = lax.broadcasted_iota(jnp.int32, (CHUNK, PAIR_W), 1) < MLSTM_QK_DIM
    zeros_c = jnp.zeros((MLSTM_QK_DIM, VEXT_W), BF16)
    ong = ong_ref[...]
    tile_shape = (CHUNK, CHUNK)
    rep_shape = (MLSTM_HEADS, LANES)

    chunk_stats = []
    m_vec = m_scr[0:MLSTM_HEADS, :]
    for c in range(n_chunks):
        g_c = cur.g[:, c * CHUNK:(c + 1) * CHUNK]
        i_r = g_c[0:MLSTM_HEADS]
        logf_r = _log_sigmoid(g_c[MLSTM_HEADS:GATE_ROWS])
        b_r = _lane_cumsum(logf_r)
        b_last = _rep(b_r[:, CHUNK - 1:CHUNK], rep_shape)
        a_r = b_last - b_r + i_r
        a_max = _rep(jnp.max(a_r, axis=1, keepdims=True), rep_shape)
        m_new = jnp.maximum(b_last + m_vec, a_max)
        chunk_stats.append(dict(
            logf_r=logf_r, ib_r=i_r - b_r, m_prev=m_vec,
            decay=jnp.exp(b_last + m_vec - m_new), scale=jnp.exp(a_max - m_new),
            w_loc=jnp.exp(a_r - a_max)))
        m_vec = m_new
    c_state = [c_scr[hd] for hd in range(MLSTM_HEADS)]
    c_next = [None] * MLSTM_HEADS
    s_qk = {}

    def mlstm_scores(c):
        rs = slice(c * CHUNK, (c + 1) * CHUNK)
        for pair in range(MLSTM_HEADS // 2):
            q2 = cur.qk[rs, pair * PAIR_W:(pair + 1) * PAIR_W]
            k2 = cur.qk[rs, MLSTM_QK_W + pair * PAIR_W:MLSTM_QK_W + (pair + 1) * PAIR_W]
            zk = jnp.zeros_like(k2)
            k_both = jnp.concatenate([jnp.where(lane_lo, k2, zk), jnp.where(lane_lo, zk, k2)], axis=0)
            s2 = lax.dot_general(q2, k_both, _NT, preferred_element_type=F32)
            s_qk[c, 2 * pair] = s2[:, 0:CHUNK]
            s_qk[c, 2 * pair + 1] = s2[:, CHUNK:2 * CHUNK]

    def mlstm_state(c):
        rs = slice(c * CHUNK, (c + 1) * CHUNK)
        st = chunk_stats[c]
        for hd in range(MLSTM_HEADS):
            vext = cur.vext[rs, hd * VEXT_W:(hd + 1) * VEXT_W]
            ktw = (cur.kT[hd * MLSTM_QK_DIM:(hd + 1) * MLSTM_QK_DIM, rs] * st["w_loc"][hd:hd + 1]).astype(BF16)
            u = jnp.dot(ktw, vext, preferred_element_type=F32)
            decay = jnp.concatenate([st["decay"][hd:hd + 1]] * 2, axis=1)
            scale = jnp.concatenate([st["scale"][hd:hd + 1]] * 2, axis=1)
            c_next[hd] = decay * c_state[hd] + scale * u

    def mlstm_out(c):
        rs = slice(c * CHUNK, (c + 1) * CHUNK)
        st = chunk_stats[c]
        for hd in range(MLSTM_HEADS):
            pair, odd = divmod(hd, 2)
            q2 = cur.qk[rs, pair * PAIR_W:(pair + 1) * PAIR_W]
            vext = cur.vext[rs, hd * VEXT_W:(hd + 1) * VEXT_W]
            b_col = _rep(jnp.sum(jnp.where(tri, st["logf_r"][hd:hd + 1], 0.0), axis=1, keepdims=True),
                         tile_shape)
            log_d = jnp.where(tri, b_col + st["ib_r"][hd:hd + 1], -jnp.inf)
            inter_log = b_col + st["m_prev"][hd:hd + 1]
            m_j = jnp.maximum(inter_log, _rep(jnp.max(log_d, axis=1, keepdims=True), tile_shape))
            sqk = (s_qk.pop((c, hd)) * jnp.exp(log_d - m_j)).astype(BF16)
            c_bf = c_state[hd].astype(BF16)
            c_pad = jnp.concatenate([zeros_c, c_bf] if odd else [c_bf, zeros_c], axis=0)
            intra = jnp.dot(sqk, vext, preferred_element_type=F32)
            inter = jnp.dot(q2, c_pad, preferred_element_type=F32)
            inter_scale = jnp.exp(inter_log - m_j)
            num = intra[:, 0:MLSTM_V_DIM] + inter_scale * inter[:, 0:MLSTM_V_DIM]
            den = intra[:, MLSTM_V_DIM:VEXT_W] + inter_scale * inter[:, MLSTM_V_DIM:VEXT_W]
            hh = num / jnp.maximum(jnp.abs(den), jnp.exp(-m_j))
            vs = slice(hd * MLSTM_V_DIM, (hd + 1) * MLSTM_V_DIM)
            hn = hh * lax.rsqrt(jnp.mean(hh * hh, axis=1, keepdims=True) + NORM_EPS) * ong[:, vs]
            hm_scr[rs, vs] = (hn * cur.so[rs, vs]).astype(BF16)
            c_state[hd] = c_next[hd]

    qi2 = lax.broadcasted_iota(jnp.int32, (WINDOW, 2 * WINDOW), 0)
    si2 = lax.broadcasted_iota(jnp.int32, (WINDOW, 2 * WINDOW), 1)
    band = (si2 > qi2) & (si2 <= qi2 + WINDOW)
    band_first = band & ((si2 >= WINDOW) | (t_cur > 0))
    kv_lo = lax.broadcasted_iota(jnp.int32, (2 * WINDOW, LANES), 1) < ATTN_HEAD_DIM
    ones_lo = jnp.where(kv_lo, 1.0, 0.0).astype(BF16)
    ones_hi = jnp.where(kv_lo, 0.0, 1.0).astype(BF16)
    zkv = jnp.zeros((2 * WINDOW, LANES), BF16)
    attn = {}

    def window_rows(tile_ref, prev_ref, nb):
        if nb == 0:
            return jnp.concatenate([prev_ref[...], tile_ref[0:WINDOW, :]], axis=0)
        return tile_ref[(nb - 1) * WINDOW:(nb + 1) * WINDOW, :]

    def attn_scores(nb):
        qs = slice(nb * WINDOW, (nb + 1) * WINDOW)
        mask = band_first if nb == 0 else band
        k_own, k_sw = window_rows(cur.kc, pk_scr, nb), window_rows(cur.kcs, pks_scr, nb)
        for kh in range(ATTN_KV_HEADS):
            own, swapped = (k_own, k_sw) if kh == 0 else (k_sw, k_own)
            k_lo = jnp.where(kv_lo, own, zkv)
            k_hi = jnp.where(kv_lo, zkv, swapped)
            pair0 = kh * (group // 2)
            lhs = jnp.concatenate([cur.qa[qs, (pair0 + j) * LANES:(pair0 + j + 1) * LANES]
                                   for j in range(group // 2)], axis=0)
            probs, sink_terms = [], []
            for odd, k_sel in enumerate((k_lo, k_hi)):
                s_all = lax.dot_general(lhs, k_sel, _NT, preferred_element_type=F32)
                p_rows, sk_rows = [], []
                for j in range(group // 2):
                    sink = sink_ref[kh * group + 2 * j + odd]
                    sc = jnp.where(mask, s_all[j * WINDOW:(j + 1) * WINDOW], -jnp.inf)
                    mx = jnp.maximum(jnp.max(sc, axis=1, keepdims=True), sink)
                    p_rows.append(jnp.exp(sc - mx).astype(BF16))
                    sk_rows.append(_rep(jnp.exp(sink - mx), (WINDOW, LANES)))
                probs.append(jnp.concatenate(p_rows, axis=0))
                sink_terms.append(jnp.concatenate(sk_rows, axis=0))
            attn[nb, kh] = (jnp.concatenate(probs, axis=1), jnp.where(kv_lo, sink_terms[0], sink_terms[1]))

    def attn_values(nb):
        qs = slice(nb * WINDOW, (nb + 1) * WINDOW)
        v_own, v_sw = window_rows(cur.vc, pv_scr, nb), window_rows(cur.vcs, pvs_scr, nb)
        for kh in range(ATTN_KV_HEADS):
            own, swapped = (v_own, v_sw) if kh == 0 else (v_sw, v_own)
            v_lo = jnp.where(kv_lo, own, zkv)
            v_hi = jnp.where(kv_lo, zkv, swapped)
            probs, sink_term = attn.pop((nb, kh))
            rhs = jnp.concatenate([jnp.concatenate([v_lo, ones_lo], axis=1),
                                   jnp.concatenate([v_hi, ones_hi], axis=1)], axis=0)
            o = jnp.dot(probs, rhs, preferred_element_type=F32)
            on = (o[:, 0:LANES] / (o[:, LANES:2 * LANES] + sink_term)).astype(BF16)
            pair0 = kh * (group // 2)
            for j in range(group // 2):
                oa_scr[qs, (pair0 + j) * LANES:(pair0 + j + 1) * LANES] = on[j * WINDOW:(j + 1) * WINDOW]

    def branch_attn(idx):
        c0 = idx * MIX_COL_CHUNK
        y_a = jnp.dot(oa_scr[...], wbra_ref[:, c0:c0 + MIX_COL_CHUNK], preferred_element_type=F32)
        sga_scr[:, c0:c0 + MIX_COL_CHUNK] = sga_scr[:, c0:c0 + MIX_COL_CHUNK] * y_a

    gate_piece(0)
    p_norm()
    attn_scores(0); mlstm_scores(0); mlstm_state(0)
    gate_piece(1)
    attn_values(0); mlstm_out(0)
    p_qk()
    attn_scores(1); mlstm_scores(1); mlstm_state(1)
    p_kT()
    attn_values(1); mlstm_out(1)
    p_v()
    attn_scores(2); mlstm_scores(2); mlstm_state(2)
    gate_piece(2)
    attn_values(2); mlstm_out(2)
    p_so()
    attn_scores(3); mlstm_scores(3); mlstm_state(3)
    gate_piece(3)
    attn_values(3)
    p_qa()
    mlstm_out(3)
    branch_attn(0); p_kva(); branch_attn(1)

    for hd in range(MLSTM_HEADS):
        c_scr[hd] = c_state[hd]
    m_scr[0:MLSTM_HEADS, :] = m_vec
    for prev_ref, tile_ref in ((pk_scr, cur.kc), (pks_scr, cur.kcs), (pv_scr, cur.vc), (pvs_scr, cur.vcs)):
        prev_ref[...] = tile_ref[rows - WINDOW:rows, :]

    for c0, c1 in _col_chunks(D_MODEL, MIX_COL_CHUNK):
        y_m = jnp.dot(hm_scr[...], wbrm_ref[:, c0:c1], preferred_element_type=F32)
        mg_scr[:, c0:c1] = (sgm_scr[:, c0:c1] * y_m + sga_scr[:, c0:c1]).astype(BF16)
    o_ref[0] = cur.xs[...] + jnp.dot(mg_scr[...], wout_ref[...], preferred_element_type=F32)


def _mixer_body(*refs, tiles_per_seq):
    n_in = 18
    n_slot = len(Slot._fields)
    ins = refs[:n_in]
    o_ref = refs[n_in]
    scr = refs[n_in + 1:]
    slot_a, slot_b = Slot(*scr[:n_slot]), Slot(*scr[n_slot:2 * n_slot])
    shared = scr[2 * n_slot:]
    pk_scr, pks_scr, pv_scr, pvs_scr = shared[0:4]
    c_scr, m_scr = shared[-2:]
    s = pl.program_id(0)
    t_cur = lax.rem(s + tiles_per_seq - 1, tiles_per_seq)

    @pl.when(s == 0)
    def _():
        for ref in list(slot_b) + list(shared):
            ref[...] = jnp.zeros_like(ref)

    @pl.when(t_cur == 0)
    def _():
        c_scr[...] = jnp.zeros_like(c_scr)
        m_scr[...] = jnp.zeros_like(m_scr)
        for ref in (pk_scr, pks_scr, pv_scr, pvs_scr):
            ref[...] = jnp.zeros_like(ref)

    even = lax.rem(s, 2) == 0

    @pl.when(even)
    def _():
        _mixer_step(t_cur, ins, slot_b, slot_a, shared, o_ref)

    @pl.when(jnp.logical_not(even))
    def _():
        _mixer_step(t_cur, ins, slot_a, slot_b, shared, o_ref)


def _mixer(x, positions, norm_g, w_in, b_i, b_f, out_norm_g, sinks, w_br_m, w_br_a, w_out):
    b, s, d = x.shape
    starts = [0]
    for w in IN_WIDTHS:
        starts.append(starts[-1] + w)
    col = lambda i: w_in[:, starts[i]:starts[i + 1]]
    q_m, k_m, v_m, o_m, i_m, f_m, q_a, k_a, v_a, g_m, g_a = (col(i) for i in range(len(IN_WIDTHS)))
    w_qk = jnp.concatenate([q_m * (MLSTM_QK_DIM ** -0.5), k_m], axis=1).astype(BF16)
    w_kgT = jnp.concatenate([k_m, i_m, f_m, jnp.zeros((d, KG_ROWS - MLSTM_QK_W - GATE_ROWS), F32)],
                            axis=1).T.astype(BF16)
    w_qa = (q_a * (ATTN_HEAD_DIM ** -0.5)).astype(BF16)
    w_kva = jnp.concatenate([k_a, v_a], axis=1).astype(BF16)
    b_if = jnp.concatenate([b_i, b_f]).reshape(GATE_ROWS, 1)
    half = ROPE_DIM // 2
    inv = (ROPE_THETA ** (-jnp.arange(half, dtype=F32) * 2.0 / ROPE_DIM)).reshape(half, 1)

    rows = MIX_ROWS
    tiles_per_seq = s // rows
    n_tiles = b * tiles_per_seq
    def in_tile(i):
        j = jnp.minimum(i, n_tiles - 1)
        return j // tiles_per_seq, j % tiles_per_seq

    def out_tile(i):
        j = jnp.maximum(i - 1, 0)
        return j // tiles_per_seq, j % tiles_per_seq

    in_specs = [
        pl.BlockSpec((1, rows, d), lambda i: (*in_tile(i), 0)),
        pl.BlockSpec((1, 1, rows), lambda i: (in_tile(i)[0], 0, in_tile(i)[1])),
        _const_spec((1, d)), _const_spec((half, 1)), _const_spec((GATE_ROWS, 1)),
        _const_spec((1, MLSTM_V_W)),
        pl.BlockSpec(memory_space=pltpu.SMEM),
        _const_spec((d, 2 * MLSTM_QK_W)), _const_spec((KG_ROWS, d)), _const_spec((d, MLSTM_V_W)),
        _const_spec((d, MLSTM_V_W)), _const_spec((d, ATTN_Q_W)),
        _const_spec((d, 2 * ATTN_KV_W)), _const_spec((d, d)), _const_spec((d, d)),
        _const_spec((MLSTM_V_W, d)), _const_spec((ATTN_Q_W, d)), _const_spec((d, d)),
    ]
    slot = list(_slot_shapes(rows, d))
    shared = [
        pltpu.VMEM((WINDOW, LANES), BF16),
        pltpu.VMEM((WINDOW, LANES), BF16),
        pltpu.VMEM((WINDOW, LANES), BF16),
        pltpu.VMEM((WINDOW, LANES), BF16),
        pltpu.VMEM((rows, d), F32),
        pltpu.VMEM((rows, d), F32),
        pltpu.VMEM((rows, MLSTM_V_W), BF16),
        pltpu.VMEM((rows, ATTN_Q_W), BF16),
        pltpu.VMEM((rows, d), BF16),
        pltpu.VMEM((MLSTM_HEADS, MLSTM_QK_DIM, VEXT_W), F32),
        pltpu.VMEM((8, LANES), F32),
    ]
    return pl.pallas_call(
        functools.partial(_mixer_body, tiles_per_seq=tiles_per_seq),
        grid=(n_tiles + 1,),
        in_specs=in_specs,
        out_specs=pl.BlockSpec((1, rows, d), lambda i: (*out_tile(i), 0)),
        out_shape=jax.ShapeDtypeStruct((b, s, d), F32),
        scratch_shapes=slot + slot + shared,
        compiler_params=pltpu.CompilerParams(
            dimension_semantics=("arbitrary",), vmem_limit_bytes=(7 * VMEM_BYTES_V7X) // 8),
        name="mixer",
    )(x, positions.reshape(b, 1, s), norm_g.reshape(1, d), inv, b_if, out_norm_g.reshape(1, MLSTM_V_W),
      sinks, w_qk, w_kgT, v_m.astype(BF16), o_m.astype(BF16), w_qa, w_kva,
      g_m.astype(BF16), g_a.astype(BF16), w_br_m.astype(BF16), w_br_a.astype(BF16), w_out.astype(BF16))


def kernel(x, positions, ffn1_norm_g, ffn1_w_gate, ffn1_w_up, ffn1_w_down, mix_norm_g, w_in, mlstm_b_i, mlstm_b_f, mlstm_out_norm_g, attn_sinks, w_branch_mlstm, w_branch_attn, w_out, ffn2_norm_g, ffn2_w_gate, ffn2_w_up, ffn2_w_down, final_norm_g):
    b, s, d = x.shape
    depth = w_in.shape[0]
    for l in range(depth):
        last = l == depth - 1
        x = _ffn(x.reshape(b * s, d), ffn1_norm_g[l], ffn1_w_gate[l], ffn1_w_up[l], ffn1_w_down[l]).reshape(b, s, d)
        x = _mixer(x, positions, mix_norm_g[l], w_in[l], mlstm_b_i[l], mlstm_b_f[l], mlstm_out_norm_g[l],
                   attn_sinks[l], w_branch_mlstm[l], w_branch_attn[l], w_out[l])
        x = _ffn(x.reshape(b * s, d), ffn2_norm_g[l], ffn2_w_gate[l], ffn2_w_up[l], ffn2_w_down[l],
                 final_g=final_norm_g if last else None).reshape(b, s, d)
    return x
```

```python
import collections
import functools

import jax
import jax.numpy as jnp
from jax import lax
from jax.experimental import pallas as pl
from jax.experimental.pallas import tpu as pltpu

D_MODEL = 1024
D_FF = 2816
MLSTM_HEADS = 4
MLSTM_QK_DIM = 64
MLSTM_V_DIM = 128
GATE_SOFTCAP = 15.0
ATTN_Q_HEADS = 8
ATTN_KV_HEADS = 2
ATTN_HEAD_DIM = 64
WINDOW = 128
ROPE_DIM = ATTN_HEAD_DIM // 4
ROPE_THETA = 500000.0
NORM_EPS = 1e-6

MLSTM_QK_W = MLSTM_HEADS * MLSTM_QK_DIM
MLSTM_V_W = MLSTM_HEADS * MLSTM_V_DIM
ATTN_Q_W = ATTN_Q_HEADS * ATTN_HEAD_DIM
ATTN_KV_W = ATTN_KV_HEADS * ATTN_HEAD_DIM
IN_WIDTHS = (MLSTM_QK_W, MLSTM_QK_W, MLSTM_V_W, MLSTM_V_W, MLSTM_HEADS, MLSTM_HEADS,
             ATTN_Q_W, ATTN_KV_W, ATTN_KV_W, D_MODEL, D_MODEL)

LANES = 128
MXU_COLS = 256
VMEM_BYTES_V7X = 64 * 1024 * 1024

FFN_ROWS = 512
FFN_COL_CHUNK = 2 * MXU_COLS
MIX_ROWS = 512
MIX_COL_CHUNK = 2 * MXU_COLS
CHUNK = LANES
PAIR_W = 2 * MLSTM_QK_DIM
VEXT_W = 2 * MLSTM_V_DIM
GATE_ROWS = 2 * MLSTM_HEADS
KG_ROWS = MLSTM_QK_W + 16

F32 = jnp.float32
BF16 = jnp.bfloat16
_NT = (((1,), (1,)), ((), ()))


def _rms(x, g):
    return x * lax.rsqrt(jnp.mean(x * x, axis=-1, keepdims=True) + NORM_EPS) * g


def _col_chunks(total, width):
    return [(c, min(c + width, total)) for c in range(0, total, width)]


def _const_spec(shape):
    return pl.BlockSpec(shape, lambda *_: (0,) * len(shape), pipeline_mode=pl.Buffered(1))


def _ffn_body(*refs, final_norm):
    if final_norm:
        x_ref, g_ref, wg_ref, wu_ref, wd_ref, fg_ref, o_ref, act_ref = refs
    else:
        x_ref, g_ref, wg_ref, wu_ref, wd_ref, o_ref, act_ref = refs
    x = x_ref[...]
    h = _rms(x, g_ref[...])
    for c0, c1 in _col_chunks(D_FF, FFN_COL_CHUNK):
        g = jnp.dot(h, wg_ref[:, c0:c1], preferred_element_type=F32)
        u = jnp.dot(h, wu_ref[:, c0:c1], preferred_element_type=F32)
        act_ref[:, c0:c1] = g * jax.nn.sigmoid(g) * u
    r = x + 0.5 * jnp.dot(act_ref[...], wd_ref[...], preferred_element_type=F32)
    if final_norm:
        r = _rms(r, fg_ref[...])
    o_ref[...] = r


def _ffn(x2, norm_g, w_gate, w_up, w_down, final_g=None):
    n, d = x2.shape
    final_norm = final_g is not None
    row_spec = pl.BlockSpec((FFN_ROWS, d), lambda i: (i, 0))
    in_specs = [row_spec, _const_spec((1, d)), _const_spec((d, D_FF)), _const_spec((d, D_FF)),
                _const_spec((D_FF, d))]
    args = [x2, norm_g.reshape(1, d), w_gate, w_up, w_down]
    if final_norm:
        in_specs.append(_const_spec((1, d)))
        args.append(final_g.reshape(1, d))
    return pl.pallas_call(
        functools.partial(_ffn_body, final_norm=final_norm),
        grid=(n // FFN_ROWS,),
        in_specs=in_specs,
        out_specs=row_spec,
        out_shape=jax.ShapeDtypeStruct((n, d), F32),
        scratch_shapes=[pltpu.VMEM((FFN_ROWS, D_FF), F32)],
        compiler_params=pltpu.CompilerParams(
            dimension_semantics=("arbitrary",), vmem_limit_bytes=(7 * VMEM_BYTES_V7X) // 8),
        name="ffn_final" if final_norm else "ffn",
    )(*args)


Slot = collections.namedtuple("Slot", "xs h qk kT vext so qa kc kcs vc vcs g")


def _slot_shapes(rows, d):
    return Slot(
        xs=pltpu.VMEM((rows, d), F32),
        h=pltpu.VMEM((rows, d), BF16),
        qk=pltpu.VMEM((rows, 2 * MLSTM_QK_W), BF16),
        kT=pltpu.VMEM((MLSTM_QK_W, rows), F32),
        vext=pltpu.VMEM((rows, MLSTM_HEADS * VEXT_W), BF16),
        so=pltpu.VMEM((rows, MLSTM_V_W), F32),
        qa=pltpu.VMEM((rows, ATTN_Q_W), BF16),
        kc=pltpu.VMEM((rows, LANES), BF16),
        kcs=pltpu.VMEM((rows, LANES), BF16),
        vc=pltpu.VMEM((rows, LANES), BF16),
        vcs=pltpu.VMEM((rows, LANES), BF16),
        g=pltpu.VMEM((GATE_ROWS, rows), F32),
    )


def _softcap(a):
    return GATE_SOFTCAP * jnp.tanh(a / GATE_SOFTCAP)


def _log_sigmoid(a):
    return jnp.minimum(a, 0.0) - jnp.log1p(jnp.exp(-jnp.abs(a)))


def _lane_cumsum(a):
    lane = lax.broadcasted_iota(jnp.int32, a.shape, 1)
    d = 1
    while d < LANES:
        a = a + jnp.where(lane >= d, pltpu.roll(a, d, 1), 0.0)
        d *= 2
    return a


def _rope_tables(pos_row, inv_col):
    ang = inv_col * pos_row
    c, s = jnp.cos(ang), jnp.sin(ang)
    one, zero = jnp.ones_like(c), jnp.zeros_like(c)
    per_head = ATTN_HEAD_DIM // 8

    def tile(first, second, rest):
        rows = []
        for _ in range(LANES // ATTN_HEAD_DIM):
            rows += [first, second] + [rest] * (per_head - 2)
        return jnp.concatenate(rows, axis=0).T
    return tile(c, c, one), tile(zero, s, zero), tile(-s, zero, zero)


def _rope(a, tables):
    cos_t, sin_up, sin_dn = tables
    half = ROPE_DIM // 2
    return a * cos_t + pltpu.roll(a, half, 1) * sin_up + pltpu.roll(a, LANES - half, 1) * sin_dn


def _rep(a, shape):
    return jnp.broadcast_to(a, shape)


def _mixer_step(t_cur, ins, cur, nxt, shared, o_ref):
    (x_ref, pos_ref, ng_ref, inv_ref, bif_ref, ong_ref, sink_ref,
     wqk_ref, wkgT_ref, wv_ref, wo_ref, wqa_ref, wkva_ref, wgm_ref, wga_ref,
     wbrm_ref, wbra_ref, wout_ref) = ins
    pk_scr, pks_scr, pv_scr, pvs_scr, sgm_scr, sga_scr, hm_scr, oa_scr, mg_scr, c_scr, m_scr = shared
    rows = MIX_ROWS
    n_chunks = rows // CHUNK
    n_blocks = rows // WINDOW
    group = ATTN_Q_HEADS // ATTN_KV_HEADS

    shared_vals = {}

    def p_norm():
        x_new = x_ref[0]
        nxt.xs[...] = x_new
        nxt.h[...] = _rms(x_new, ng_ref[...]).astype(BF16)

    def p_qk():
        nxt.qk[...] = jnp.dot(nxt.h[...], wqk_ref[...], preferred_element_type=F32).astype(BF16)

    def p_kT():
        kg = lax.dot_general(wkgT_ref[...], nxt.h[...], _NT, preferred_element_type=F32)
        nxt.kT[...] = kg[0:MLSTM_QK_W]
        nxt.g[...] = _softcap(kg[MLSTM_QK_W:MLSTM_QK_W + GATE_ROWS] + bif_ref[...])

    def p_v():
        v = jnp.dot(nxt.h[...], wv_ref[...], preferred_element_type=F32)
        for hd in range(MLSTM_HEADS):
            nxt.vext[:, hd * VEXT_W:hd * VEXT_W + MLSTM_V_DIM] = (
                v[:, hd * MLSTM_V_DIM:(hd + 1) * MLSTM_V_DIM].astype(BF16))
            nxt.vext[:, hd * VEXT_W + MLSTM_V_DIM:(hd + 1) * VEXT_W] = jnp.ones((rows, MLSTM_V_DIM), BF16)

    def p_so():
        nxt.so[...] = jax.nn.sigmoid(jnp.dot(nxt.h[...], wo_ref[...], preferred_element_type=F32))

    def p_qa():
        shared_vals["tables"] = _rope_tables(pos_ref[0].astype(F32), inv_ref[...])
        qa = jnp.dot(nxt.h[...], wqa_ref[...], preferred_element_type=F32)
        for p in range(ATTN_Q_W // LANES):
            nxt.qa[:, p * LANES:(p + 1) * LANES] = _rope(
                qa[:, p * LANES:(p + 1) * LANES], shared_vals["tables"]).astype(BF16)

    def p_kva():
        kva = jnp.dot(nxt.h[...], wkva_ref[...], preferred_element_type=F32)
        ka = _rope(kva[:, 0:ATTN_KV_W], shared_vals["tables"])
        va = kva[:, ATTN_KV_W:2 * ATTN_KV_W]
        nxt.kc[...] = ka.astype(BF16)
        nxt.kcs[...] = pltpu.roll(ka, ATTN_HEAD_DIM, 1).astype(BF16)
        nxt.vc[...] = va.astype(BF16)
        nxt.vcs[...] = pltpu.roll(va, ATTN_HEAD_DIM, 1).astype(BF16)

    def gate_piece(idx):
        gref, wref = (sgm_scr, wgm_ref) if idx < 2 else (sga_scr, wga_ref)
        c0 = (idx % 2) * MIX_COL_CHUNK
        gref[:, c0:c0 + MIX_COL_CHUNK] = jax.nn.sigmoid(
            jnp.dot(cur.h[...], wref[:, c0:c0 + MIX_COL_CHUNK], preferred_element_type=F32))

    qi = lax.broadcasted_iota(jnp.int32, (CHUNK, CHUNK), 0)
    si = lax.broadcasted_iota(jnp.int32, (CHUNK, CHUNK), 1)
    tri = si <= qi
    lane_lo = lax.broadcasted_iota(jnp.int32, (CHUNK, PAIR_W), 1) < MLSTM_QK_DIM
    zeros_c = jnp.zeros((MLSTM_QK_DIM, VEXT_W), BF16)
    ong = ong_ref[...]
    tile_shape = (CHUNK, CHUNK)
    rep_shape = (MLSTM_HEADS, LANES)

    chunk_stats = []
    m_vec = m_scr[0:MLSTM_HEADS, :]
    for c in range(n_chunks):
        g_c = cur.g[:, c * CHUNK:(c + 1) * CHUNK]
        i_r = g_c[0:MLSTM_HEADS]
        logf_r = _log_sigmoid(g_c[MLSTM_HEADS:GATE_ROWS])
        b_r = _lane_cumsum(logf_r)
        b_last = _rep(b_r[:, CHUNK - 1:CHUNK], rep_shape)
        a_r = b_last - b_r + i_r
        a_max = _rep(jnp.max(a_r, axis=1, keepdims=True), rep_shape)
        m_new = jnp.maximum(b_last + m_vec, a_max)
        chunk_stats.append(dict(
            logf_r=logf_r, ib_r=i_r - b_r, m_prev=m_vec,
            decay=jnp.exp(b_last + m_vec - m_new), scale=jnp.exp(a_max - m_new),
            w_loc=jnp.exp(a_r - a_max)))
        m_vec = m_new
    c_state = [c_scr[hd] for hd in range(MLSTM_HEADS)]
    c_next = [None] * MLSTM_HEADS
    s_qk = {}

    def mlstm_scores(c):
        rs = slice(c * CHUNK, (c + 1) * CHUNK)
        for pair in range(MLSTM_HEADS // 2):
            q2 = cur.qk[rs, pair * PAIR_W:(pair + 1) * PAIR_W]
            k2 = cur.qk[rs, MLSTM_QK_W + pair * PAIR_W:MLSTM_QK_W + (pair + 1) * PAIR_W]
            zk = jnp.zeros_like(k2)
            k_both = jnp.concatenate([jnp.where(lane_lo, k2, zk), jnp.where(lane_lo, zk, k2)], axis=0)
            s2 = lax.dot_general(q2, k_both, _NT, preferred_element_type=F32)
            s_qk[c, 2 * pair] = s2[:, 0:CHUNK]
            s_qk[c, 2 * pair + 1] = s2[:, CHUNK:2 * CHUNK]

    def mlstm_state(c):
        rs = slice(c * CHUNK, (c + 1) * CHUNK)
        st = chunk_stats[c]
        for hd in range(MLSTM_HEADS):
            vext = cur.vext[rs, hd * VEXT_W:(hd + 1) * VEXT_W]
            ktw = (cur.kT[hd * MLSTM_QK_DIM:(hd + 1) * MLSTM_QK_DIM, rs] * st["w_loc"][hd:hd + 1]).astype(BF16)
            u = jnp.dot(ktw, vext, preferred_element_type=F32)
            decay = jnp.concatenate([st["decay"][hd:hd + 1]] * 2, axis=1)
            scale = jnp.concatenate([st["scale"][hd:hd + 1]] * 2, axis=1)
            c_next[hd] = decay * c_state[hd] + scale * u

    def mlstm_out(c):
        rs = slice(c * CHUNK, (c + 1) * CHUNK)
        st = chunk_stats[c]
        for hd in range(MLSTM_HEADS):
            pair, odd = divmod(hd, 2)
            q2 = cur.qk[rs, pair * PAIR_W:(pair + 1) * PAIR_W]
            vext = cur.vext[rs, hd * VEXT_W:(hd + 1) * VEXT_W]
            b_col = _rep(jnp.sum(jnp.where(tri, st["logf_r"][hd:hd + 1], 0.0), axis=1, keepdims=True),
                         tile_shape)
            log_d = jnp.where(tri, b_col + st["ib_r"][hd:hd + 1], -jnp.inf)
            inter_log = b_col + st["m_prev"][hd:hd + 1]
            m_j = jnp.maximum(inter_log, _rep(jnp.max(log_d, axis=1, keepdims=True), tile_shape))
            sqk = (s_qk.pop((c, hd)) * jnp.exp(log_d - m_j)).astype(BF16)
            c_bf = c_state[hd].astype(BF16)
            c_pad = jnp.concatenate([zeros_c, c_bf] if odd else [c_bf, zeros_c], axis=0)
            intra = jnp.dot(sqk, vext, preferred_element_type=F32)
            inter = jnp.dot(q2, c_pad, preferred_element_type=F32)
            inter_scale = jnp.exp(inter_log - m_j)
            num = intra[:, 0:MLSTM_V_DIM] + inter_scale * inter[:, 0:MLSTM_V_DIM]
            den = intra[:, MLSTM_V_DIM:VEXT_W] + inter_scale * inter[:, MLSTM_V_DIM:VEXT_W]
            hh = num / jnp.maximum(jnp.abs(den), jnp.exp(-m_j))
            vs = slice(hd * MLSTM_V_DIM, (hd + 1) * MLSTM_V_DIM)
            hn = hh * lax.rsqrt(jnp.mean(hh * hh, axis=1, keepdims=True) + NORM_EPS) * ong[:, vs]
            hm_scr[rs, vs] = (hn * cur.so[rs, vs]).astype(BF16)
            c_state[hd] = c_next[hd]

    qi2 = lax.broadcasted_iota(jnp.int32, (WINDOW, 2 * WINDOW), 0)
    si2 = lax.broadcasted_iota(jnp.int32, (WINDOW, 2 * WINDOW), 1)
    band = (si2 > qi2) & (si2 <= qi2 + WINDOW)
    band_first = band & ((si2 >= WINDOW) | (t_cur > 0))
    kv_lo = lax.broadcasted_iota(jnp.int32, (2 * WINDOW, LANES), 1) < ATTN_HEAD_DIM
    ones_lo = jnp.where(kv_lo, 1.0, 0.0).astype(BF16)
    ones_hi = jnp.where(kv_lo, 0.0, 1.0).astype(BF16)
    zkv = jnp.zeros((2 * WINDOW, LANES), BF16)
    attn = {}

    def window_rows(tile_ref, prev_ref, nb):
        if nb == 0:
            return jnp.concatenate([prev_ref[...], tile_ref[0:WINDOW, :]], axis=0)
        return tile_ref[(nb - 1) * WINDOW:(nb + 1) * WINDOW, :]

    def attn_scores(nb):
        qs = slice(nb * WINDOW, (nb + 1) * WINDOW)
        mask = band_first if nb == 0 else band
        k_own, k_sw = window_rows(cur.kc, pk_scr, nb), window_rows(cur.kcs, pks_scr, nb)
        for kh in range(ATTN_KV_HEADS):
            own, swapped = (k_own, k_sw) if kh == 0 else (k_sw, k_own)
            k_lo = jnp.where(kv_lo, own, zkv)
            k_hi = jnp.where(kv_lo, zkv, swapped)
            pair0 = kh * (group // 2)
            lhs = jnp.concatenate([cur.qa[qs, (pair0 + j) * LANES:(pair0 + j + 1) * LANES]
                                   for j in range(group // 2)], axis=0)
            probs, sink_terms = [], []
            for odd, k_sel in enumerate((k_lo, k_hi)):
                s_all = lax.dot_general(lhs, k_sel, _NT, preferred_element_type=F32)
                p_rows, sk_rows = [], []
                for j in range(group // 2):
                    sink = sink_ref[kh * group + 2 * j + odd]
                    sc = jnp.where(mask, s_all[j * WINDOW:(j + 1) * WINDOW], -jnp.inf)
                    mx = jnp.maximum(jnp.max(sc, axis=1, keepdims=True), sink)
                    p_rows.append(jnp.exp(sc - mx).astype(BF16))
                    sk_rows.append(_rep(jnp.exp(sink - mx), (WINDOW, LANES)))
                probs.append(jnp.concatenate(p_rows, axis=0))
                sink_terms.append(jnp.concatenate(sk_rows, axis=0))
            attn[nb, kh] = (jnp.concatenate(probs, axis=1), jnp.where(kv_lo, sink_terms[0], sink_terms[1]))

    def attn_values(nb):
        qs = slice(nb * WINDOW, (nb + 1) * WINDOW)
        v_own, v_sw = window_rows(cur.vc, pv_scr, nb), window_rows(cur.vcs, pvs_scr, nb)
        for kh in range(ATTN_KV_HEADS):
            own, swapped = (v_own, v_sw) if kh == 0 else (v_sw, v_own)
            v_lo = jnp.where(kv_lo, own, zkv)
            v_hi = jnp.where(kv_lo, zkv, swapped)
            probs, sink_term = attn.pop((nb, kh))
            rhs = jnp.concatenate([jnp.concatenate([v_lo, ones_lo], axis=1),
                                   jnp.concatenate([v_hi, ones_hi], axis=1)], axis=0)
            o = jnp.dot(probs, rhs, preferred_element_type=F32)
            on = (o[:, 0:LANES] / (o[:, LANES:2 * LANES] + sink_term)).astype(BF16)
            pair0 = kh * (group // 2)
            for j in range(group // 2):
                oa_scr[qs, (pair0 + j) * LANES:(pair0 + j + 1) * LANES] = on[j * WINDOW:(j + 1) * WINDOW]

    def branch_attn(idx):
        c0 = idx * MIX_COL_CHUNK
        y_a = jnp.dot(oa_scr[...], wbra_ref[:, c0:c0 + MIX_COL_CHUNK], preferred_element_type=F32)
        sga_scr[:, c0:c0 + MIX_COL_CHUNK] = sga_scr[:, c0:c0 + MIX_COL_CHUNK] * y_a

    gate_piece(0)
    p_norm()
    attn_scores(0); mlstm_scores(0); mlstm_state(0)
    gate_piece(1)
    attn_values(0); mlstm_out(0)
    p_qk()
    attn_scores(1); mlstm_scores(1); mlstm_state(1)
    p_kT()
    attn_values(1); mlstm_out(1)
    p_v()
    attn_scores(2); mlstm_scores(2); mlstm_state(2)
    gate_piece(2)
    attn_values(2); mlstm_out(2)
    p_so()
    attn_scores(3); mlstm_scores(3); mlstm_state(3)
    gate_piece(3)
    attn_values(3)
    p_qa()
    mlstm_out(3)
    branch_attn(0); p_kva(); branch_attn(1)

    for hd in range(MLSTM_HEADS):
        c_scr[hd] = c_state[hd]
    m_scr[0:MLSTM_HEADS, :] = m_vec
    for prev_ref, tile_ref in ((pk_scr, cur.kc), (pks_scr, cur.kcs), (pv_scr, cur.vc), (pvs_scr, cur.vcs)):
        prev_ref[...] = tile_ref[rows - WINDOW:rows, :]

    for c0, c1 in _col_chunks(D_MODEL, MIX_COL_CHUNK):
        y_m = jnp.dot(hm_scr[...], wbrm_ref[:, c0:c1], preferred_element_type=F32)
        mg_scr[:, c0:c1] = (sgm_scr[:, c0:c1] * y_m + sga_scr[:, c0:c1]).astype(BF16)
    o_ref[0] = cur.xs[...] + jnp.dot(mg_scr[...], wout_ref[...], preferred_element_type=F32)


def _mixer_body(*refs, tiles_per_seq):
    n_in = 18
    n_slot = len(Slot._fields)
    ins = refs[:n_in]
    o_ref = refs[n_in]
    scr = refs[n_in + 1:]
    slot_a, slot_b = Slot(*scr[:n_slot]), Slot(*scr[n_slot:2 * n_slot])
    shared = scr[2 * n_slot:]
    pk_scr, pks_scr, pv_scr, pvs_scr = shared[0:4]
    c_scr, m_scr = shared[-2:]
    s = pl.program_id(0)
    t_cur = lax.rem(s + tiles_per_seq - 1, tiles_per_seq)

    @pl.when(s == 0)
    def _():
        for ref in list(slot_b) + list(shared):
            ref[...] = jnp.zeros_like(ref)

    @pl.when(t_cur == 0)
    def _():
        c_scr[...] = jnp.zeros_like(c_scr)
        m_scr[...] = jnp.zeros_like(m_scr)
        for ref in (pk_scr, pks_scr, pv_scr, pvs_scr):
            ref[...] = jnp.zeros_like(ref)

    even = lax.rem(s, 2) == 0

    @pl.when(even)
    def _():
        _mixer_step(t_cur, ins, slot_b, slot_a, shared, o_ref)

    @pl.when(jnp.logical_not(even))
    def _():
        _mixer_step(t_cur, ins, slot_a, slot_b, shared, o_ref)


def _mixer(x, positions, norm_g, w_in, b_i, b_f, out_norm_g, sinks, w_br_m, w_br_a, w_out):
    b, s, d = x.shape
    starts = [0]
    for w in IN_WIDTHS:
        starts.append(starts[-1] + w)
    col = lambda i: w_in[:, starts[i]:starts[i + 1]]
    q_m, k_m, v_m, o_m, i_m, f_m, q_a, k_a, v_a, g_m, g_a = (col(i) for i in range(len(IN_WIDTHS)))
    w_qk = jnp.concatenate([q_m * (MLSTM_QK_DIM ** -0.5), k_m], axis=1).astype(BF16)
    w_kgT = jnp.concatenate([k_m, i_m, f_m, jnp.zeros((d, KG_ROWS - MLSTM_QK_W - GATE_ROWS), F32)],
                            axis=1).T.astype(BF16)
    w_qa = (q_a * (ATTN_HEAD_DIM ** -0.5)).astype(BF16)
    w_kva = jnp.concatenate([k_a, v_a], axis=1).astype(BF16)
    b_if = jnp.concatenate([b_i, b_f]).reshape(GATE_ROWS, 1)
    half = ROPE_DIM // 2
    inv = (ROPE_THETA ** (-jnp.arange(half, dtype=F32) * 2.0 / ROPE_DIM)).reshape(half, 1)

    rows = MIX_ROWS
    tiles_per_seq = s // rows
    n_tiles = b * tiles_per_seq
    def in_tile(i):
        j = jnp.minimum(i, n_tiles - 1)
        return j // tiles_per_seq, j % tiles_per_seq

    def out_tile(i):
        j = jnp.maximum(i - 1, 0)
        return j // tiles_per_seq, j % tiles_per_seq

    in_specs = [
        pl.BlockSpec((1, rows, d), lambda i: (*in_tile(i), 0)),
        pl.BlockSpec((1, 1, rows), lambda i: (in_tile(i)[0], 0, in_tile(i)[1])),
        _const_spec((1, d)), _const_spec((half, 1)), _const_spec((GATE_ROWS, 1)),
        _const_spec((1, MLSTM_V_W)),
        pl.BlockSpec(memory_space=pltpu.SMEM),
        _const_spec((d, 2 * MLSTM_QK_W)), _const_spec((KG_ROWS, d)), _const_spec((d, MLSTM_V_W)),
        _const_spec((d, MLSTM_V_W)), _const_spec((d, ATTN_Q_W)),
        _const_spec((d, 2 * ATTN_KV_W)), _const_spec((d, d)), _const_spec((d, d)),
        _const_spec((MLSTM_V_W, d)), _const_spec((ATTN_Q_W, d)), _const_spec((d, d)),
    ]
    slot = list(_slot_shapes(rows, d))
    shared = [
        pltpu.VMEM((WINDOW, LANES), BF16),
        pltpu.VMEM((WINDOW, LANES), BF16),
        pltpu.VMEM((WINDOW, LANES), BF16),
        pltpu.VMEM((WINDOW, LANES), BF16),
        pltpu.VMEM((rows, d), F32),
        pltpu.VMEM((rows, d), F32),
        pltpu.VMEM((rows, MLSTM_V_W), BF16),
        pltpu.VMEM((rows, ATTN_Q_W), BF16),
        pltpu.VMEM((rows, d), BF16),
        pltpu.VMEM((MLSTM_HEADS, MLSTM_QK_DIM, VEXT_W), F32),
        pltpu.VMEM((8, LANES), F32),
    ]
    return pl.pallas_call(
        functools.partial(_mixer_body, tiles_per_seq=tiles_per_seq),
        grid=(n_tiles + 1,),
        in_specs=in_specs,
        out_specs=pl.BlockSpec((1, rows, d), lambda i: (*out_tile(i), 0)),
        out_shape=jax.ShapeDtypeStruct((b, s, d), F32),
        scratch_shapes=slot + slot + shared,
        compiler_params=pltpu.CompilerParams(
            dimension_semantics=("arbitrary",), vmem_limit_bytes=(7 * VMEM_BYTES_V7X) // 8),
        name="mixer",
    )(x, positions.reshape(b, 1, s), norm_g.reshape(1, d), inv, b_if, out_norm_g.reshape(1, MLSTM_V_W),
      sinks, w_qk, w_kgT, v_m.astype(BF16), o_m.astype(BF16), w_qa, w_kva,
      g_m.astype(BF16), g_a.astype(BF16), w_br_m.astype(BF16), w_br_a.astype(BF16), w_out.astype(BF16))


def kernel(x, positions, ffn1_norm_g, ffn1_w_gate, ffn1_w_up, ffn1_w_down, mix_norm_g, w_in, mlstm_b_i, mlstm_b_f, mlstm_out_norm_g, attn_sinks, w_branch_mlstm, w_branch_attn, w_out, ffn2_norm_g, ffn2_w_gate, ffn2_w_up, ffn2_w_down, final_norm_g):
    b, s, d = x.shape
    depth = w_in.shape[0]
    for l in range(depth):
        last = l == depth - 1
        x = _ffn(x.reshape(b * s, d), ffn1_norm_g[l], ffn1_w_gate[l], ffn1_w_up[l], ffn1_w_down[l]).reshape(b, s, d)
        x = _mixer(x, positions, mix_norm_g[l], w_in[l], mlstm_b_i[l], mlstm_b_f[l], mlstm_out_norm_g[l],
                   attn_sinks[l], w_branch_mlstm[l], w_branch_attn[l], w_out[l])
        x = _ffn(x.reshape(b * s, d), ffn2_norm_g[l], ffn2_w_gate[l], ffn2_w_up[l], ffn2_w_down[l],
                 final_g=final_norm_g if last else None).reshape(b, s, d)
    return x
```

```python
import collections
import functools

import jax
import jax.numpy as jnp
from jax import lax
from jax.experimental import pallas as pl
from jax.experimental.pallas import tpu as pltpu

D_MODEL = 1024
D_FF = 2816
MLSTM_HEADS = 4
MLSTM_QK_DIM = 64
MLSTM_V_DIM = 128
GATE_SOFTCAP = 15.0
ATTN_Q_HEADS = 8
ATTN_KV_HEADS = 2
ATTN_HEAD_DIM = 64
WINDOW = 128
ROPE_DIM = ATTN_HEAD_DIM // 4
ROPE_THETA = 500000.0
NORM_EPS = 1e-6

MLSTM_QK_W = MLSTM_HEADS * MLSTM_QK_DIM
MLSTM_V_W = MLSTM_HEADS * MLSTM_V_DIM
ATTN_Q_W = ATTN_Q_HEADS * ATTN_HEAD_DIM
ATTN_KV_W = ATTN_KV_HEADS * ATTN_HEAD_DIM
IN_WIDTHS = (MLSTM_QK_W, MLSTM_QK_W, MLSTM_V_W, MLSTM_V_W, MLSTM_HEADS, MLSTM_HEADS,
             ATTN_Q_W, ATTN_KV_W, ATTN_KV_W, D_MODEL, D_MODEL)

LANES = 128
MXU_COLS = 256
VMEM_BYTES_V7X = 64 * 1024 * 1024

FFN_ROWS = 1024
FFN_COL_CHUNK = 2 * MXU_COLS
MIX_ROWS = 512
MIX_COL_CHUNK = 2 * MXU_COLS
CHUNK = LANES
PAIR_W = 2 * MLSTM_QK_DIM
VEXT_W = 2 * MLSTM_V_DIM
GATE_ROWS = 2 * MLSTM_HEADS
KG_ROWS = MLSTM_QK_W + 16
N_MIXER_IN = 18

F32 = jnp.float32
BF16 = jnp.bfloat16
_NT = (((1,), (1,)), ((), ()))


def _rms(x, g):
    return x * lax.rsqrt(jnp.mean(x * x, axis=-1, keepdims=True) + NORM_EPS) * g


def _col_chunks(total, width):
    return [(c, min(c + width, total)) for c in range(0, total, width)]


def _const_spec(shape):
    return pl.BlockSpec(shape, lambda *_: (0,) * len(shape), pipeline_mode=pl.Buffered(1))


def _prep_mixer_weights(win_ref, wbrm_ref, wbra_ref, wout_ref, outs):
    wqk_o, wv_o, wo_o, wqa_o, wkva_o, wgm_o, wga_o, wbrm_o, wbra_o, wout_o = outs
    starts = [0]
    for w in IN_WIDTHS:
        starts.append(starts[-1] + w)
    col = lambda i: win_ref[:, starts[i]:starts[i + 1]]
    wqk_o[...] = jnp.concatenate([col(0) * (MLSTM_QK_DIM ** -0.5), col(1)], axis=1).astype(BF16)
    wv_o[...] = col(2).astype(BF16)
    wo_o[...] = col(3).astype(BF16)
    wqa_o[...] = (col(6) * (ATTN_HEAD_DIM ** -0.5)).astype(BF16)
    wkva_o[...] = jnp.concatenate([col(7), col(8)], axis=1).astype(BF16)
    wgm_o[...] = col(9).astype(BF16)
    wga_o[...] = col(10).astype(BF16)
    wbrm_o[...] = wbrm_ref[...].astype(BF16)
    wbra_o[...] = wbra_ref[...].astype(BF16)
    wout_o[...] = wout_ref[...].astype(BF16)


def _ffn_body(*refs, final_norm, prep):
    n_in = 5 + int(final_norm) + (4 if prep else 0)
    x_ref, g_ref, wg_ref, wu_ref, wd_ref = refs[:5]
    o_ref = refs[n_in]
    act_ref = refs[-1]
    x = x_ref[...]
    h = _rms(x, g_ref[...]).astype(BF16)
    for c0, c1 in _col_chunks(D_FF, FFN_COL_CHUNK):
        g = jnp.dot(h, wg_ref[:, c0:c1], preferred_element_type=F32)
        u = jnp.dot(h, wu_ref[:, c0:c1], preferred_element_type=F32)
        act_ref[:, c0:c1] = (g * jax.nn.sigmoid(g) * u).astype(BF16)
    if prep:
        _prep_mixer_weights(*refs[n_in - 4:n_in], refs[n_in + 1:-1])
    r = x + 0.5 * jnp.dot(act_ref[...], wd_ref[...], preferred_element_type=F32)
    if final_norm:
        r = _rms(r, refs[5][...])
    o_ref[...] = r


def _ffn(x2, norm_g, w_gate, w_up, w_down, final_g=None, mixer_weights=None):
    n, d = x2.shape
    steps = n // FFN_ROWS
    final_norm = final_g is not None
    prep = mixer_weights is not None
    row_spec = pl.BlockSpec((FFN_ROWS, d), lambda i: (i, 0))
    in_specs = [row_spec, _const_spec((1, d)), _const_spec((d, D_FF)), _const_spec((d, D_FF)),
                _const_spec((D_FF, d))]
    args = [x2, norm_g.reshape(1, d), w_gate, w_up, w_down]
    out_specs = [row_spec]
    out_shape = [jax.ShapeDtypeStruct((n, d), F32)]
    if final_norm:
        in_specs.append(_const_spec((1, d)))
        args.append(final_g.reshape(1, d))
    if prep:
        slab = lambda arr: pl.BlockSpec((arr.shape[0] // steps, arr.shape[1]), lambda i: (i, 0))
        for arr in mixer_weights:
            in_specs.append(slab(arr))
            args.append(arr)
        piece_widths = (2 * MLSTM_QK_W, MLSTM_V_W, MLSTM_V_W, ATTN_Q_W, 2 * ATTN_KV_W, d, d)
        pieces = [jax.ShapeDtypeStruct((d, w), BF16) for w in piece_widths]
        pieces += [jax.ShapeDtypeStruct(arr.shape, BF16) for arr in mixer_weights[1:]]
        out_shape += pieces
        out_specs += [slab(p) for p in pieces]
    res = pl.pallas_call(
        functools.partial(_ffn_body, final_norm=final_norm, prep=prep),
        grid=(steps,),
        in_specs=in_specs,
        out_specs=out_specs,
        out_shape=out_shape,
        scratch_shapes=[pltpu.VMEM((FFN_ROWS, D_FF), BF16)],
        compiler_params=pltpu.CompilerParams(
            dimension_semantics=("arbitrary",), vmem_limit_bytes=(3 * VMEM_BYTES_V7X) // 4),
        name="ffn_final" if final_norm else "ffn",
    )(*args)
    return (res[0], res[1:]) if prep else res[0]


Slot = collections.namedtuple("Slot", "xs h qk kT vext so qa kc kcs vc vcs g")


def _slot_shapes(rows, d):
    return Slot(
        xs=pltpu.VMEM((rows, d), F32),
        h=pltpu.VMEM((rows, d), BF16),
        qk=pltpu.VMEM((rows, 2 * MLSTM_QK_W), BF16),
        kT=pltpu.VMEM((MLSTM_QK_W, rows), F32),
        vext=pltpu.VMEM((rows, MLSTM_HEADS * VEXT_W), BF16),
        so=pltpu.VMEM((rows, MLSTM_V_W), F32),
        qa=pltpu.VMEM((rows, ATTN_Q_W), BF16),
        kc=pltpu.VMEM((rows, LANES), BF16),
        kcs=pltpu.VMEM((rows, LANES), BF16),
        vc=pltpu.VMEM((rows, LANES), BF16),
        vcs=pltpu.VMEM((rows, LANES), BF16),
        g=pltpu.VMEM((GATE_ROWS, rows), F32),
    )


def _softcap(a):
    return GATE_SOFTCAP * jnp.tanh(a / GATE_SOFTCAP)


def _log_sigmoid(a):
    return jnp.minimum(a, 0.0) - jnp.log1p(jnp.exp(-jnp.abs(a)))


def _lane_cumsum(a):
    lane = lax.broadcasted_iota(jnp.int32, a.shape, 1)
    d = 1
    while d < LANES:
        a = a + jnp.where(lane >= d, pltpu.roll(a, d, 1), 0.0)
        d *= 2
    return a


def _rope_tables(pos_row, inv_col):
    ang = inv_col * pos_row
    c, s = jnp.cos(ang), jnp.sin(ang)
    one, zero = jnp.ones_like(c), jnp.zeros_like(c)
    per_head = ATTN_HEAD_DIM // 8

    def tile(first, second, rest):
        rows = []
        for _ in range(LANES // ATTN_HEAD_DIM):
            rows += [first, second] + [rest] * (per_head - 2)
        return jnp.concatenate(rows, axis=0).T
    return tile(c, c, one), tile(zero, s, zero), tile(-s, zero, zero)


def _rope(a, tables):
    cos_t, sin_up, sin_dn = tables
    half = ROPE_DIM // 2
    return a * cos_t + pltpu.roll(a, half, 1) * sin_up + pltpu.roll(a, LANES - half, 1) * sin_dn


def _rep(a, shape):
    return jnp.broadcast_to(a, shape)


def _mixer_step(t_cur, ins, cur, nxt, shared, outs):
    (x_ref, pos_ref, ng_ref, inv_ref, bif_ref, ong_ref, sink_ref,
     wqk_ref, wkgT_ref, wv_ref, wo_ref, wqa_ref, wkva_ref, wgm_ref, wga_ref,
     wbrm_ref, wbra_ref, wout_ref) = ins[:N_MIXER_IN]
    cast_in = ins[N_MIXER_IN:]
    o_ref, cast_out = outs[0], outs[1:]
    pk_scr, pks_scr, pv_scr, pvs_scr, sgm_scr, sga_scr, hm_scr, oa_scr, mg_scr, c_scr, m_scr = shared
    rows = MIX_ROWS
    n_chunks = rows // CHUNK
    n_blocks = rows // WINDOW
    group = ATTN_Q_HEADS // ATTN_KV_HEADS

    shared_vals = {}

    def p_norm():
        x_new = x_ref[0]
        nxt.xs[...] = x_new
        nxt.h[...] = _rms(x_new, ng_ref[...]).astype(BF16)

    def side_cast():
        for src, dst in zip(cast_in, cast_out):
            dst[...] = src[...].astype(BF16)

    def p_qk():
        nxt.qk[...] = jnp.dot(nxt.h[...], wqk_ref[...], preferred_element_type=F32).astype(BF16)

    def p_kT():
        kg = lax.dot_general(wkgT_ref[...], nxt.h[...], _NT, preferred_element_type=F32)
        nxt.kT[...] = kg[0:MLSTM_QK_W]
        nxt.g[...] = _softcap(kg[MLSTM_QK_W:MLSTM_QK_W + GATE_ROWS] + bif_ref[...])

    def p_v():
        v = jnp.dot(nxt.h[...], wv_ref[...], preferred_element_type=F32)
        for hd in range(MLSTM_HEADS):
            nxt.vext[:, hd * VEXT_W:hd * VEXT_W + MLSTM_V_DIM] = (
                v[:, hd * MLSTM_V_DIM:(hd + 1) * MLSTM_V_DIM].astype(BF16))
            nxt.vext[:, hd * VEXT_W + MLSTM_V_DIM:(hd + 1) * VEXT_W] = jnp.ones((rows, MLSTM_V_DIM), BF16)

    def p_so():
        nxt.so[...] = jax.nn.sigmoid(jnp.dot(nxt.h[...], wo_ref[...], preferred_element_type=F32))

    def p_qa():
        shared_vals["tables"] = _rope_tables(pos_ref[0].astype(F32), inv_ref[...])
        qa = jnp.dot(nxt.h[...], wqa_ref[...], preferred_element_type=F32)
        for p in range(ATTN_Q_W // LANES):
            nxt.qa[:, p * LANES:(p + 1) * LANES] = _rope(
                qa[:, p * LANES:(p + 1) * LANES], shared_vals["tables"]).astype(BF16)

    def p_kva():
        kva = jnp.dot(nxt.h[...], wkva_ref[...], preferred_element_type=F32)
        ka = _rope(kva[:, 0:ATTN_KV_W], shared_vals["tables"])
        va = kva[:, ATTN_KV_W:2 * ATTN_KV_W]
        nxt.kc[...] = ka.astype(BF16)
        nxt.kcs[...] = pltpu.roll(ka, ATTN_HEAD_DIM, 1).astype(BF16)
        nxt.vc[...] = va.astype(BF16)
        nxt.vcs[...] = pltpu.roll(va, ATTN_HEAD_DIM, 1).astype(BF16)

    def gate_piece(idx):
        gref, wref = (sgm_scr, wgm_ref) if idx < 2 else (sga_scr, wga_ref)
        c0 = (idx % 2) * MIX_COL_CHUNK
        gref[:, c0:c0 + MIX_COL_CHUNK] = jax.nn.sigmoid(
            jnp.dot(cur.h[...], wref[:, c0:c0 + MIX_COL_CHUNK], preferred_element_type=F32))

    qi = lax.broadcasted_iota(jnp.int32, (CHUNK, CHUNK), 0)
    si = lax.broadcasted_iota(jnp.int32, (CHUNK, CHUNK), 1)
    tri = si <= qi
    lane_lo = lax.broadcasted_iota(jnp.int32, (CHUNK, PAIR_W), 1) < MLSTM_QK_DIM
    zeros_c = jnp.zeros((MLSTM_QK_DIM, VEXT_W), BF16)
    ong = ong_ref[...]
    tile_shape = (CHUNK, CHUNK)
    rep_shape = (MLSTM_HEADS, LANES)

    chunk_stats = []
    m_vec = m_scr[0:MLSTM_HEADS, :]
    for c in range(n_chunks):
        g_c = cur.g[:, c * CHUNK:(c + 1) * CHUNK]
        i_r = g_c[0:MLSTM_HEADS]
        logf_r = _log_sigmoid(g_c[MLSTM_HEADS:GATE_ROWS])
        b_r = _lane_cumsum(logf_r)
        b_last = _rep(b_r[:, CHUNK - 1:CHUNK], rep_shape)
        a_r = b_last - b_r + i_r
        a_max = _rep(jnp.max(a_r, axis=1, keepdims=True), rep_shape)
        m_new = jnp.maximum(b_last + m_vec, a_max)
        chunk_stats.append(dict(
            logf_r=logf_r, ib_r=i_r - b_r, m_prev=m_vec,
            decay=jnp.exp(b_last + m_vec - m_new), scale=jnp.exp(a_max - m_new),
            w_loc=jnp.exp(a_r - a_max)))
        m_vec = m_new
    c_state = [c_scr[hd] for hd in range(MLSTM_HEADS)]
    c_next = [None] * MLSTM_HEADS
    s_qk = {}

    def mlstm_scores(c):
        rs = slice(c * CHUNK, (c + 1) * CHUNK)
        for pair in range(MLSTM_HEADS // 2):
            q2 = cur.qk[rs, pair * PAIR_W:(pair + 1) * PAIR_W]
            k2 = cur.qk[rs, MLSTM_QK_W + pair * PAIR_W:MLSTM_QK_W + (pair + 1) * PAIR_W]
            zk = jnp.zeros_like(k2)
            k_both = jnp.concatenate([jnp.where(lane_lo, k2, zk), jnp.where(lane_lo, zk, k2)], axis=0)
            s2 = lax.dot_general(q2, k_both, _NT, preferred_element_type=F32)
            s_qk[c, 2 * pair] = s2[:, 0:CHUNK]
            s_qk[c, 2 * pair + 1] = s2[:, CHUNK:2 * CHUNK]

    def mlstm_state(c):
        rs = slice(c * CHUNK, (c + 1) * CHUNK)
        st = chunk_stats[c]
        for hd in range(MLSTM_HEADS):
            vext = cur.vext[rs, hd * VEXT_W:(hd + 1) * VEXT_W]
            ktw = (cur.kT[hd * MLSTM_QK_DIM:(hd + 1) * MLSTM_QK_DIM, rs] * st["w_loc"][hd:hd + 1]).astype(BF16)
            u = jnp.dot(ktw, vext, preferred_element_type=F32)
            decay = jnp.concatenate([st["decay"][hd:hd + 1]] * 2, axis=1)
            scale = jnp.concatenate([st["scale"][hd:hd + 1]] * 2, axis=1)
            c_next[hd] = decay * c_state[hd] + scale * u

    def mlstm_out(c):
        rs = slice(c * CHUNK, (c + 1) * CHUNK)
        st = chunk_stats[c]
        for hd in range(MLSTM_HEADS):
            pair, odd = divmod(hd, 2)
            q2 = cur.qk[rs, pair * PAIR_W:(pair + 1) * PAIR_W]
            vext = cur.vext[rs, hd * VEXT_W:(hd + 1) * VEXT_W]
            b_col = _rep(jnp.sum(jnp.where(tri, st["logf_r"][hd:hd + 1], 0.0), axis=1, keepdims=True),
                         tile_shape)
            log_d = jnp.where(tri, b_col + st["ib_r"][hd:hd + 1], -jnp.inf)
            inter_log = b_col + st["m_prev"][hd:hd + 1]
            m_j = jnp.maximum(inter_log, _rep(jnp.max(log_d, axis=1, keepdims=True), tile_shape))
            sqk = (s_qk.pop((c, hd)) * jnp.exp(log_d - m_j)).astype(BF16)
            c_bf = c_state[hd].astype(BF16)
            c_pad = jnp.concatenate([zeros_c, c_bf] if odd else [c_bf, zeros_c], axis=0)
            intra = jnp.dot(sqk, vext, preferred_element_type=F32)
            inter = jnp.dot(q2, c_pad, preferred_element_type=F32)
            inter_scale = jnp.exp(inter_log - m_j)
            num = intra[:, 0:MLSTM_V_DIM] + inter_scale * inter[:, 0:MLSTM_V_DIM]
            den = intra[:, MLSTM_V_DIM:VEXT_W] + inter_scale * inter[:, MLSTM_V_DIM:VEXT_W]
            hh = num / jnp.maximum(jnp.abs(den), jnp.exp(-m_j))
            vs = slice(hd * MLSTM_V_DIM, (hd + 1) * MLSTM_V_DIM)
            hn = hh * lax.rsqrt(jnp.mean(hh * hh, axis=1, keepdims=True) + NORM_EPS) * ong[:, vs]
            hm_scr[rs, vs] = (hn * cur.so[rs, vs]).astype(BF16)
            c_state[hd] = c_next[hd]

    qi2 = lax.broadcasted_iota(jnp.int32, (WINDOW, 2 * WINDOW), 0)
    si2 = lax.broadcasted_iota(jnp.int32, (WINDOW, 2 * WINDOW), 1)
    band = (si2 > qi2) & (si2 <= qi2 + WINDOW)
    band_first = band & ((si2 >= WINDOW) | (t_cur > 0))
    kv_lo = lax.broadcasted_iota(jnp.int32, (2 * WINDOW, LANES), 1) < ATTN_HEAD_DIM
    ones_lo = jnp.where(kv_lo, 1.0, 0.0).astype(BF16)
    ones_hi = jnp.where(kv_lo, 0.0, 1.0).astype(BF16)
    zkv = jnp.zeros((2 * WINDOW, LANES), BF16)
    attn = {}

    def window_rows(tile_ref, prev_ref, nb):
        if nb == 0:
            return jnp.concatenate([prev_ref[...], tile_ref[0:WINDOW, :]], axis=0)
        return tile_ref[(nb - 1) * WINDOW:(nb + 1) * WINDOW, :]

    def attn_scores(nb):
        qs = slice(nb * WINDOW, (nb + 1) * WINDOW)
        mask = band_first if nb == 0 else band
        k_own, k_sw = window_rows(cur.kc, pk_scr, nb), window_rows(cur.kcs, pks_scr, nb)
        for kh in range(ATTN_KV_HEADS):
            own, swapped = (k_own, k_sw) if kh == 0 else (k_sw, k_own)
            k_lo = jnp.where(kv_lo, own, zkv)
            k_hi = jnp.where(kv_lo, zkv, swapped)
            pair0 = kh * (group // 2)
            lhs = jnp.concatenate([cur.qa[qs, (pair0 + j) * LANES:(pair0 + j + 1) * LANES]
                                   for j in range(group // 2)], axis=0)
            probs, sink_terms = [], []
            for odd, k_sel in enumerate((k_lo, k_hi)):
                s_all = lax.dot_general(lhs, k_sel, _NT, preferred_element_type=F32)
                p_rows, sk_rows = [], []
                for j in range(group // 2):
                    sink = sink_ref[kh * group + 2 * j + odd]
                    sc = jnp.where(mask, s_all[j * WINDOW:(j + 1) * WINDOW], -jnp.inf)
                    mx = jnp.maximum(jnp.max(sc, axis=1, keepdims=True), sink)
                    p_rows.append(jnp.exp(sc - mx).astype(BF16))
                    sk_rows.append(_rep(jnp.exp(sink - mx), (WINDOW, LANES)))
                probs.append(jnp.concatenate(p_rows, axis=0))
                sink_terms.append(jnp.concatenate(sk_rows, axis=0))
            attn[nb, kh] = (jnp.concatenate(probs, axis=1), jnp.where(kv_lo, sink_terms[0], sink_terms[1]))

    def attn_values(nb):
        qs = slice(nb * WINDOW, (nb + 1) * WINDOW)
        v_own, v_sw = window_rows(cur.vc, pv_scr, nb), window_rows(cur.vcs, pvs_scr, nb)
        for kh in range(ATTN_KV_HEADS):
            own, swapped = (v_own, v_sw) if kh == 0 else (v_sw, v_own)
            v_lo = jnp.where(kv_lo, own, zkv)
            v_hi = jnp.where(kv_lo, zkv, swapped)
            probs, sink_term = attn.pop((nb, kh))
            rhs = jnp.concatenate([jnp.concatenate([v_lo, ones_lo], axis=1),
                                   jnp.concatenate([v_hi, ones_hi], axis=1)], axis=0)
            o = jnp.dot(probs, rhs, preferred_element_type=F32)
            on = (o[:, 0:LANES] / (o[:, LANES:2 * LANES] + sink_term)).astype(BF16)
            pair0 = kh * (group // 2)
            for j in range(group // 2):
                oa_scr[qs, (pair0 + j) * LANES:(pair0 + j + 1) * LANES] = on[j * WINDOW:(j + 1) * WINDOW]

    def branch_attn(idx):
        c0 = idx * MIX_COL_CHUNK
        y_a = jnp.dot(oa_scr[...], wbra_ref[:, c0:c0 + MIX_COL_CHUNK], preferred_element_type=F32)
        sga_scr[:, c0:c0 + MIX_COL_CHUNK] = sga_scr[:, c0:c0 + MIX_COL_CHUNK] * y_a

    gate_piece(0)
    p_norm()
    side_cast()
    attn_scores(0); mlstm_scores(0); mlstm_state(0)
    gate_piece(1)
    attn_values(0); mlstm_out(0)
    p_qk()
    attn_scores(1); mlstm_scores(1); mlstm_state(1)
    p_kT()
    attn_values(1); mlstm_out(1)
    p_v()
    attn_scores(2); mlstm_scores(2); mlstm_state(2)
    gate_piece(2)
    attn_values(2); mlstm_out(2)
    p_so()
    attn_scores(3); mlstm_scores(3); mlstm_state(3)
    gate_piece(3)
    attn_values(3)
    p_qa()
    mlstm_out(3)
    branch_attn(0); p_kva(); branch_attn(1)

    for hd in range(MLSTM_HEADS):
        c_scr[hd] = c_state[hd]
    m_scr[0:MLSTM_HEADS, :] = m_vec
    for prev_ref, tile_ref in ((pk_scr, cur.kc), (pks_scr, cur.kcs), (pv_scr, cur.vc), (pvs_scr, cur.vcs)):
        prev_ref[...] = tile_ref[rows - WINDOW:rows, :]

    for c0, c1 in _col_chunks(D_MODEL, MIX_COL_CHUNK):
        y_m = jnp.dot(hm_scr[...], wbrm_ref[:, c0:c1], preferred_element_type=F32)
        mg_scr[:, c0:c1] = (sgm_scr[:, c0:c1] * y_m + sga_scr[:, c0:c1]).astype(BF16)
    o_ref[0] = cur.xs[...] + jnp.dot(mg_scr[...], wout_ref[...], preferred_element_type=F32)


def _mixer_body(*refs, tiles_per_seq, n_cast):
    n_in = N_MIXER_IN + n_cast
    n_out = 1 + n_cast
    n_slot = len(Slot._fields)
    ins = refs[:n_in]
    outs = refs[n_in:n_in + n_out]
    scr = refs[n_in + n_out:]
    slot_a, slot_b = Slot(*scr[:n_slot]), Slot(*scr[n_slot:2 * n_slot])
    shared = scr[2 * n_slot:]
    pk_scr, pks_scr, pv_scr, pvs_scr = shared[0:4]
    c_scr, m_scr = shared[-2:]
    s = pl.program_id(0)
    t_cur = lax.rem(s + tiles_per_seq - 1, tiles_per_seq)

    @pl.when(s == 0)
    def _():
        for ref in list(slot_b) + list(shared):
            ref[...] = jnp.zeros_like(ref)

    @pl.when(t_cur == 0)
    def _():
        c_scr[...] = jnp.zeros_like(c_scr)
        m_scr[...] = jnp.zeros_like(m_scr)
        for ref in (pk_scr, pks_scr, pv_scr, pvs_scr):
            ref[...] = jnp.zeros_like(ref)

    even = lax.rem(s, 2) == 0

    @pl.when(even)
    def _():
        _mixer_step(t_cur, ins, slot_b, slot_a, shared, outs)

    @pl.when(jnp.logical_not(even))
    def _():
        _mixer_step(t_cur, ins, slot_a, slot_b, shared, outs)


def _cast_slabs(arr, steps):
    rows = arr.shape[0]
    slab = 16
    while rows % slab or rows // slab > steps:
        slab += 16
    n_slabs = rows // slab
    return pl.BlockSpec((slab, arr.shape[1]), lambda i: (jnp.minimum(i, n_slabs - 1), 0))


def _mixer(x, positions, norm_g, w_in, b_i, b_f, out_norm_g, sinks, weights, cast_weights):
    b, s, d = x.shape
    w_qk, w_v, w_o, w_qa, w_kva, w_gm, w_ga, w_brm, w_bra, w_outp = weights
    starts = [0]
    for w in IN_WIDTHS:
        starts.append(starts[-1] + w)
    col = lambda i: w_in[:, starts[i]:starts[i + 1]]
    w_kgT = jnp.concatenate([col(1), col(4), col(5), jnp.zeros((d, KG_ROWS - MLSTM_QK_W - GATE_ROWS), F32)],
                            axis=1).T.astype(BF16)
    b_if = jnp.concatenate([b_i, b_f]).reshape(GATE_ROWS, 1)
    half = ROPE_DIM // 2
    inv = (ROPE_THETA ** (-jnp.arange(half, dtype=F32) * 2.0 / ROPE_DIM)).reshape(half, 1)

    rows = MIX_ROWS
    tiles_per_seq = s // rows
    n_tiles = b * tiles_per_seq
    def in_tile(i):
        j = jnp.minimum(i, n_tiles - 1)
        return j // tiles_per_seq, j % tiles_per_seq

    def out_tile(i):
        j = jnp.maximum(i - 1, 0)
        return j // tiles_per_seq, j % tiles_per_seq

    in_specs = [
        pl.BlockSpec((1, rows, d), lambda i: (*in_tile(i), 0)),
        pl.BlockSpec((1, 1, rows), lambda i: (in_tile(i)[0], 0, in_tile(i)[1])),
        _const_spec((1, d)), _const_spec((half, 1)), _const_spec((GATE_ROWS, 1)),
        _const_spec((1, MLSTM_V_W)),
        pl.BlockSpec(memory_space=pltpu.SMEM),
        _const_spec((d, 2 * MLSTM_QK_W)), _const_spec((KG_ROWS, d)), _const_spec((d, MLSTM_V_W)),
        _const_spec((d, MLSTM_V_W)), _const_spec((d, ATTN_Q_W)),
        _const_spec((d, 2 * ATTN_KV_W)), _const_spec((d, d)), _const_spec((d, d)),
        _const_spec((MLSTM_V_W, d)), _const_spec((ATTN_Q_W, d)), _const_spec((d, d)),
    ]
    slot = list(_slot_shapes(rows, d))
    shared = [
        pltpu.VMEM((WINDOW, LANES), BF16),
        pltpu.VMEM((WINDOW, LANES), BF16),
        pltpu.VMEM((WINDOW, LANES), BF16),
        pltpu.VMEM((WINDOW, LANES), BF16),
        pltpu.VMEM((rows, d), F32),
        pltpu.VMEM((rows, d), F32),
        pltpu.VMEM((rows, MLSTM_V_W), BF16),
        pltpu.VMEM((rows, ATTN_Q_W), BF16),
        pltpu.VMEM((rows, d), BF16),
        pltpu.VMEM((MLSTM_HEADS, MLSTM_QK_DIM, VEXT_W), F32),
        pltpu.VMEM((8, LANES), F32),
    ]
    assert len(in_specs) == N_MIXER_IN
    steps = n_tiles + 1
    cast_specs = [_cast_slabs(arr, steps) for arr in cast_weights]
    res = pl.pallas_call(
        functools.partial(_mixer_body, tiles_per_seq=tiles_per_seq, n_cast=len(cast_weights)),
        grid=(steps,),
        in_specs=in_specs + cast_specs,
        out_specs=[pl.BlockSpec((1, rows, d), lambda i: (*out_tile(i), 0))] + cast_specs,
        out_shape=[jax.ShapeDtypeStruct((b, s, d), F32)]
        + [jax.ShapeDtypeStruct(arr.shape, BF16) for arr in cast_weights],
        scratch_shapes=slot + slot + shared,
        compiler_params=pltpu.CompilerParams(
            dimension_semantics=("arbitrary",), vmem_limit_bytes=(7 * VMEM_BYTES_V7X) // 8),
        name="mixer",
    )(x, positions.reshape(b, 1, s), norm_g.reshape(1, d), inv, b_if, out_norm_g.reshape(1, MLSTM_V_W),
      sinks, w_qk, w_kgT, w_v, w_o, w_qa, w_kva, w_gm, w_ga, w_brm, w_bra, w_outp, *cast_weights)
    return res[0], res[1:]


def kernel(x, positions, ffn1_norm_g, ffn1_w_gate, ffn1_w_up, ffn1_w_down, mix_norm_g, w_in, mlstm_b_i, mlstm_b_f, mlstm_out_norm_g, attn_sinks, w_branch_mlstm, w_branch_attn, w_out, ffn2_norm_g, ffn2_w_gate, ffn2_w_up, ffn2_w_down, final_norm_g):
    b, s, d = x.shape
    depth = w_in.shape[0]
    for l in range(depth):
        last = l == depth - 1
        x2, mixer_w = _ffn(x.reshape(b * s, d), ffn1_norm_g[l], ffn1_w_gate[l].astype(BF16),
                           ffn1_w_up[l].astype(BF16), ffn1_w_down[l].astype(BF16),
                           mixer_weights=(w_in[l], w_branch_mlstm[l], w_branch_attn[l], w_out[l]))
        x3, ffn2_w = _mixer(x2.reshape(b, s, d), positions, mix_norm_g[l], w_in[l], mlstm_b_i[l], mlstm_b_f[l],
                            mlstm_out_norm_g[l], attn_sinks[l], mixer_w,
                            (ffn2_w_gate[l], ffn2_w_up[l], ffn2_w_down[l]))
        x = _ffn(x3.reshape(b * s, d), ffn2_norm_g[l], *ffn2_w,
                 final_g=final_norm_g if last else None).reshape(b, s, d)
    return x
```

```python
import collections
import functools

import jax
import jax.numpy as jnp
from jax import lax
from jax.experimental import pallas as pl
from jax.experimental.pallas import tpu as pltpu

D_MODEL = 1024
D_FF = 2816
MLSTM_HEADS = 4
MLSTM_QK_DIM = 64
MLSTM_V_DIM = 128
GATE_SOFTCAP = 15.0
ATTN_Q_HEADS = 8
ATTN_KV_HEADS = 2
ATTN_HEAD_DIM = 64
WINDOW = 128
ROPE_DIM = ATTN_HEAD_DIM // 4
ROPE_THETA = 500000.0
NORM_EPS = 1e-6

MLSTM_QK_W = MLSTM_HEADS * MLSTM_QK_DIM
MLSTM_V_W = MLSTM_HEADS * MLSTM_V_DIM
ATTN_Q_W = ATTN_Q_HEADS * ATTN_HEAD_DIM
ATTN_KV_W = ATTN_KV_HEADS * ATTN_HEAD_DIM
IN_WIDTHS = (MLSTM_QK_W, MLSTM_QK_W, MLSTM_V_W, MLSTM_V_W, MLSTM_HEADS, MLSTM_HEADS,
             ATTN_Q_W, ATTN_KV_W, ATTN_KV_W, D_MODEL, D_MODEL)

LANES = 128
MXU_COLS = 256
VMEM_BYTES_V7X = 64 * 1024 * 1024

FFN_ROWS = 1024
FFN_COL_CHUNK = 2 * MXU_COLS
MIX_ROWS = 512
MIX_COL_CHUNK = 2 * MXU_COLS
CHUNK = LANES
PAIR_W = 2 * MLSTM_QK_DIM
VEXT_W = 2 * MLSTM_V_DIM
GATE_ROWS = 2 * MLSTM_HEADS
QKG_W = 2 * MLSTM_QK_W + LANES
N_MIXER_IN = 17

F32 = jnp.float32
BF16 = jnp.bfloat16
_NT = (((1,), (1,)), ((), ()))


def _rms(x, g):
    return x * lax.rsqrt(jnp.mean(x * x, axis=-1, keepdims=True) + NORM_EPS) * g


def _col_chunks(total, width):
    return [(c, min(c + width, total)) for c in range(0, total, width)]


def _const_spec(shape):
    return pl.BlockSpec(shape, lambda *_: (0,) * len(shape), pipeline_mode=pl.Buffered(1))


def _prep_mixer_weights(win_ref, wbrm_ref, wbra_ref, wout_ref, outs):
    wqk_o, wv_o, wo_o, wqa_o, wkva_o, wgm_o, wga_o, wbrm_o, wbra_o, wout_o = outs
    starts = [0]
    for w in IN_WIDTHS:
        starts.append(starts[-1] + w)
    col = lambda i: win_ref[:, starts[i]:starts[i + 1]]
    gate_pad = jnp.zeros((win_ref.shape[0], LANES - GATE_ROWS), F32)
    wqk_o[...] = jnp.concatenate([col(0) * (MLSTM_QK_DIM ** -0.5), col(1), col(4), col(5), gate_pad],
                                 axis=1).astype(BF16)
    wv_o[...] = col(2).astype(BF16)
    wo_o[...] = col(3).astype(BF16)
    wqa_o[...] = (col(6) * (ATTN_HEAD_DIM ** -0.5)).astype(BF16)
    wkva_o[...] = jnp.concatenate([col(7), col(8)], axis=1).astype(BF16)
    wgm_o[...] = col(9).astype(BF16)
    wga_o[...] = col(10).astype(BF16)
    wbrm_o[...] = wbrm_ref[...].astype(BF16)
    wbra_o[...] = wbra_ref[...].astype(BF16)
    wout_o[...] = wout_ref[...].astype(BF16)


def _ffn_body(*refs, final_norm, prep):
    n_in = 5 + int(final_norm) + (4 if prep else 0)
    x_ref, g_ref, wg_ref, wu_ref, wd_ref = refs[:5]
    o_ref = refs[n_in]
    act_ref = refs[-1]
    x = x_ref[...]
    h = _rms(x, g_ref[...]).astype(BF16)
    for c0, c1 in _col_chunks(D_FF, FFN_COL_CHUNK):
        g = jnp.dot(h, wg_ref[:, c0:c1], preferred_element_type=F32)
        u = jnp.dot(h, wu_ref[:, c0:c1], preferred_element_type=F32)
        act_ref[:, c0:c1] = (g * jax.nn.sigmoid(g) * u).astype(BF16)
    if prep:
        _prep_mixer_weights(*refs[n_in - 4:n_in], refs[n_in + 1:-1])
    r = x + 0.5 * jnp.dot(act_ref[...], wd_ref[...], preferred_element_type=F32)
    if final_norm:
        r = _rms(r, refs[5][...])
    o_ref[...] = r


def _ffn(x2, norm_g, w_gate, w_up, w_down, final_g=None, mixer_weights=None):
    n, d = x2.shape
    steps = n // FFN_ROWS
    final_norm = final_g is not None
    prep = mixer_weights is not None
    row_spec = pl.BlockSpec((FFN_ROWS, d), lambda i: (i, 0))
    in_specs = [row_spec, _const_spec((1, d)), _const_spec((d, D_FF)), _const_spec((d, D_FF)),
                _const_spec((D_FF, d))]
    args = [x2, norm_g.reshape(1, d), w_gate, w_up, w_down]
    out_specs = [row_spec]
    out_shape = [jax.ShapeDtypeStruct((n, d), F32)]
    if final_norm:
        in_specs.append(_const_spec((1, d)))
        args.append(final_g.reshape(1, d))
    if prep:
        slab = lambda arr: pl.BlockSpec((arr.shape[0] // steps, arr.shape[1]), lambda i: (i, 0))
        for arr in mixer_weights:
            in_specs.append(slab(arr))
            args.append(arr)
        piece_widths = (QKG_W, MLSTM_V_W, MLSTM_V_W, ATTN_Q_W, 2 * ATTN_KV_W, d, d)
        pieces = [jax.ShapeDtypeStruct((d, w), BF16) for w in piece_widths]
        pieces += [jax.ShapeDtypeStruct(arr.shape, BF16) for arr in mixer_weights[1:]]
        out_shape += pieces
        out_specs += [slab(p) for p in pieces]
    res = pl.pallas_call(
        functools.partial(_ffn_body, final_norm=final_norm, prep=prep),
        grid=(steps,),
        in_specs=in_specs,
        out_specs=out_specs,
        out_shape=out_shape,
        scratch_shapes=[pltpu.VMEM((FFN_ROWS, D_FF), BF16)],
        compiler_params=pltpu.CompilerParams(
            dimension_semantics=("arbitrary",), vmem_limit_bytes=(3 * VMEM_BYTES_V7X) // 4),
        name="ffn_final" if final_norm else "ffn",
    )(*args)
    return (res[0], res[1:]) if prep else res[0]


Slot = collections.namedtuple("Slot", "xs h qk kT vext so qa kc kcs vc vcs g")


def _slot_shapes(rows, d):
    return Slot(
        xs=pltpu.VMEM((rows, d), F32),
        h=pltpu.VMEM((rows, d), BF16),
        qk=pltpu.VMEM((rows, 2 * MLSTM_QK_W), BF16),
        kT=pltpu.VMEM((MLSTM_QK_W, rows), F32),
        vext=pltpu.VMEM((rows, MLSTM_HEADS * VEXT_W), BF16),
        so=pltpu.VMEM((rows, MLSTM_V_W), F32),
        qa=pltpu.VMEM((rows, ATTN_Q_W), BF16),
        kc=pltpu.VMEM((rows, LANES), BF16),
        kcs=pltpu.VMEM((rows, LANES), BF16),
        vc=pltpu.VMEM((rows, LANES), BF16),
        vcs=pltpu.VMEM((rows, LANES), BF16),
        g=pltpu.VMEM((GATE_ROWS, rows), F32),
    )


def _softcap(a):
    return GATE_SOFTCAP * jnp.tanh(a / GATE_SOFTCAP)


def _log_sigmoid(a):
    return jnp.minimum(a, 0.0) - jnp.log1p(jnp.exp(-jnp.abs(a)))


def _lane_cumsum(a):
    lane = lax.broadcasted_iota(jnp.int32, a.shape, 1)
    d = 1
    while d < LANES:
        a = a + jnp.where(lane >= d, pltpu.roll(a, d, 1), 0.0)
        d *= 2
    return a


def _rope_tables(pos_row, inv_col):
    ang = inv_col * pos_row
    c, s = jnp.cos(ang), jnp.sin(ang)
    one, zero = jnp.ones_like(c), jnp.zeros_like(c)
    per_head = ATTN_HEAD_DIM // 8

    def tile(first, second, rest):
        rows = []
        for _ in range(LANES // ATTN_HEAD_DIM):
            rows += [first, second] + [rest] * (per_head - 2)
        return jnp.concatenate(rows, axis=0).T
    return tile(c, c, one), tile(zero, s, zero), tile(-s, zero, zero)


def _rope(a, tables):
    cos_t, sin_up, sin_dn = tables
    half = ROPE_DIM // 2
    return a * cos_t + pltpu.roll(a, half, 1) * sin_up + pltpu.roll(a, LANES - half, 1) * sin_dn


def _rep(a, shape):
    return jnp.broadcast_to(a, shape)


def _mixer_step(t_cur, ins, cur, nxt, shared, outs):
    (x_ref, pos_ref, ng_ref, inv_ref, bif_ref, ong_ref, sink_ref,
     wqk_ref, wv_ref, wo_ref, wqa_ref, wkva_ref, wgm_ref, wga_ref,
     wbrm_ref, wbra_ref, wout_ref) = ins[:N_MIXER_IN]
    cast_in = ins[N_MIXER_IN:]
    o_ref, cast_out = outs[0], outs[1:]
    pk_scr, pks_scr, pv_scr, pvs_scr, sgm_scr, sga_scr, hm_scr, oa_scr, mg_scr, c_scr, m_scr = shared
    rows = MIX_ROWS
    n_chunks = rows // CHUNK
    n_blocks = rows // WINDOW
    group = ATTN_Q_HEADS // ATTN_KV_HEADS

    shared_vals = {}

    def p_norm():
        x_new = x_ref[0]
        nxt.xs[...] = x_new
        nxt.h[...] = _rms(x_new, ng_ref[...]).astype(BF16)

    def side_cast():
        for src, dst in zip(cast_in, cast_out):
            dst[...] = src[...].astype(BF16)

    def p_qk():
        qkg = jnp.dot(nxt.h[...], wqk_ref[...], preferred_element_type=F32)
        nxt.qk[...] = qkg[:, 0:2 * MLSTM_QK_W].astype(BF16)
        nxt.kT[...] = qkg[:, MLSTM_QK_W:2 * MLSTM_QK_W].T
        gates_t = qkg[:, 2 * MLSTM_QK_W:QKG_W].T
        nxt.g[...] = _softcap(gates_t[0:GATE_ROWS] + bif_ref[...])

    def p_v():
        v = jnp.dot(nxt.h[...], wv_ref[...], preferred_element_type=F32)
        for hd in range(MLSTM_HEADS):
            nxt.vext[:, hd * VEXT_W:hd * VEXT_W + MLSTM_V_DIM] = (
                v[:, hd * MLSTM_V_DIM:(hd + 1) * MLSTM_V_DIM].astype(BF16))
            nxt.vext[:, hd * VEXT_W + MLSTM_V_DIM:(hd + 1) * VEXT_W] = jnp.ones((rows, MLSTM_V_DIM), BF16)

    def p_so():
        nxt.so[...] = jax.nn.sigmoid(jnp.dot(nxt.h[...], wo_ref[...], preferred_element_type=F32))

    def p_qa():
        shared_vals["tables"] = _rope_tables(pos_ref[0].astype(F32), inv_ref[...])
        qa = jnp.dot(nxt.h[...], wqa_ref[...], preferred_element_type=F32)
        for p in range(ATTN_Q_W // LANES):
            nxt.qa[:, p * LANES:(p + 1) * LANES] = _rope(
                qa[:, p * LANES:(p + 1) * LANES], shared_vals["tables"]).astype(BF16)

    def p_kva():
        kva = jnp.dot(nxt.h[...], wkva_ref[...], preferred_element_type=F32)
        ka = _rope(kva[:, 0:ATTN_KV_W], shared_vals["tables"])
        va = kva[:, ATTN_KV_W:2 * ATTN_KV_W]
        nxt.kc[...] = ka.astype(BF16)
        nxt.kcs[...] = pltpu.roll(ka, ATTN_HEAD_DIM, 1).astype(BF16)
        nxt.vc[...] = va.astype(BF16)
        nxt.vcs[...] = pltpu.roll(va, ATTN_HEAD_DIM, 1).astype(BF16)

    def gate_piece(idx):
        gref, wref = (sgm_scr, wgm_ref) if idx < 2 else (sga_scr, wga_ref)
        c0 = (idx % 2) * MIX_COL_CHUNK
        gref[:, c0:c0 + MIX_COL_CHUNK] = jax.nn.sigmoid(
            jnp.dot(cur.h[...], wref[:, c0:c0 + MIX_COL_CHUNK], preferred_element_type=F32))

    qi = lax.broadcasted_iota(jnp.int32, (CHUNK, CHUNK), 0)
    si = lax.broadcasted_iota(jnp.int32, (CHUNK, CHUNK), 1)
    tri = si <= qi
    lane_lo = lax.broadcasted_iota(jnp.int32, (CHUNK, PAIR_W), 1) < MLSTM_QK_DIM
    zeros_c = jnp.zeros((MLSTM_QK_DIM, VEXT_W), BF16)
    ong = ong_ref[...]
    tile_shape = (CHUNK, CHUNK)
    rep_shape = (MLSTM_HEADS, LANES)

    chunk_stats = []
    m_vec = m_scr[0:MLSTM_HEADS, :]
    for c in range(n_chunks):
        g_c = cur.g[:, c * CHUNK:(c + 1) * CHUNK]
        i_r = g_c[0:MLSTM_HEADS]
        logf_r = _log_sigmoid(g_c[MLSTM_HEADS:GATE_ROWS])
        b_r = _lane_cumsum(logf_r)
        b_last = _rep(b_r[:, CHUNK - 1:CHUNK], rep_shape)
        a_r = b_last - b_r + i_r
        a_max = _rep(jnp.max(a_r, axis=1, keepdims=True), rep_shape)
        m_new = jnp.maximum(b_last + m_vec, a_max)
        chunk_stats.append(dict(
            logf_r=logf_r, ib_r=i_r - b_r, m_prev=m_vec,
            decay=jnp.exp(b_last + m_vec - m_new), scale=jnp.exp(a_max - m_new),
            w_loc=jnp.exp(a_r - a_max)))
        m_vec = m_new
    c_state = [c_scr[hd] for hd in range(MLSTM_HEADS)]
    c_next = [None] * MLSTM_HEADS
    s_qk = {}

    def mlstm_scores(c):
        rs = slice(c * CHUNK, (c + 1) * CHUNK)
        for pair in range(MLSTM_HEADS // 2):
            q2 = cur.qk[rs, pair * PAIR_W:(pair + 1) * PAIR_W]
            k2 = cur.qk[rs, MLSTM_QK_W + pair * PAIR_W:MLSTM_QK_W + (pair + 1) * PAIR_W]
            zk = jnp.zeros_like(k2)
            k_both = jnp.concatenate([jnp.where(lane_lo, k2, zk), jnp.where(lane_lo, zk, k2)], axis=0)
            s2 = lax.dot_general(q2, k_both, _NT, preferred_element_type=F32)
            s_qk[c, 2 * pair] = s2[:, 0:CHUNK]
            s_qk[c, 2 * pair + 1] = s2[:, CHUNK:2 * CHUNK]

    def mlstm_state(c):
        rs = slice(c * CHUNK, (c + 1) * CHUNK)
        st = chunk_stats[c]
        for hd in range(MLSTM_HEADS):
            vext = cur.vext[rs, hd * VEXT_W:(hd + 1) * VEXT_W]
            ktw = (cur.kT[hd * MLSTM_QK_DIM:(hd + 1) * MLSTM_QK_DIM, rs] * st["w_loc"][hd:hd + 1]).astype(BF16)
            u = jnp.dot(ktw, vext, preferred_element_type=F32)
            decay = jnp.concatenate([st["decay"][hd:hd + 1]] * 2, axis=1)
            scale = jnp.concatenate([st["scale"][hd:hd + 1]] * 2, axis=1)
            c_next[hd] = decay * c_state[hd] + scale * u

    def mlstm_out(c):
        rs = slice(c * CHUNK, (c + 1) * CHUNK)
        st = chunk_stats[c]
        for hd in range(MLSTM_HEADS):
            pair, odd = divmod(hd, 2)
            q2 = cur.qk[rs, pair * PAIR_W:(pair + 1) * PAIR_W]
            vext = cur.vext[rs, hd * VEXT_W:(hd + 1) * VEXT_W]
            b_col = _rep(jnp.sum(jnp.where(tri, st["logf_r"][hd:hd + 1], 0.0), axis=1, keepdims=True),
                         tile_shape)
            log_d = jnp.where(tri, b_col + st["ib_r"][hd:hd + 1], -jnp.inf)
            inter_log = b_col + st["m_prev"][hd:hd + 1]
            m_j = jnp.maximum(inter_log, _rep(jnp.max(log_d, axis=1, keepdims=True), tile_shape))
            sqk = (s_qk.pop((c, hd)) * jnp.exp(log_d - m_j)).astype(BF16)
            c_bf = c_state[hd].astype(BF16)
            c_pad = jnp.concatenate([zeros_c, c_bf] if odd else [c_bf, zeros_c], axis=0)
            intra = jnp.dot(sqk, vext, preferred_element_type=F32)
            inter = jnp.dot(q2, c_pad, preferred_element_type=F32)
            inter_scale = jnp.exp(inter_log - m_j)
            num = intra[:, 0:MLSTM_V_DIM] + inter_scale * inter[:, 0:MLSTM_V_DIM]
            den = intra[:, MLSTM_V_DIM:VEXT_W] + inter_scale * inter[:, MLSTM_V_DIM:VEXT_W]
            hh = num / jnp.maximum(jnp.abs(den), jnp.exp(-m_j))
            vs = slice(hd * MLSTM_V_DIM, (hd + 1) * MLSTM_V_DIM)
            hn = hh * lax.rsqrt(jnp.mean(hh * hh, axis=1, keepdims=True) + NORM_EPS) * ong[:, vs]
            hm_scr[rs, vs] = (hn * cur.so[rs, vs]).astype(BF16)
            c_state[hd] = c_next[hd]

    qi2 = lax.broadcasted_iota(jnp.int32, (WINDOW, 2 * WINDOW), 0)
    si2 = lax.broadcasted_iota(jnp.int32, (WINDOW, 2 * WINDOW), 1)
    band = (si2 > qi2) & (si2 <= qi2 + WINDOW)
    band_first = band & ((si2 >= WINDOW) | (t_cur > 0))
    kv_lo = lax.broadcasted_iota(jnp.int32, (2 * WINDOW, LANES), 1) < ATTN_HEAD_DIM
    ones_lo = jnp.where(kv_lo, 1.0, 0.0).astype(BF16)
    ones_hi = jnp.where(kv_lo, 0.0, 1.0).astype(BF16)
    zkv = jnp.zeros((2 * WINDOW, LANES), BF16)
    attn = {}

    def window_rows(tile_ref, prev_ref, nb):
        if nb == 0:
            return jnp.concatenate([prev_ref[...], tile_ref[0:WINDOW, :]], axis=0)
        return tile_ref[(nb - 1) * WINDOW:(nb + 1) * WINDOW, :]

    def attn_scores(nb):
        qs = slice(nb * WINDOW, (nb + 1) * WINDOW)
        mask = band_first if nb == 0 else band
        k_own, k_sw = window_rows(cur.kc, pk_scr, nb), window_rows(cur.kcs, pks_scr, nb)
        for kh in range(ATTN_KV_HEADS):
            own, swapped = (k_own, k_sw) if kh == 0 else (k_sw, k_own)
            k_lo = jnp.where(kv_lo, own, zkv)
            k_hi = jnp.where(kv_lo, zkv, swapped)
            pair0 = kh * (group // 2)
            lhs = jnp.concatenate([cur.qa[qs, (pair0 + j) * LANES:(pair0 + j + 1) * LANES]
                                   for j in range(group // 2)], axis=0)
            probs, sink_terms = [], []
            for odd, k_sel in enumerate((k_lo, k_hi)):
                s_all = lax.dot_general(lhs, k_sel, _NT, preferred_element_type=F32)
                p_rows, sk_rows = [], []
                for j in range(group // 2):
                    sink = sink_ref[kh * group + 2 * j + odd]
                    sc = jnp.where(mask, s_all[j * WINDOW:(j + 1) * WINDOW], -jnp.inf)
                    mx = jnp.maximum(jnp.max(sc, axis=1, keepdims=True), sink)
                    p_rows.append(jnp.exp(sc - mx).astype(BF16))
                    sk_rows.append(_rep(jnp.exp(sink - mx), (WINDOW, LANES)))
                probs.append(jnp.concatenate(p_rows, axis=0))
                sink_terms.append(jnp.concatenate(sk_rows, axis=0))
            attn[nb, kh] = (jnp.concatenate(probs, axis=1), jnp.where(kv_lo, sink_terms[0], sink_terms[1]))

    def attn_values(nb):
        qs = slice(nb * WINDOW, (nb + 1) * WINDOW)
        v_own, v_sw = window_rows(cur.vc, pv_scr, nb), window_rows(cur.vcs, pvs_scr, nb)
        for kh in range(ATTN_KV_HEADS):
            own, swapped = (v_own, v_sw) if kh == 0 else (v_sw, v_own)
            v_lo = jnp.where(kv_lo, own, zkv)
            v_hi = jnp.where(kv_lo, zkv, swapped)
            probs, sink_term = attn.pop((nb, kh))
            rhs = jnp.concatenate([jnp.concatenate([v_lo, ones_lo], axis=1),
                                   jnp.concatenate([v_hi, ones_hi], axis=1)], axis=0)
            o = jnp.dot(probs, rhs, preferred_element_type=F32)
            on = (o[:, 0:LANES] / (o[:, LANES:2 * LANES] + sink_term)).astype(BF16)
            pair0 = kh * (group // 2)
            for j in range(group // 2):
                oa_scr[qs, (pair0 + j) * LANES:(pair0 + j + 1) * LANES] = on[j * WINDOW:(j + 1) * WINDOW]

    def branch_attn(idx):
        c0 = idx * MIX_COL_CHUNK
        y_a = jnp.dot(oa_scr[...], wbra_ref[:, c0:c0 + MIX_COL_CHUNK], preferred_element_type=F32)
        sga_scr[:, c0:c0 + MIX_COL_CHUNK] = sga_scr[:, c0:c0 + MIX_COL_CHUNK] * y_a

    gate_piece(0)
    p_norm()
    side_cast()
    attn_scores(0); mlstm_scores(0); mlstm_state(0)
    gate_piece(1)
    attn_values(0); mlstm_out(0)
    p_qk()
    attn_scores(1); mlstm_scores(1); mlstm_state(1)
    p_v()
    attn_values(1); mlstm_out(1)
    p_so()
    attn_scores(2); mlstm_scores(2); mlstm_state(2)
    gate_piece(2)
    attn_values(2); mlstm_out(2)
    p_qa()
    attn_scores(3); mlstm_scores(3); mlstm_state(3)
    gate_piece(3)
    attn_values(3)
    p_kva()
    mlstm_out(3)
    branch_attn(0); branch_attn(1)

    for hd in range(MLSTM_HEADS):
        c_scr[hd] = c_state[hd]
    m_scr[0:MLSTM_HEADS, :] = m_vec
    for prev_ref, tile_ref in ((pk_scr, cur.kc), (pks_scr, cur.kcs), (pv_scr, cur.vc), (pvs_scr, cur.vcs)):
        prev_ref[...] = tile_ref[rows - WINDOW:rows, :]

    for c0, c1 in _col_chunks(D_MODEL, MIX_COL_CHUNK):
        y_m = jnp.dot(hm_scr[...], wbrm_ref[:, c0:c1], preferred_element_type=F32)
        mg_scr[:, c0:c1] = (sgm_scr[:, c0:c1] * y_m + sga_scr[:, c0:c1]).astype(BF16)
    o_ref[0] = cur.xs[...] + jnp.dot(mg_scr[...], wout_ref[...], preferred_element_type=F32)


def _mixer_body(*refs, tiles_per_seq, n_cast):
    n_in = N_MIXER_IN + n_cast
    n_out = 1 + n_cast
    n_slot = len(Slot._fields)
    ins = refs[:n_in]
    outs = refs[n_in:n_in + n_out]
    scr = refs[n_in + n_out:]
    slot_a, slot_b = Slot(*scr[:n_slot]), Slot(*scr[n_slot:2 * n_slot])
    shared = scr[2 * n_slot:]
    pk_scr, pks_scr, pv_scr, pvs_scr = shared[0:4]
    c_scr, m_scr = shared[-2:]
    s = pl.program_id(0)
    t_cur = lax.rem(s + tiles_per_seq - 1, tiles_per_seq)

    @pl.when(s == 0)
    def _():
        for ref in list(slot_b) + list(shared):
            ref[...] = jnp.zeros_like(ref)

    @pl.when(t_cur == 0)
    def _():
        c_scr[...] = jnp.zeros_like(c_scr)
        m_scr[...] = jnp.zeros_like(m_scr)
        for ref in (pk_scr, pks_scr, pv_scr, pvs_scr):
            ref[...] = jnp.zeros_like(ref)

    even = lax.rem(s, 2) == 0

    @pl.when(even)
    def _():
        _mixer_step(t_cur, ins, slot_b, slot_a, shared, outs)

    @pl.when(jnp.logical_not(even))
    def _():
        _mixer_step(t_cur, ins, slot_a, slot_b, shared, outs)


def _cast_slabs(arr, steps):
    rows = arr.shape[0]
    slab = 16
    while rows % slab or rows // slab > steps:
        slab += 16
    n_slabs = rows // slab
    return pl.BlockSpec((slab, arr.shape[1]), lambda i: (jnp.minimum(i, n_slabs - 1), 0))


def _mixer(x, positions, norm_g, b_i, b_f, out_norm_g, sinks, weights, cast_weights):
    b, s, d = x.shape
    w_qk, w_v, w_o, w_qa, w_kva, w_gm, w_ga, w_brm, w_bra, w_outp = weights
    b_if = jnp.concatenate([b_i, b_f]).reshape(GATE_ROWS, 1)
    half = ROPE_DIM // 2
    inv = (ROPE_THETA ** (-jnp.arange(half, dtype=F32) * 2.0 / ROPE_DIM)).reshape(half, 1)

    rows = MIX_ROWS
    tiles_per_seq = s // rows
    n_tiles = b * tiles_per_seq
    def in_tile(i):
        j = jnp.minimum(i, n_tiles - 1)
        return j // tiles_per_seq, j % tiles_per_seq

    def out_tile(i):
        j = jnp.maximum(i - 1, 0)
        return j // tiles_per_seq, j % tiles_per_seq

    in_specs = [
        pl.BlockSpec((1, rows, d), lambda i: (*in_tile(i), 0)),
        pl.BlockSpec((1, 1, rows), lambda i: (in_tile(i)[0], 0, in_tile(i)[1])),
        _const_spec((1, d)), _const_spec((half, 1)), _const_spec((GATE_ROWS, 1)),
        _const_spec((1, MLSTM_V_W)),
        pl.BlockSpec(memory_space=pltpu.SMEM),
        _const_spec((d, QKG_W)), _const_spec((d, MLSTM_V_W)),
        _const_spec((d, MLSTM_V_W)), _const_spec((d, ATTN_Q_W)),
        _const_spec((d, 2 * ATTN_KV_W)), _const_spec((d, d)), _const_spec((d, d)),
        _const_spec((MLSTM_V_W, d)), _const_spec((ATTN_Q_W, d)), _const_spec((d, d)),
    ]
    slot = list(_slot_shapes(rows, d))
    shared = [
        pltpu.VMEM((WINDOW, LANES), BF16),
        pltpu.VMEM((WINDOW, LANES), BF16),
        pltpu.VMEM((WINDOW, LANES), BF16),
        pltpu.VMEM((WINDOW, LANES), BF16),
        pltpu.VMEM((rows, d), F32),
        pltpu.VMEM((rows, d), F32),
        pltpu.VMEM((rows, MLSTM_V_W), BF16),
        pltpu.VMEM((rows, ATTN_Q_W), BF16),
        pltpu.VMEM((rows, d), BF16),
        pltpu.VMEM((MLSTM_HEADS, MLSTM_QK_DIM, VEXT_W), F32),
        pltpu.VMEM((8, LANES), F32),
    ]
    assert len(in_specs) == N_MIXER_IN
    steps = n_tiles + 1
    cast_specs = [_cast_slabs(arr, steps) for arr in cast_weights]
    res = pl.pallas_call(
        functools.partial(_mixer_body, tiles_per_seq=tiles_per_seq, n_cast=len(cast_weights)),
        grid=(steps,),
        in_specs=in_specs + cast_specs,
        out_specs=[pl.BlockSpec((1, rows, d), lambda i: (*out_tile(i), 0))] + cast_specs,
        out_shape=[jax.ShapeDtypeStruct((b, s, d), F32)]
        + [jax.ShapeDtypeStruct(arr.shape, BF16) for arr in cast_weights],
        scratch_shapes=slot + slot + shared,
        compiler_params=pltpu.CompilerParams(
            dimension_semantics=("arbitrary",), vmem_limit_bytes=(7 * VMEM_BYTES_V7X) // 8),
        name="mixer",
    )(x, positions.reshape(b, 1, s), norm_g.reshape(1, d), inv, b_if, out_norm_g.reshape(1, MLSTM_V_W),
      sinks, w_qk, w_v, w_o, w_qa, w_kva, w_gm, w_ga, w_brm, w_bra, w_outp, *cast_weights)
    return res[0], res[1:]


def kernel(x, positions, ffn1_norm_g, ffn1_w_gate, ffn1_w_up, ffn1_w_down, mix_norm_g, w_in, mlstm_b_i, mlstm_b_f, mlstm_out_norm_g, attn_sinks, w_branch_mlstm, w_branch_attn, w_out, ffn2_norm_g, ffn2_w_gate, ffn2_w_up, ffn2_w_down, final_norm_g):
    b, s, d = x.shape
    depth = w_in.shape[0]
    for l in range(depth):
        last = l == depth - 1
        x2, mixer_w = _ffn(x.reshape(b * s, d), ffn1_norm_g[l], ffn1_w_gate[l].astype(BF16),
                           ffn1_w_up[l].astype(BF16), ffn1_w_down[l].astype(BF16),
                           mixer_weights=(w_in[l], w_branch_mlstm[l], w_branch_attn[l], w_out[l]))
        x3, ffn2_w = _mixer(x2.reshape(b, s, d), positions, mix_norm_g[l], mlstm_b_i[l], mlstm_b_f[l],
                            mlstm_out_norm_g[l], attn_sinks[l], mixer_w,
                            (ffn2_w_gate[l], ffn2_w_up[l], ffn2_w_down[l]))
        x = _ffn(x3.reshape(b * s, d), ffn2_norm_g[l], *ffn2_w,
                 final_g=final_norm_g if last else None).reshape(b, s, d)
    return x
```

```python
import collections
import functools

import jax
import jax.numpy as jnp
from jax import lax
from jax.experimental import pallas as pl
from jax.experimental.pallas import tpu as pltpu

D_MODEL = 1024
D_FF = 2816
MLSTM_HEADS = 4
MLSTM_QK_DIM = 64
MLSTM_V_DIM = 128
GATE_SOFTCAP = 15.0
ATTN_Q_HEADS = 8
ATTN_KV_HEADS = 2
ATTN_HEAD_DIM = 64
WINDOW = 128
ROPE_DIM = ATTN_HEAD_DIM // 4
ROPE_THETA = 500000.0
NORM_EPS = 1e-6

MLSTM_QK_W = MLSTM_HEADS * MLSTM_QK_DIM
MLSTM_V_W = MLSTM_HEADS * MLSTM_V_DIM
ATTN_Q_W = ATTN_Q_HEADS * ATTN_HEAD_DIM
ATTN_KV_W = ATTN_KV_HEADS * ATTN_HEAD_DIM
IN_WIDTHS = (MLSTM_QK_W, MLSTM_QK_W, MLSTM_V_W, MLSTM_V_W, MLSTM_HEADS, MLSTM_HEADS,
             ATTN_Q_W, ATTN_KV_W, ATTN_KV_W, D_MODEL, D_MODEL)

LANES = 128
MXU_COLS = 256
VMEM_BYTES_V7X = 64 * 1024 * 1024

FFN_ROWS = 1024
FFN_COL_CHUNK = 2 * MXU_COLS
MIX_ROWS = 512
MIX_COL_CHUNK = 2 * MXU_COLS
CHUNK = LANES
PAIR_W = 2 * MLSTM_QK_DIM
VEXT_W = 2 * MLSTM_V_DIM
GATE_ROWS = 2 * MLSTM_HEADS
QKG_W = 2 * MLSTM_QK_W + LANES
N_MIXER_IN = 17

F32 = jnp.float32
BF16 = jnp.bfloat16


def _rms(x, g):
    return x * lax.rsqrt(jnp.mean(x * x, axis=-1, keepdims=True) + NORM_EPS) * g


def _col_chunks(total, width):
    return [(c, min(c + width, total)) for c in range(0, total, width)]


def _const_spec(shape):
    return pl.BlockSpec(shape, lambda *_: (0,) * len(shape), pipeline_mode=pl.Buffered(1))


def _prep_mixer_weights(win_ref, wbrm_ref, wbra_ref, wout_ref, outs):
    wqk_o, wv_o, wo_o, wqa_o, wkva_o, wgm_o, wga_o, wbrm_o, wbra_o, wout_o = outs
    starts = [0]
    for w in IN_WIDTHS:
        starts.append(starts[-1] + w)
    col = lambda i: win_ref[:, starts[i]:starts[i + 1]]
    gate_pad = jnp.zeros((win_ref.shape[0], LANES - GATE_ROWS), F32)
    wqk_o[...] = jnp.concatenate([col(0) * (MLSTM_QK_DIM ** -0.5), col(1), col(4), col(5), gate_pad],
                                 axis=1).astype(BF16)
    wv_o[...] = col(2).astype(BF16)
    wo_o[...] = col(3).astype(BF16)
    wqa_o[...] = (col(6) * (ATTN_HEAD_DIM ** -0.5)).astype(BF16)
    wkva_o[...] = jnp.concatenate([col(7), col(8)], axis=1).astype(BF16)
    wgm_o[...] = col(9).astype(BF16)
    wga_o[...] = col(10).astype(BF16)
    wbrm_o[...] = wbrm_ref[...].astype(BF16)
    wbra_o[...] = wbra_ref[...].astype(BF16)
    wout_o[...] = wout_ref[...].astype(BF16)


def _ffn_body(*refs, final_norm, prep):
    n_in = 5 + int(final_norm) + (4 if prep else 0)
    x_ref, g_ref, wg_ref, wu_ref, wd_ref = refs[:5]
    o_ref = refs[n_in]
    act_ref = refs[-1]
    x = x_ref[...]
    h = _rms(x, g_ref[...]).astype(BF16)
    for c0, c1 in _col_chunks(D_FF, FFN_COL_CHUNK):
        g = jnp.dot(h, wg_ref[:, c0:c1], preferred_element_type=F32)
        u = jnp.dot(h, wu_ref[:, c0:c1], preferred_element_type=F32)
        act_ref[:, c0:c1] = (g * jax.nn.sigmoid(g) * u).astype(BF16)
    if prep:
        _prep_mixer_weights(*refs[n_in - 4:n_in], refs[n_in + 1:-1])
    r = x + 0.5 * jnp.dot(act_ref[...], wd_ref[...], preferred_element_type=F32)
    if final_norm:
        r = _rms(r, refs[5][...])
    o_ref[...] = r


def _ffn(x2, norm_g, w_gate, w_up, w_down, final_g=None, mixer_weights=None):
    n, d = x2.shape
    steps = n // FFN_ROWS
    final_norm = final_g is not None
    prep = mixer_weights is not None
    row_spec = pl.BlockSpec((FFN_ROWS, d), lambda i: (i, 0))
    in_specs = [row_spec, _const_spec((1, d)), _const_spec((d, D_FF)), _const_spec((d, D_FF)),
                _const_spec((D_FF, d))]
    args = [x2, norm_g.reshape(1, d), w_gate, w_up, w_down]
    out_specs = [row_spec]
    out_shape = [jax.ShapeDtypeStruct((n, d), F32)]
    if final_norm:
        in_specs.append(_const_spec((1, d)))
        args.append(final_g.reshape(1, d))
    if prep:
        slab = lambda arr: pl.BlockSpec((arr.shape[0] // steps, arr.shape[1]), lambda i: (i, 0))
        for arr in mixer_weights:
            in_specs.append(slab(arr))
            args.append(arr)
        piece_widths = (QKG_W, MLSTM_V_W, MLSTM_V_W, ATTN_Q_W, 2 * ATTN_KV_W, d, d)
        pieces = [jax.ShapeDtypeStruct((d, w), BF16) for w in piece_widths]
        pieces += [jax.ShapeDtypeStruct(arr.shape, BF16) for arr in mixer_weights[1:]]
        out_shape += pieces
        out_specs += [slab(p) for p in pieces]
    res = pl.pallas_call(
        functools.partial(_ffn_body, final_norm=final_norm, prep=prep),
        grid=(steps,),
        in_specs=in_specs,
        out_specs=out_specs,
        out_shape=out_shape,
        scratch_shapes=[pltpu.VMEM((FFN_ROWS, D_FF), BF16)],
        compiler_params=pltpu.CompilerParams(
            dimension_semantics=("arbitrary",), vmem_limit_bytes=(3 * VMEM_BYTES_V7X) // 4),
        name="ffn_final" if final_norm else "ffn",
    )(*args)
    return (res[0], res[1:]) if prep else res[0]


Slot = collections.namedtuple("Slot", "xs h q kT vext so qa kcT vc vcs g")


def _slot_shapes(rows, d):
    return Slot(
        xs=pltpu.VMEM((rows, d), F32),
        h=pltpu.VMEM((rows, d), BF16),
        q=pltpu.VMEM((rows, MLSTM_QK_W), BF16),
        kT=pltpu.VMEM((MLSTM_QK_W, rows), F32),
        vext=pltpu.VMEM((rows, MLSTM_HEADS * VEXT_W), BF16),
        so=pltpu.VMEM((rows, MLSTM_V_W), F32),
        qa=pltpu.VMEM((rows, ATTN_Q_W), BF16),
        kcT=pltpu.VMEM((LANES, rows), BF16),
        vc=pltpu.VMEM((rows, LANES), BF16),
        vcs=pltpu.VMEM((rows, LANES), BF16),
        g=pltpu.VMEM((GATE_ROWS, rows), F32),
    )


def _softcap(a):
    return GATE_SOFTCAP * jnp.tanh(a / GATE_SOFTCAP)


def _log_sigmoid(a):
    return jnp.minimum(a, 0.0) - jnp.log1p(jnp.exp(-jnp.abs(a)))


def _lane_cumsum(a):
    lane = lax.broadcasted_iota(jnp.int32, a.shape, 1)
    d = 1
    while d < LANES:
        a = a + jnp.where(lane >= d, pltpu.roll(a, d, 1), 0.0)
        d *= 2
    return a


def _rope_tables(pos_row, inv_col):
    ang = inv_col * pos_row
    c, s = jnp.cos(ang), jnp.sin(ang)
    one, zero = jnp.ones_like(c), jnp.zeros_like(c)
    per_head = ATTN_HEAD_DIM // 8

    def tile(first, second, rest):
        rows = []
        for _ in range(LANES // ATTN_HEAD_DIM):
            rows += [first, second] + [rest] * (per_head - 2)
        return jnp.concatenate(rows, axis=0).T
    return tile(c, c, one), tile(zero, s, zero), tile(-s, zero, zero)


def _rope(a, tables):
    cos_t, sin_up, sin_dn = tables
    half = ROPE_DIM // 2
    return a * cos_t + pltpu.roll(a, half, 1) * sin_up + pltpu.roll(a, LANES - half, 1) * sin_dn


def _rep(a, shape):
    return jnp.broadcast_to(a, shape)


def _mixer_step(t_cur, ins, cur, nxt, shared, outs):
    (x_ref, pos_ref, ng_ref, inv_ref, bif_ref, ong_ref, sink_ref,
     wqk_ref, wv_ref, wo_ref, wqa_ref, wkva_ref, wgm_ref, wga_ref,
     wbrm_ref, wbra_ref, wout_ref) = ins[:N_MIXER_IN]
    cast_in = ins[N_MIXER_IN:]
    o_ref, cast_out = outs[0], outs[1:]
    pkT_scr, pv_scr, pvs_scr, sgm_scr, sga_scr, hm_scr, oa_scr, mg_scr, c_scr, m_scr = shared
    rows = MIX_ROWS
    n_chunks = rows // CHUNK
    n_blocks = rows // WINDOW
    group = ATTN_Q_HEADS // ATTN_KV_HEADS

    shared_vals = {}

    def p_norm():
        x_new = x_ref[0]
        nxt.xs[...] = x_new
        nxt.h[...] = _rms(x_new, ng_ref[...]).astype(BF16)

    def side_cast():
        for src, dst in zip(cast_in, cast_out):
            dst[...] = src[...].astype(BF16)

    def p_qk():
        qkg = jnp.dot(nxt.h[...], wqk_ref[...], preferred_element_type=F32)
        nxt.q[...] = qkg[:, 0:MLSTM_QK_W].astype(BF16)
        nxt.kT[...] = qkg[:, MLSTM_QK_W:2 * MLSTM_QK_W].T
        gates_t = qkg[:, 2 * MLSTM_QK_W:QKG_W].T
        nxt.g[...] = _softcap(gates_t[0:GATE_ROWS] + bif_ref[...])

    def p_v():
        v = jnp.dot(nxt.h[...], wv_ref[...], preferred_element_type=F32)
        for hd in range(MLSTM_HEADS):
            nxt.vext[:, hd * VEXT_W:hd * VEXT_W + MLSTM_V_DIM] = (
                v[:, hd * MLSTM_V_DIM:(hd + 1) * MLSTM_V_DIM].astype(BF16))
            nxt.vext[:, hd * VEXT_W + MLSTM_V_DIM:(hd + 1) * VEXT_W] = jnp.ones((rows, MLSTM_V_DIM), BF16)

    def p_so():
        nxt.so[...] = jax.nn.sigmoid(jnp.dot(nxt.h[...], wo_ref[...], preferred_element_type=F32))

    def p_qa():
        shared_vals["tables"] = _rope_tables(pos_ref[0].astype(F32), inv_ref[...])
        qa = jnp.dot(nxt.h[...], wqa_ref[...], preferred_element_type=F32)
        for p in range(ATTN_Q_W // LANES):
            nxt.qa[:, p * LANES:(p + 1) * LANES] = _rope(
                qa[:, p * LANES:(p + 1) * LANES], shared_vals["tables"]).astype(BF16)

    def p_kva():
        kva = jnp.dot(nxt.h[...], wkva_ref[...], preferred_element_type=F32)
        ka = _rope(kva[:, 0:ATTN_KV_W], shared_vals["tables"])
        va = kva[:, ATTN_KV_W:2 * ATTN_KV_W]
        nxt.kcT[...] = ka.T.astype(BF16)
        nxt.vc[...] = va.astype(BF16)
        nxt.vcs[...] = pltpu.roll(va, ATTN_HEAD_DIM, 1).astype(BF16)

    def gate_piece(idx):
        gref, wref = (sgm_scr, wgm_ref) if idx < 2 else (sga_scr, wga_ref)
        c0 = (idx % 2) * MIX_COL_CHUNK
        gref[:, c0:c0 + MIX_COL_CHUNK] = jax.nn.sigmoid(
            jnp.dot(cur.h[...], wref[:, c0:c0 + MIX_COL_CHUNK], preferred_element_type=F32))

    qi = lax.broadcasted_iota(jnp.int32, (CHUNK, CHUNK), 0)
    si = lax.broadcasted_iota(jnp.int32, (CHUNK, CHUNK), 1)
    tri = si <= qi
    zeros_c = jnp.zeros((MLSTM_QK_DIM, VEXT_W), BF16)
    ong = ong_ref[...]
    tile_shape = (CHUNK, CHUNK)
    rep_shape = (MLSTM_HEADS, LANES)

    chunk_stats = []
    m_vec = m_scr[0:MLSTM_HEADS, :]
    for c in range(n_chunks):
        g_c = cur.g[:, c * CHUNK:(c + 1) * CHUNK]
        i_r = g_c[0:MLSTM_HEADS]
        logf_r = _log_sigmoid(g_c[MLSTM_HEADS:GATE_ROWS])
        b_r = _lane_cumsum(logf_r)
        b_last = _rep(b_r[:, CHUNK - 1:CHUNK], rep_shape)
        a_r = b_last - b_r + i_r
        a_max = _rep(jnp.max(a_r, axis=1, keepdims=True), rep_shape)
        m_new = jnp.maximum(b_last + m_vec, a_max)
        chunk_stats.append(dict(
            logf_r=logf_r, ib_r=i_r - b_r, m_prev=m_vec,
            decay=jnp.exp(b_last + m_vec - m_new), scale=jnp.exp(a_max - m_new),
            w_loc=jnp.exp(a_r - a_max)))
        m_vec = m_new
    c_state = [c_scr[hd] for hd in range(MLSTM_HEADS)]
    c_next = [None] * MLSTM_HEADS
    s_qk = {}

    def mlstm_scores(c):
        rs = slice(c * CHUNK, (c + 1) * CHUNK)
        for pair in range(MLSTM_HEADS // 2):
            q2 = cur.q[rs, pair * PAIR_W:(pair + 1) * PAIR_W]
            k_even, k_odd = (cur.kT[hd * MLSTM_QK_DIM:(hd + 1) * MLSTM_QK_DIM, rs].astype(BF16)
                             for hd in (2 * pair, 2 * pair + 1))
            zk = jnp.zeros_like(k_even)
            k_both = jnp.concatenate([jnp.concatenate([k_even, zk], axis=1),
                                      jnp.concatenate([zk, k_odd], axis=1)], axis=0)
            s2 = jnp.dot(q2, k_both, preferred_element_type=F32)
            s_qk[c, 2 * pair] = s2[:, 0:CHUNK]
            s_qk[c, 2 * pair + 1] = s2[:, CHUNK:2 * CHUNK]

    def mlstm_state(c):
        rs = slice(c * CHUNK, (c + 1) * CHUNK)
        st = chunk_stats[c]
        for hd in range(MLSTM_HEADS):
            vext = cur.vext[rs, hd * VEXT_W:(hd + 1) * VEXT_W]
            ktw = (cur.kT[hd * MLSTM_QK_DIM:(hd + 1) * MLSTM_QK_DIM, rs] * st["w_loc"][hd:hd + 1]).astype(BF16)
            u = jnp.dot(ktw, vext, preferred_element_type=F32)
            decay = jnp.concatenate([st["decay"][hd:hd + 1]] * 2, axis=1)
            scale = jnp.concatenate([st["scale"][hd:hd + 1]] * 2, axis=1)
            c_next[hd] = decay * c_state[hd] + scale * u

    def mlstm_out(c):
        rs = slice(c * CHUNK, (c + 1) * CHUNK)
        st = chunk_stats[c]
        for hd in range(MLSTM_HEADS):
            pair, odd = divmod(hd, 2)
            q2 = cur.q[rs, pair * PAIR_W:(pair + 1) * PAIR_W]
            vext = cur.vext[rs, hd * VEXT_W:(hd + 1) * VEXT_W]
            b_col = _rep(jnp.sum(jnp.where(tri, st["logf_r"][hd:hd + 1], 0.0), axis=1, keepdims=True),
                         tile_shape)
            log_d = jnp.where(tri, b_col + st["ib_r"][hd:hd + 1], -jnp.inf)
            inter_log = b_col + st["m_prev"][hd:hd + 1]
            m_j = jnp.maximum(inter_log, _rep(jnp.max(log_d, axis=1, keepdims=True), tile_shape))
            sqk = (s_qk.pop((c, hd)) * jnp.exp(log_d - m_j)).astype(BF16)
            c_bf = c_state[hd].astype(BF16)
            c_pad = jnp.concatenate([zeros_c, c_bf] if odd else [c_bf, zeros_c], axis=0)
            intra = jnp.dot(sqk, vext, preferred_element_type=F32)
            inter = jnp.dot(q2, c_pad, preferred_element_type=F32)
            inter_scale = jnp.exp(inter_log - m_j)
            num = intra[:, 0:MLSTM_V_DIM] + inter_scale * inter[:, 0:MLSTM_V_DIM]
            den = intra[:, MLSTM_V_DIM:VEXT_W] + inter_scale * inter[:, MLSTM_V_DIM:VEXT_W]
            hh = num / jnp.maximum(jnp.abs(den), jnp.exp(-m_j))
            vs = slice(hd * MLSTM_V_DIM, (hd + 1) * MLSTM_V_DIM)
            hn = hh * lax.rsqrt(jnp.mean(hh * hh, axis=1, keepdims=True) + NORM_EPS) * ong[:, vs]
            hm_scr[rs, vs] = (hn * cur.so[rs, vs]).astype(BF16)
            c_state[hd] = c_next[hd]

    qi2 = lax.broadcasted_iota(jnp.int32, (WINDOW, 2 * WINDOW), 0)
    si2 = lax.broadcasted_iota(jnp.int32, (WINDOW, 2 * WINDOW), 1)
    band = (si2 > qi2) & (si2 <= qi2 + WINDOW)
    band_first = band & ((si2 >= WINDOW) | (t_cur > 0))
    kv_lo = lax.broadcasted_iota(jnp.int32, (2 * WINDOW, LANES), 1) < ATTN_HEAD_DIM
    ones_lo = jnp.where(kv_lo, 1.0, 0.0).astype(BF16)
    ones_hi = jnp.where(kv_lo, 0.0, 1.0).astype(BF16)
    zkv = jnp.zeros((2 * WINDOW, LANES), BF16)
    attn = {}

    def window_rows(tile_ref, prev_ref, nb):
        if nb == 0:
            return jnp.concatenate([prev_ref[...], tile_ref[0:WINDOW, :]], axis=0)
        return tile_ref[(nb - 1) * WINDOW:(nb + 1) * WINDOW, :]

    def attn_scores(nb):
        qs = slice(nb * WINDOW, (nb + 1) * WINDOW)
        mask = band_first if nb == 0 else band
        if nb == 0:
            k_win = jnp.concatenate([pkT_scr[...], cur.kcT[:, 0:WINDOW]], axis=1)
        else:
            k_win = cur.kcT[:, (nb - 1) * WINDOW:(nb + 1) * WINDOW]
        zk = jnp.zeros((ATTN_HEAD_DIM, 2 * WINDOW), BF16)
        for kh in range(ATTN_KV_HEADS):
            k_h = k_win[kh * ATTN_HEAD_DIM:(kh + 1) * ATTN_HEAD_DIM]
            k_for_even = jnp.concatenate([k_h, zk], axis=0)
            k_for_odd = jnp.concatenate([zk, k_h], axis=0)
            pair0 = kh * (group // 2)
            lhs = jnp.concatenate([cur.qa[qs, (pair0 + j) * LANES:(pair0 + j + 1) * LANES]
                                   for j in range(group // 2)], axis=0)
            probs, sink_terms = [], []
            for odd, k_sel in enumerate((k_for_even, k_for_odd)):
                s_all = jnp.dot(lhs, k_sel, preferred_element_type=F32)
                p_rows, sk_rows = [], []
                for j in range(group // 2):
                    sink = sink_ref[kh * group + 2 * j + odd]
                    sc = jnp.where(mask, s_all[j * WINDOW:(j + 1) * WINDOW], -jnp.inf)
                    mx = jnp.maximum(jnp.max(sc, axis=1, keepdims=True), sink)
                    p_rows.append(jnp.exp(sc - mx).astype(BF16))
                    sk_rows.append(_rep(jnp.exp(sink - mx), (WINDOW, LANES)))
                probs.append(jnp.concatenate(p_rows, axis=0))
                sink_terms.append(jnp.concatenate(sk_rows, axis=0))
            attn[nb, kh] = (jnp.concatenate(probs, axis=1), jnp.where(kv_lo, sink_terms[0], sink_terms[1]))

    def attn_values(nb):
        qs = slice(nb * WINDOW, (nb + 1) * WINDOW)
        v_own, v_sw = window_rows(cur.vc, pv_scr, nb), window_rows(cur.vcs, pvs_scr, nb)
        for kh in range(ATTN_KV_HEADS):
            own, swapped = (v_own, v_sw) if kh == 0 else (v_sw, v_own)
            v_lo = jnp.where(kv_lo, own, zkv)
            v_hi = jnp.where(kv_lo, zkv, swapped)
            probs, sink_term = attn.pop((nb, kh))
            rhs = jnp.concatenate([jnp.concatenate([v_lo, ones_lo], axis=1),
                                   jnp.concatenate([v_hi, ones_hi], axis=1)], axis=0)
            o = jnp.dot(probs, rhs, preferred_element_type=F32)
            on = (o[:, 0:LANES] / (o[:, LANES:2 * LANES] + sink_term)).astype(BF16)
            pair0 = kh * (group // 2)
            for j in range(group // 2):
                oa_scr[qs, (pair0 + j) * LANES:(pair0 + j + 1) * LANES] = on[j * WINDOW:(j + 1) * WINDOW]

    def branch_attn(idx):
        c0 = idx * MIX_COL_CHUNK
        y_a = jnp.dot(oa_scr[...], wbra_ref[:, c0:c0 + MIX_COL_CHUNK], preferred_element_type=F32)
        sga_scr[:, c0:c0 + MIX_COL_CHUNK] = sga_scr[:, c0:c0 + MIX_COL_CHUNK] * y_a

    gate_piece(0)
    p_norm()
    side_cast()
    attn_scores(0); mlstm_scores(0); mlstm_state(0)
    gate_piece(1)
    attn_values(0); mlstm_out(0)
    p_qk()
    attn_scores(1); mlstm_scores(1); mlstm_state(1)
    p_v()
    attn_values(1); mlstm_out(1)
    p_so()
    attn_scores(2); mlstm_scores(2); mlstm_state(2)
    gate_piece(2)
    attn_values(2); mlstm_out(2)
    p_qa()
    attn_scores(3); mlstm_scores(3); mlstm_state(3)
    gate_piece(3)
    attn_values(3)
    p_kva()
    mlstm_out(3)
    branch_attn(0); branch_attn(1)

    for hd in range(MLSTM_HEADS):
        c_scr[hd] = c_state[hd]
    m_scr[0:MLSTM_HEADS, :] = m_vec
    pkT_scr[...] = cur.kcT[:, rows - WINDOW:rows]
    for prev_ref, tile_ref in ((pv_scr, cur.vc), (pvs_scr, cur.vcs)):
        prev_ref[...] = tile_ref[rows - WINDOW:rows, :]

    for c0, c1 in _col_chunks(D_MODEL, MIX_COL_CHUNK):
        y_m = jnp.dot(hm_scr[...], wbrm_ref[:, c0:c1], preferred_element_type=F32)
        mg_scr[:, c0:c1] = (sgm_scr[:, c0:c1] * y_m + sga_scr[:, c0:c1]).astype(BF16)
    o_ref[0] = cur.xs[...] + jnp.dot(mg_scr[...], wout_ref[...], preferred_element_type=F32)


def _mixer_body(*refs, tiles_per_seq, n_cast):
    n_in = N_MIXER_IN + n_cast
    n_out = 1 + n_cast
    n_slot = len(Slot._fields)
    ins = refs[:n_in]
    outs = refs[n_in:n_in + n_out]
    scr = refs[n_in + n_out:]
    slot_a, slot_b = Slot(*scr[:n_slot]), Slot(*scr[n_slot:2 * n_slot])
    shared = scr[2 * n_slot:]
    prev_blocks = shared[0:3]
    c_scr, m_scr = shared[-2:]
    s = pl.program_id(0)
    t_cur = lax.rem(s + tiles_per_seq - 1, tiles_per_seq)

    @pl.when(s == 0)
    def _():
        for ref in list(slot_b) + list(shared):
            ref[...] = jnp.zeros_like(ref)

    @pl.when(t_cur == 0)
    def _():
        c_scr[...] = jnp.zeros_like(c_scr)
        m_scr[...] = jnp.zeros_like(m_scr)
        for ref in prev_blocks:
            ref[...] = jnp.zeros_like(ref)

    even = lax.rem(s, 2) == 0

    @pl.when(even)
    def _():
        _mixer_step(t_cur, ins, slot_b, slot_a, shared, outs)

    @pl.when(jnp.logical_not(even))
    def _():
        _mixer_step(t_cur, ins, slot_a, slot_b, shared, outs)


def _cast_slabs(arr, steps):
    rows = arr.shape[0]
    slab = 16
    while rows % slab or rows // slab > steps:
        slab += 16
    n_slabs = rows // slab
    return pl.BlockSpec((slab, arr.shape[1]), lambda i: (jnp.minimum(i, n_slabs - 1), 0))


def _mixer(x, positions, norm_g, b_i, b_f, out_norm_g, sinks, weights, cast_weights):
    b, s, d = x.shape
    w_qk, w_v, w_o, w_qa, w_kva, w_gm, w_ga, w_brm, w_bra, w_outp = weights
    b_if = jnp.concatenate([b_i, b_f]).reshape(GATE_ROWS, 1)
    half = ROPE_DIM // 2
    inv = (ROPE_THETA ** (-jnp.arange(half, dtype=F32) * 2.0 / ROPE_DIM)).reshape(half, 1)

    rows = MIX_ROWS
    tiles_per_seq = s // rows
    n_tiles = b * tiles_per_seq
    def in_tile(i):
        j = jnp.minimum(i, n_tiles - 1)
        return j // tiles_per_seq, j % tiles_per_seq

    def out_tile(i):
        j = jnp.maximum(i - 1, 0)
        return j // tiles_per_seq, j % tiles_per_seq

    in_specs = [
        pl.BlockSpec((1, rows, d), lambda i: (*in_tile(i), 0)),
        pl.BlockSpec((1, 1, rows), lambda i: (in_tile(i)[0], 0, in_tile(i)[1])),
        _const_spec((1, d)), _const_spec((half, 1)), _const_spec((GATE_ROWS, 1)),
        _const_spec((1, MLSTM_V_W)),
        pl.BlockSpec(memory_space=pltpu.SMEM),
        _const_spec((d, QKG_W)), _const_spec((d, MLSTM_V_W)),
        _const_spec((d, MLSTM_V_W)), _const_spec((d, ATTN_Q_W)),
        _const_spec((d, 2 * ATTN_KV_W)), _const_spec((d, d)), _const_spec((d, d)),
        _const_spec((MLSTM_V_W, d)), _const_spec((ATTN_Q_W, d)), _const_spec((d, d)),
    ]
    slot = list(_slot_shapes(rows, d))
    shared = [
        pltpu.VMEM((LANES, WINDOW), BF16),
        pltpu.VMEM((WINDOW, LANES), BF16),
        pltpu.VMEM((WINDOW, LANES), BF16),
        pltpu.VMEM((rows, d), F32),
        pltpu.VMEM((rows, d), F32),
        pltpu.VMEM((rows, MLSTM_V_W), BF16),
        pltpu.VMEM((rows, ATTN_Q_W), BF16),
        pltpu.VMEM((rows, d), BF16),
        pltpu.VMEM((MLSTM_HEADS, MLSTM_QK_DIM, VEXT_W), F32),
        pltpu.VMEM((8, LANES), F32),
    ]
    assert len(in_specs) == N_MIXER_IN
    steps = n_tiles + 1
    cast_specs = [_cast_slabs(arr, steps) for arr in cast_weights]
    res = pl.pallas_call(
        functools.partial(_mixer_body, tiles_per_seq=tiles_per_seq, n_cast=len(cast_weights)),
        grid=(steps,),
        in_specs=in_specs + cast_specs,
        out_specs=[pl.BlockSpec((1, rows, d), lambda i: (*out_tile(i), 0))] + cast_specs,
        out_shape=[jax.ShapeDtypeStruct((b, s, d), F32)]
        + [jax.ShapeDtypeStruct(arr.shape, BF16) for arr in cast_weights],
        scratch_shapes=slot + slot + shared,
        compiler_params=pltpu.CompilerParams(
            dimension_semantics=("arbitrary",), vmem_limit_bytes=(7 * VMEM_BYTES_V7X) // 8),
        name="mixer",
    )(x, positions.reshape(b, 1, s), norm_g.reshape(1, d), inv, b_if, out_norm_g.reshape(1, MLSTM_V_W),
      sinks, w_qk, w_v, w_o, w_qa, w_kva, w_gm, w_ga, w_brm, w_bra, w_outp, *cast_weights)
    return res[0], res[1:]


def kernel(x, positions, ffn1_norm_g, ffn1_w_gate, ffn1_w_up, ffn1_w_down, mix_norm_g, w_in, mlstm_b_i, mlstm_b_f, mlstm_out_norm_g, attn_sinks, w_branch_mlstm, w_branch_attn, w_out, ffn2_norm_g, ffn2_w_gate, ffn2_w_up, ffn2_w_down, final_norm_g):
    b, s, d = x.shape
    depth = w_in.shape[0]
    for l in range(depth):
        last = l == depth - 1
        x2, mixer_w = _ffn(x.reshape(b * s, d), ffn1_norm_g[l], ffn1_w_gate[l].astype(BF16),
                           ffn1_w_up[l].astype(BF16), ffn1_w_down[l].astype(BF16),
                           mixer_weights=(w_in[l], w_branch_mlstm[l], w_branch_attn[l], w_out[l]))
        x3, ffn2_w = _mixer(x2.reshape(b, s, d), positions, mix_norm_g[l], mlstm_b_i[l], mlstm_b_f[l],
                            mlstm_out_norm_g[l], attn_sinks[l], mixer_w,
                            (ffn2_w_gate[l], ffn2_w_up[l], ffn2_w_down[l]))
        x = _ffn(x3.reshape(b * s, d), ffn2_norm_g[l], *ffn2_w,
                 final_g=final_norm_g if last else None).reshape(b, s, d)
    return x
```

```python
import collections
import functools

import jax
import jax.numpy as jnp
from jax import lax
from jax.experimental import pallas as pl
from jax.experimental.pallas import tpu as pltpu

D_MODEL = 1024
D_FF = 2816
MLSTM_HEADS = 4
MLSTM_QK_DIM = 64
MLSTM_V_DIM = 128
GATE_SOFTCAP = 15.0
ATTN_Q_HEADS = 8
ATTN_KV_HEADS = 2
ATTN_HEAD_DIM = 64
WINDOW = 128
ROPE_DIM = ATTN_HEAD_DIM // 4
ROPE_THETA = 500000.0
NORM_EPS = 1e-6

MLSTM_QK_W = MLSTM_HEADS * MLSTM_QK_DIM
MLSTM_V_W = MLSTM_HEADS * MLSTM_V_DIM
ATTN_Q_W = ATTN_Q_HEADS * ATTN_HEAD_DIM
ATTN_KV_W = ATTN_KV_HEADS * ATTN_HEAD_DIM
IN_WIDTHS = (MLSTM_QK_W, MLSTM_QK_W, MLSTM_V_W, MLSTM_V_W, MLSTM_HEADS, MLSTM_HEADS,
             ATTN_Q_W, ATTN_KV_W, ATTN_KV_W, D_MODEL, D_MODEL)

LANES = 128
MXU_COLS = 256
VMEM_BYTES_V7X = 64 * 1024 * 1024

FFN_ROWS = 1024
FFN_COL_CHUNK = 2 * MXU_COLS
MIX_ROWS = 512
MIX_COL_CHUNK = 2 * MXU_COLS
CHUNK = LANES
PAIR_W = 2 * MLSTM_QK_DIM
VEXT_W = 2 * MLSTM_V_DIM
GATE_ROWS = 2 * MLSTM_HEADS
QKG_W = 2 * MLSTM_QK_W + LANES
N_MIXER_IN = 17

F32 = jnp.float32
BF16 = jnp.bfloat16


def _rms(x, g):
    return x * lax.rsqrt(jnp.mean(x * x, axis=-1, keepdims=True) + NORM_EPS) * g


def _col_chunks(total, width):
    return [(c, min(c + width, total)) for c in range(0, total, width)]


def _const_spec(shape):
    return pl.BlockSpec(shape, lambda *_: (0,) * len(shape), pipeline_mode=pl.Buffered(1))


def _prep_mixer_weights(win_ref, wbrm_ref, wbra_ref, wout_ref, outs):
    wqk_o, wv_o, wo_o, wqa_o, wkva_o, wgm_o, wga_o, wbrm_o, wbra_o, wout_o = outs
    starts = [0]
    for w in IN_WIDTHS:
        starts.append(starts[-1] + w)
    col = lambda i: win_ref[:, starts[i]:starts[i + 1]]
    gate_pad = jnp.zeros((win_ref.shape[0], LANES - GATE_ROWS), F32)
    wqk_o[...] = jnp.concatenate([col(0) * (MLSTM_QK_DIM ** -0.5), col(1), col(4), col(5), gate_pad],
                                 axis=1).astype(BF16)
    wv_o[...] = col(2).astype(BF16)
    wo_o[...] = col(3).astype(BF16)
    wqa_o[...] = (col(6) * (ATTN_HEAD_DIM ** -0.5)).astype(BF16)
    wkva_o[...] = jnp.concatenate([col(7), col(8)], axis=1).astype(BF16)
    wgm_o[...] = col(9).astype(BF16)
    wga_o[...] = col(10).astype(BF16)
    wbrm_o[...] = wbrm_ref[...].astype(BF16)
    wbra_o[...] = wbra_ref[...].astype(BF16)
    wout_o[...] = wout_ref[...].astype(BF16)


def _ffn_body(*refs, final_norm, prep):
    n_in = 5 + int(final_norm) + (4 if prep else 0)
    x_ref, g_ref, wg_ref, wu_ref, wd_ref = refs[:5]
    o_ref = refs[n_in]
    act_ref = refs[-1]
    x = x_ref[...]
    h = _rms(x, g_ref[...]).astype(BF16)
    for c0, c1 in _col_chunks(D_FF, FFN_COL_CHUNK):
        g = jnp.dot(h, wg_ref[:, c0:c1], preferred_element_type=F32)
        u = jnp.dot(h, wu_ref[:, c0:c1], preferred_element_type=F32)
        act_ref[:, c0:c1] = (g * jax.nn.sigmoid(g) * u).astype(BF16)
    if prep:
        _prep_mixer_weights(*refs[n_in - 4:n_in], refs[n_in + 1:-1])
    r = x + 0.5 * jnp.dot(act_ref[...], wd_ref[...], preferred_element_type=F32)
    if final_norm:
        r = _rms(r, refs[5][...])
    o_ref[...] = r


def _ffn(x2, norm_g, w_gate, w_up, w_down, final_g=None, mixer_weights=None):
    n, d = x2.shape
    steps = n // FFN_ROWS
    final_norm = final_g is not None
    prep = mixer_weights is not None
    row_spec = pl.BlockSpec((FFN_ROWS, d), lambda i: (i, 0))
    in_specs = [row_spec, _const_spec((1, d)), _const_spec((d, D_FF)), _const_spec((d, D_FF)),
                _const_spec((D_FF, d))]
    args = [x2, norm_g.reshape(1, d), w_gate, w_up, w_down]
    out_specs = [row_spec]
    out_shape = [jax.ShapeDtypeStruct((n, d), F32)]
    if final_norm:
        in_specs.append(_const_spec((1, d)))
        args.append(final_g.reshape(1, d))
    if prep:
        slab = lambda arr: pl.BlockSpec((arr.shape[0] // steps, arr.shape[1]), lambda i: (i, 0))
        for arr in mixer_weights:
            in_specs.append(slab(arr))
            args.append(arr)
        piece_widths = (QKG_W, MLSTM_V_W, MLSTM_V_W, ATTN_Q_W, 2 * ATTN_KV_W, d, d)
        pieces = [jax.ShapeDtypeStruct((d, w), BF16) for w in piece_widths]
        pieces += [jax.ShapeDtypeStruct(arr.shape, BF16) for arr in mixer_weights[1:]]
        out_shape += pieces
        out_specs += [slab(p) for p in pieces]
    res = pl.pallas_call(
        functools.partial(_ffn_body, final_norm=final_norm, prep=prep),
        grid=(steps,),
        in_specs=in_specs,
        out_specs=out_specs,
        out_shape=out_shape,
        scratch_shapes=[pltpu.VMEM((FFN_ROWS, D_FF), BF16)],
        compiler_params=pltpu.CompilerParams(
            dimension_semantics=("arbitrary",), vmem_limit_bytes=(3 * VMEM_BYTES_V7X) // 4),
        name="ffn_final" if final_norm else "ffn",
    )(*args)
    return (res[0], res[1:]) if prep else res[0]


Slot = collections.namedtuple("Slot", "xs h q kT vext so qa kcT vc vcs g")


def _slot_shapes(rows, d):
    return Slot(
        xs=pltpu.VMEM((rows, d), F32),
        h=pltpu.VMEM((rows, d), BF16),
        q=pltpu.VMEM((rows, MLSTM_QK_W), BF16),
        kT=pltpu.VMEM((MLSTM_QK_W, rows), F32),
        vext=pltpu.VMEM((rows, MLSTM_HEADS * VEXT_W), BF16),
        so=pltpu.VMEM((rows, MLSTM_V_W), F32),
        qa=pltpu.VMEM((rows, ATTN_Q_W), BF16),
        kcT=pltpu.VMEM((LANES, rows), BF16),
        vc=pltpu.VMEM((rows, LANES), BF16),
        vcs=pltpu.VMEM((rows, LANES), BF16),
        g=pltpu.VMEM((GATE_ROWS, rows), F32),
    )


def _softcap(a):
    return GATE_SOFTCAP * jnp.tanh(a / GATE_SOFTCAP)


def _log_sigmoid(a):
    return jnp.minimum(a, 0.0) - jnp.log1p(jnp.exp(-jnp.abs(a)))


def _lane_cumsum(a):
    lane = lax.broadcasted_iota(jnp.int32, a.shape, 1)
    d = 1
    while d < LANES:
        a = a + jnp.where(lane >= d, pltpu.roll(a, d, 1), 0.0)
        d *= 2
    return a


def _rope_tables(pos_row, inv_col):
    ang = inv_col * pos_row
    c, s = jnp.cos(ang), jnp.sin(ang)
    one, zero = jnp.ones_like(c), jnp.zeros_like(c)
    per_head = ATTN_HEAD_DIM // 8

    def tile(first, second, rest):
        rows = []
        for _ in range(LANES // ATTN_HEAD_DIM):
            rows += [first, second] + [rest] * (per_head - 2)
        return jnp.concatenate(rows, axis=0).T
    return tile(c, c, one), tile(zero, s, zero), tile(-s, zero, zero)


def _rope(a, tables):
    cos_t, sin_up, sin_dn = tables
    half = ROPE_DIM // 2
    return a * cos_t + pltpu.roll(a, half, 1) * sin_up + pltpu.roll(a, LANES - half, 1) * sin_dn


def _rep(a, shape):
    return jnp.broadcast_to(a, shape)


def _mixer_step(t_cur, ins, cur, nxt, shared, outs):
    (x_ref, pos_ref, ng_ref, inv_ref, bif_ref, ong_ref, sink_ref,
     wqk_ref, wv_ref, wo_ref, wqa_ref, wkva_ref, wgm_ref, wga_ref,
     wbrm_ref, wbra_ref, wout_ref) = ins[:N_MIXER_IN]
    cast_in = ins[N_MIXER_IN:]
    o_ref, cast_out = outs[0], outs[1:]
    pkT_scr, pv_scr, pvs_scr, sgm_scr, sga_scr, hm_scr, oa_scr, mg_scr, c_scr, m_scr = shared
    rows = MIX_ROWS
    n_chunks = rows // CHUNK
    n_blocks = rows // WINDOW
    group = ATTN_Q_HEADS // ATTN_KV_HEADS

    shared_vals = {}

    def p_norm():
        x_new = x_ref[0]
        nxt.xs[...] = x_new
        nxt.h[...] = _rms(x_new, ng_ref[...]).astype(BF16)

    def side_cast():
        for src, dst in zip(cast_in, cast_out):
            dst[...] = src[...].astype(BF16)

    def p_qk():
        qkg = jnp.dot(nxt.h[...], wqk_ref[...], preferred_element_type=F32)
        nxt.q[...] = qkg[:, 0:MLSTM_QK_W].astype(BF16)
        nxt.kT[...] = qkg[:, MLSTM_QK_W:2 * MLSTM_QK_W].T
        gates_t = qkg[:, 2 * MLSTM_QK_W:QKG_W].T
        nxt.g[...] = _softcap(gates_t[0:GATE_ROWS] + bif_ref[...])

    def p_v():
        v = jnp.dot(nxt.h[...], wv_ref[...], preferred_element_type=F32)
        for hd in range(MLSTM_HEADS):
            nxt.vext[:, hd * VEXT_W:hd * VEXT_W + MLSTM_V_DIM] = (
                v[:, hd * MLSTM_V_DIM:(hd + 1) * MLSTM_V_DIM].astype(BF16))
            nxt.vext[:, hd * VEXT_W + MLSTM_V_DIM:(hd + 1) * VEXT_W] = jnp.ones((rows, MLSTM_V_DIM), BF16)

    def p_so():
        nxt.so[...] = jax.nn.sigmoid(jnp.dot(nxt.h[...], wo_ref[...], preferred_element_type=F32))

    def p_qa():
        shared_vals["tables"] = _rope_tables(pos_ref[0].astype(F32), inv_ref[...])
        qa = jnp.dot(nxt.h[...], wqa_ref[...], preferred_element_type=F32)
        for p in range(ATTN_Q_W // LANES):
            nxt.qa[:, p * LANES:(p + 1) * LANES] = _rope(
                qa[:, p * LANES:(p + 1) * LANES], shared_vals["tables"]).astype(BF16)

    def p_kva():
        kva = jnp.dot(nxt.h[...], wkva_ref[...], preferred_element_type=F32)
        ka = _rope(kva[:, 0:ATTN_KV_W], shared_vals["tables"])
        va = kva[:, ATTN_KV_W:2 * ATTN_KV_W]
        nxt.kcT[...] = ka.T.astype(BF16)
        nxt.vc[...] = va.astype(BF16)
        nxt.vcs[...] = pltpu.roll(va, ATTN_HEAD_DIM, 1).astype(BF16)

    def gate_piece(idx):
        gref, wref = (sgm_scr, wgm_ref) if idx < 2 else (sga_scr, wga_ref)
        c0 = (idx % 2) * MIX_COL_CHUNK
        gref[:, c0:c0 + MIX_COL_CHUNK] = jax.nn.sigmoid(
            jnp.dot(cur.h[...], wref[:, c0:c0 + MIX_COL_CHUNK], preferred_element_type=F32))

    qi = lax.broadcasted_iota(jnp.int32, (CHUNK, CHUNK), 0)
    si = lax.broadcasted_iota(jnp.int32, (CHUNK, CHUNK), 1)
    tri = si <= qi
    zeros_c = jnp.zeros((MLSTM_QK_DIM, VEXT_W), BF16)
    ong = ong_ref[...]
    tile_shape = (CHUNK, CHUNK)
    rep_shape = (MLSTM_HEADS, LANES)

    chunk_stats = []
    m_vec = m_scr[0:MLSTM_HEADS, :]
    for c in range(n_chunks):
        g_c = cur.g[:, c * CHUNK:(c + 1) * CHUNK]
        i_r = g_c[0:MLSTM_HEADS]
        logf_r = _log_sigmoid(g_c[MLSTM_HEADS:GATE_ROWS])
        b_r = _lane_cumsum(logf_r)
        b_last = _rep(b_r[:, CHUNK - 1:CHUNK], rep_shape)
        a_r = b_last - b_r + i_r
        a_max = _rep(jnp.max(a_r, axis=1, keepdims=True), rep_shape)
        m_new = jnp.maximum(b_last + m_vec, a_max)
        chunk_stats.append(dict(
            logf_r=logf_r, ib_r=i_r - b_r, m_prev=m_vec,
            decay=jnp.exp(b_last + m_vec - m_new), scale=jnp.exp(a_max - m_new),
            w_loc=jnp.exp(a_r - a_max)))
        m_vec = m_new
    c_state = [c_scr[hd] for hd in range(MLSTM_HEADS)]
    c_next = [None] * MLSTM_HEADS
    s_qk = {}

    def mlstm_scores(c):
        rs = slice(c * CHUNK, (c + 1) * CHUNK)
        for pair in range(MLSTM_HEADS // 2):
            q2 = cur.q[rs, pair * PAIR_W:(pair + 1) * PAIR_W]
            k_even, k_odd = (cur.kT[hd * MLSTM_QK_DIM:(hd + 1) * MLSTM_QK_DIM, rs].astype(BF16)
                             for hd in (2 * pair, 2 * pair + 1))
            zk = jnp.zeros_like(k_even)
            k_both = jnp.concatenate([jnp.concatenate([k_even, zk], axis=1),
                                      jnp.concatenate([zk, k_odd], axis=1)], axis=0)
            s2 = jnp.dot(q2, k_both, preferred_element_type=F32)
            s_qk[c, 2 * pair] = s2[:, 0:CHUNK]
            s_qk[c, 2 * pair + 1] = s2[:, CHUNK:2 * CHUNK]

    def mlstm_state(c):
        rs = slice(c * CHUNK, (c + 1) * CHUNK)
        st = chunk_stats[c]
        for hd in range(MLSTM_HEADS):
            vext = cur.vext[rs, hd * VEXT_W:(hd + 1) * VEXT_W]
            ktw = (cur.kT[hd * MLSTM_QK_DIM:(hd + 1) * MLSTM_QK_DIM, rs] * st["w_loc"][hd:hd + 1]).astype(BF16)
            u = jnp.dot(ktw, vext, preferred_element_type=F32)
            decay = jnp.concatenate([st["decay"][hd:hd + 1]] * 2, axis=1)
            scale = jnp.concatenate([st["scale"][hd:hd + 1]] * 2, axis=1)
            c_next[hd] = decay * c_state[hd] + scale * u

    def mlstm_out(c):
        rs = slice(c * CHUNK, (c + 1) * CHUNK)
        st = chunk_stats[c]
        for hd in range(MLSTM_HEADS):
            pair, odd = divmod(hd, 2)
            q2 = cur.q[rs, pair * PAIR_W:(pair + 1) * PAIR_W]
            vext = cur.vext[rs, hd * VEXT_W:(hd + 1) * VEXT_W]
            b_col = _rep(jnp.sum(jnp.where(tri, st["logf_r"][hd:hd + 1], 0.0), axis=1, keepdims=True),
                         tile_shape)
            log_d = jnp.where(tri, b_col + st["ib_r"][hd:hd + 1], -jnp.inf)
            inter_log = b_col + st["m_prev"][hd:hd + 1]
            m_j = jnp.maximum(inter_log, _rep(jnp.max(log_d, axis=1, keepdims=True), tile_shape))
            sqk = (s_qk.pop((c, hd)) * jnp.exp(log_d - m_j)).astype(BF16)
            c_bf = c_state[hd].astype(BF16)
            c_pad = jnp.concatenate([zeros_c, c_bf] if odd else [c_bf, zeros_c], axis=0)
            q_inter = (q2 * jnp.exp(inter_log - m_j)).astype(BF16)
            tot = jnp.dot(jnp.concatenate([sqk, q_inter], axis=1), jnp.concatenate([vext, c_pad], axis=0),
                          preferred_element_type=F32)
            num = tot[:, 0:MLSTM_V_DIM]
            den = tot[:, MLSTM_V_DIM:VEXT_W]
            hh = num / jnp.maximum(jnp.abs(den), jnp.exp(-m_j))
            vs = slice(hd * MLSTM_V_DIM, (hd + 1) * MLSTM_V_DIM)
            hn = hh * lax.rsqrt(jnp.mean(hh * hh, axis=1, keepdims=True) + NORM_EPS) * ong[:, vs]
            hm_scr[rs, vs] = (hn * cur.so[rs, vs]).astype(BF16)
            c_state[hd] = c_next[hd]

    qi2 = lax.broadcasted_iota(jnp.int32, (WINDOW, 2 * WINDOW), 0)
    si2 = lax.broadcasted_iota(jnp.int32, (WINDOW, 2 * WINDOW), 1)
    band = (si2 > qi2) & (si2 <= qi2 + WINDOW)
    band_first = band & ((si2 >= WINDOW) | (t_cur > 0))
    kv_lo = lax.broadcasted_iota(jnp.int32, (2 * WINDOW, LANES), 1) < ATTN_HEAD_DIM
    ones_lo = jnp.where(kv_lo, 1.0, 0.0).astype(BF16)
    ones_hi = jnp.where(kv_lo, 0.0, 1.0).astype(BF16)
    zkv = jnp.zeros((2 * WINDOW, LANES), BF16)
    attn = {}

    def window_rows(tile_ref, prev_ref, nb):
        if nb == 0:
            return jnp.concatenate([prev_ref[...], tile_ref[0:WINDOW, :]], axis=0)
        return tile_ref[(nb - 1) * WINDOW:(nb + 1) * WINDOW, :]

    def attn_scores(nb):
        qs = slice(nb * WINDOW, (nb + 1) * WINDOW)
        mask = band_first if nb == 0 else band
        if nb == 0:
            k_win = jnp.concatenate([pkT_scr[...], cur.kcT[:, 0:WINDOW]], axis=1)
        else:
            k_win = cur.kcT[:, (nb - 1) * WINDOW:(nb + 1) * WINDOW]
        zk = jnp.zeros((ATTN_HEAD_DIM, 2 * WINDOW), BF16)
        for kh in range(ATTN_KV_HEADS):
            k_h = k_win[kh * ATTN_HEAD_DIM:(kh + 1) * ATTN_HEAD_DIM]
            k_for_even = jnp.concatenate([k_h, zk], axis=0)
            k_for_odd = jnp.concatenate([zk, k_h], axis=0)
            pair0 = kh * (group // 2)
            lhs = jnp.concatenate([cur.qa[qs, (pair0 + j) * LANES:(pair0 + j + 1) * LANES]
                                   for j in range(group // 2)], axis=0)
            probs, sink_terms = [], []
            for odd, k_sel in enumerate((k_for_even, k_for_odd)):
                s_all = jnp.dot(lhs, k_sel, preferred_element_type=F32)
                p_rows, sk_rows = [], []
                for j in range(group // 2):
                    sink = sink_ref[kh * group + 2 * j + odd]
                    sc = jnp.where(mask, s_all[j * WINDOW:(j + 1) * WINDOW], -jnp.inf)
                    mx = jnp.maximum(jnp.max(sc, axis=1, keepdims=True), sink)
                    p_rows.append(jnp.exp(sc - mx).astype(BF16))
                    sk_rows.append(_rep(jnp.exp(sink - mx), (WINDOW, LANES)))
                probs.append(jnp.concatenate(p_rows, axis=0))
                sink_terms.append(jnp.concatenate(sk_rows, axis=0))
            attn[nb, kh] = (jnp.concatenate(probs, axis=1), jnp.where(kv_lo, sink_terms[0], sink_terms[1]))

    def attn_values(nb):
        qs = slice(nb * WINDOW, (nb + 1) * WINDOW)
        v_own, v_sw = window_rows(cur.vc, pv_scr, nb), window_rows(cur.vcs, pvs_scr, nb)
        for kh in range(ATTN_KV_HEADS):
            own, swapped = (v_own, v_sw) if kh == 0 else (v_sw, v_own)
            v_lo = jnp.where(kv_lo, own, zkv)
            v_hi = jnp.where(kv_lo, zkv, swapped)
            probs, sink_term = attn.pop((nb, kh))
            rhs = jnp.concatenate([jnp.concatenate([v_lo, ones_lo], axis=1),
                                   jnp.concatenate([v_hi, ones_hi], axis=1)], axis=0)
            o = jnp.dot(probs, rhs, preferred_element_type=F32)
            on = (o[:, 0:LANES] / (o[:, LANES:2 * LANES] + sink_term)).astype(BF16)
            pair0 = kh * (group // 2)
            for j in range(group // 2):
                oa_scr[qs, (pair0 + j) * LANES:(pair0 + j + 1) * LANES] = on[j * WINDOW:(j + 1) * WINDOW]

    def branch_attn(idx):
        c0 = idx * MIX_COL_CHUNK
        y_a = jnp.dot(oa_scr[...], wbra_ref[:, c0:c0 + MIX_COL_CHUNK], preferred_element_type=F32)
        sga_scr[:, c0:c0 + MIX_COL_CHUNK] = sga_scr[:, c0:c0 + MIX_COL_CHUNK] * y_a

    gate_piece(0)
    p_norm()
    side_cast()
    attn_scores(0); mlstm_scores(0); mlstm_state(0)
    gate_piece(1)
    attn_values(0); mlstm_out(0)
    p_qk()
    attn_scores(1); mlstm_scores(1); mlstm_state(1)
    p_v()
    attn_values(1); mlstm_out(1)
    p_so()
    attn_scores(2); mlstm_scores(2); mlstm_state(2)
    gate_piece(2)
    attn_values(2); mlstm_out(2)
    p_qa()
    attn_scores(3); mlstm_scores(3); mlstm_state(3)
    gate_piece(3)
    attn_values(3)
    p_kva()
    mlstm_out(3)
    branch_attn(0); branch_attn(1)

    for hd in range(MLSTM_HEADS):
        c_scr[hd] = c_state[hd]
    m_scr[0:MLSTM_HEADS, :] = m_vec
    pkT_scr[...] = cur.kcT[:, rows - WINDOW:rows]
    for prev_ref, tile_ref in ((pv_scr, cur.vc), (pvs_scr, cur.vcs)):
        prev_ref[...] = tile_ref[rows - WINDOW:rows, :]

    for c0, c1 in _col_chunks(D_MODEL, MIX_COL_CHUNK):
        y_m = jnp.dot(hm_scr[...], wbrm_ref[:, c0:c1], preferred_element_type=F32)
        mg_scr[:, c0:c1] = (sgm_scr[:, c0:c1] * y_m + sga_scr[:, c0:c1]).astype(BF16)
    o_ref[0] = cur.xs[...] + jnp.dot(mg_scr[...], wout_ref[...], preferred_element_type=F32)


def _mixer_body(*refs, tiles_per_seq, n_cast):
    n_in = N_MIXER_IN + n_cast
    n_out = 1 + n_cast
    n_slot = len(Slot._fields)
    ins = refs[:n_in]
    outs = refs[n_in:n_in + n_out]
    scr = refs[n_in + n_out:]
    slot_a, slot_b = Slot(*scr[:n_slot]), Slot(*scr[n_slot:2 * n_slot])
    shared = scr[2 * n_slot:]
    prev_blocks = shared[0:3]
    c_scr, m_scr = shared[-2:]
    s = pl.program_id(0)
    t_cur = lax.rem(s + tiles_per_seq - 1, tiles_per_seq)

    @pl.when(s == 0)
    def _():
        for ref in list(slot_b) + list(shared):
            ref[...] = jnp.zeros_like(ref)

    @pl.when(t_cur == 0)
    def _():
        c_scr[...] = jnp.zeros_like(c_scr)
        m_scr[...] = jnp.zeros_like(m_scr)
        for ref in prev_blocks:
            ref[...] = jnp.zeros_like(ref)

    even = lax.rem(s, 2) == 0

    @pl.when(even)
    def _():
        _mixer_step(t_cur, ins, slot_b, slot_a, shared, outs)

    @pl.when(jnp.logical_not(even))
    def _():
        _mixer_step(t_cur, ins, slot_a, slot_b, shared, outs)


def _cast_slabs(arr, steps):
    rows = arr.shape[0]
    slab = 16
    while rows % slab or rows // slab > steps:
        slab += 16
    n_slabs = rows // slab
    return pl.BlockSpec((slab, arr.shape[1]), lambda i: (jnp.minimum(i, n_slabs - 1), 0))


def _mixer(x, positions, norm_g, b_i, b_f, out_norm_g, sinks, weights, cast_weights):
    b, s, d = x.shape
    w_qk, w_v, w_o, w_qa, w_kva, w_gm, w_ga, w_brm, w_bra, w_outp = weights
    b_if = jnp.concatenate([b_i, b_f]).reshape(GATE_ROWS, 1)
    half = ROPE_DIM // 2
    inv = (ROPE_THETA ** (-jnp.arange(half, dtype=F32) * 2.0 / ROPE_DIM)).reshape(half, 1)

    rows = MIX_ROWS
    tiles_per_seq = s // rows
    n_tiles = b * tiles_per_seq
    def in_tile(i):
        j = jnp.minimum(i, n_tiles - 1)
        return j // tiles_per_seq, j % tiles_per_seq

    def out_tile(i):
        j = jnp.maximum(i - 1, 0)
        return j // tiles_per_seq, j % tiles_per_seq

    in_specs = [
        pl.BlockSpec((1, rows, d), lambda i: (*in_tile(i), 0)),
        pl.BlockSpec((1, 1, rows), lambda i: (in_tile(i)[0], 0, in_tile(i)[1])),
        _const_spec((1, d)), _const_spec((half, 1)), _const_spec((GATE_ROWS, 1)),
        _const_spec((1, MLSTM_V_W)),
        pl.BlockSpec(memory_space=pltpu.SMEM),
        _const_spec((d, QKG_W)), _const_spec((d, MLSTM_V_W)),
        _const_spec((d, MLSTM_V_W)), _const_spec((d, ATTN_Q_W)),
        _const_spec((d, 2 * ATTN_KV_W)), _const_spec((d, d)), _const_spec((d, d)),
        _const_spec((MLSTM_V_W, d)), _const_spec((ATTN_Q_W, d)), _const_spec((d, d)),
    ]
    slot = list(_slot_shapes(rows, d))
    shared = [
        pltpu.VMEM((LANES, WINDOW), BF16),
        pltpu.VMEM((WINDOW, LANES), BF16),
        pltpu.VMEM((WINDOW, LANES), BF16),
        pltpu.VMEM((rows, d), F32),
        pltpu.VMEM((rows, d), F32),
        pltpu.VMEM((rows, MLSTM_V_W), BF16),
        pltpu.VMEM((rows, ATTN_Q_W), BF16),
        pltpu.VMEM((rows, d), BF16),
        pltpu.VMEM((MLSTM_HEADS, MLSTM_QK_DIM, VEXT_W), F32),
        pltpu.VMEM((8, LANES), F32),
    ]
    assert len(in_specs) == N_MIXER_IN
    steps = n_tiles + 1
    cast_specs = [_cast_slabs(arr, steps) for arr in cast_weights]
    res = pl.pallas_call(
        functools.partial(_mixer_body, tiles_per_seq=tiles_per_seq, n_cast=len(cast_weights)),
        grid=(steps,),
        in_specs=in_specs + cast_specs,
        out_specs=[pl.BlockSpec((1, rows, d), lambda i: (*out_tile(i), 0))] + cast_specs,
        out_shape=[jax.ShapeDtypeStruct((b, s, d), F32)]
        + [jax.ShapeDtypeStruct(arr.shape, BF16) for arr in cast_weights],
        scratch_shapes=slot + slot + shared,
        compiler_params=pltpu.CompilerParams(
            dimension_semantics=("arbitrary",), vmem_limit_bytes=(7 * VMEM_BYTES_V7X) // 8),
        name="mixer",
    )(x, positions.reshape(b, 1, s), norm_g.reshape(1, d), inv, b_if, out_norm_g.reshape(1, MLSTM_V_W),
      sinks, w_qk, w_v, w_o, w_qa, w_kva, w_gm, w_ga, w_brm, w_bra, w_outp, *cast_weights)
    return res[0], res[1:]


def kernel(x, positions, ffn1_norm_g, ffn1_w_gate, ffn1_w_up, ffn1_w_down, mix_norm_g, w_in, mlstm_b_i, mlstm_b_f, mlstm_out_norm_g, attn_sinks, w_branch_mlstm, w_branch_attn, w_out, ffn2_norm_g, ffn2_w_gate, ffn2_w_up, ffn2_w_down, final_norm_g):
    b, s, d = x.shape
    depth = w_in.shape[0]
    for l in range(depth):
        last = l == depth - 1
        x2, mixer_w = _ffn(x.reshape(b * s, d), ffn1_norm_g[l], ffn1_w_gate[l].astype(BF16),
                           ffn1_w_up[l].astype(BF16), ffn1_w_down[l].astype(BF16),
                           mixer_weights=(w_in[l], w_branch_mlstm[l], w_branch_attn[l], w_out[l]))
        x3, ffn2_w = _mixer(x2.reshape(b, s, d), positions, mix_norm_g[l], mlstm_b_i[l], mlstm_b_f[l],
                            mlstm_out_norm_g[l], attn_sinks[l], mixer_w,
                            (ffn2_w_gate[l], ffn2_w_up[l], ffn2_w_down[l]))
        x = _ffn(x3.reshape(b * s, d), ffn2_norm_g[l], *ffn2_w,
                 final_g=final_norm_g if last else None).reshape(b, s, d)
    return x
```

```python
import collections
import functools

import jax
import jax.numpy as jnp
from jax import lax
from jax.experimental import pallas as pl
from jax.experimental.pallas import tpu as pltpu

D_MODEL = 1024
D_FF = 2816
MLSTM_HEADS = 4
MLSTM_QK_DIM = 64
MLSTM_V_DIM = 128
GATE_SOFTCAP = 15.0
ATTN_Q_HEADS = 8
ATTN_KV_HEADS = 2
ATTN_HEAD_DIM = 64
WINDOW = 128
ROPE_DIM = ATTN_HEAD_DIM // 4
ROPE_THETA = 500000.0
NORM_EPS = 1e-6

MLSTM_QK_W = MLSTM_HEADS * MLSTM_QK_DIM
MLSTM_V_W = MLSTM_HEADS * MLSTM_V_DIM
ATTN_Q_W = ATTN_Q_HEADS * ATTN_HEAD_DIM
ATTN_KV_W = ATTN_KV_HEADS * ATTN_HEAD_DIM
IN_WIDTHS = (MLSTM_QK_W, MLSTM_QK_W, MLSTM_V_W, MLSTM_V_W, MLSTM_HEADS, MLSTM_HEADS,
             ATTN_Q_W, ATTN_KV_W, ATTN_KV_W, D_MODEL, D_MODEL)

LANES = 128
MXU_COLS = 256
VMEM_BYTES_V7X = 64 * 1024 * 1024

FFN_ROWS = 1024
FFN_COL_CHUNK = 2 * MXU_COLS
MIX_ROWS = 512
MIX_COL_CHUNK = 2 * MXU_COLS
CHUNK = LANES
PAIR_W = 2 * MLSTM_QK_DIM
VEXT_W = 2 * MLSTM_V_DIM
SUM_ROWS = 16
GATE_ROWS = 2 * MLSTM_HEADS
QKG_W = 2 * MLSTM_QK_W + LANES
N_MIXER_IN = 17

F32 = jnp.float32
BF16 = jnp.bfloat16


def _rms(x, g):
    return x * lax.rsqrt(jnp.mean(x * x, axis=-1, keepdims=True) + NORM_EPS) * g


def _col_chunks(total, width):
    return [(c, min(c + width, total)) for c in range(0, total, width)]


def _const_spec(shape):
    return pl.BlockSpec(shape, lambda *_: (0,) * len(shape), pipeline_mode=pl.Buffered(1))


def _prep_mixer_weights(win_ref, wbrm_ref, wbra_ref, wout_ref, outs):
    wqk_o, wv_o, wo_o, wqa_o, wkva_o, wgm_o, wga_o, wbrm_o, wbra_o, wout_o = outs
    starts = [0]
    for w in IN_WIDTHS:
        starts.append(starts[-1] + w)
    col = lambda i: win_ref[:, starts[i]:starts[i + 1]]
    gate_pad = jnp.zeros((win_ref.shape[0], LANES - GATE_ROWS), F32)
    wqk_o[...] = jnp.concatenate([col(0) * (MLSTM_QK_DIM ** -0.5), col(1), col(4), col(5), gate_pad],
                                 axis=1).astype(BF16)
    wv_o[...] = col(2).astype(BF16)
    wo_o[...] = col(3).astype(BF16)
    wqa_o[...] = (col(6) * (ATTN_HEAD_DIM ** -0.5)).astype(BF16)
    wkva_o[...] = jnp.concatenate([col(7), col(8)], axis=1).astype(BF16)
    wgm_o[...] = col(9).astype(BF16)
    wga_o[...] = col(10).astype(BF16)
    wbrm_o[...] = wbrm_ref[...].astype(BF16)
    wbra_o[...] = wbra_ref[...].astype(BF16)
    wout_o[...] = wout_ref[...].astype(BF16)


def _ffn_body(*refs, final_norm, prep):
    n_in = 5 + int(final_norm) + (4 if prep else 0)
    x_ref, g_ref, wg_ref, wu_ref, wd_ref = refs[:5]
    o_ref = refs[n_in]
    act_ref = refs[-1]
    x = x_ref[...]
    h = _rms(x, g_ref[...]).astype(BF16)
    for c0, c1 in _col_chunks(D_FF, FFN_COL_CHUNK):
        g = jnp.dot(h, wg_ref[:, c0:c1], preferred_element_type=F32)
        u = jnp.dot(h, wu_ref[:, c0:c1], preferred_element_type=F32)
        act_ref[:, c0:c1] = (g * jax.nn.sigmoid(g) * u).astype(BF16)
    if prep:
        _prep_mixer_weights(*refs[n_in - 4:n_in], refs[n_in + 1:-1])
    r = x + 0.5 * jnp.dot(act_ref[...], wd_ref[...], preferred_element_type=F32)
    if final_norm:
        r = _rms(r, refs[5][...])
    o_ref[...] = r


def _ffn(x2, norm_g, w_gate, w_up, w_down, final_g=None, mixer_weights=None):
    n, d = x2.shape
    steps = n // FFN_ROWS
    final_norm = final_g is not None
    prep = mixer_weights is not None
    row_spec = pl.BlockSpec((FFN_ROWS, d), lambda i: (i, 0))
    in_specs = [row_spec, _const_spec((1, d)), _const_spec((d, D_FF)), _const_spec((d, D_FF)),
                _const_spec((D_FF, d))]
    args = [x2, norm_g.reshape(1, d), w_gate, w_up, w_down]
    out_specs = [row_spec]
    out_shape = [jax.ShapeDtypeStruct((n, d), F32)]
    if final_norm:
        in_specs.append(_const_spec((1, d)))
        args.append(final_g.reshape(1, d))
    if prep:
        slab = lambda arr: pl.BlockSpec((arr.shape[0] // steps, arr.shape[1]), lambda i: (i, 0))
        for arr in mixer_weights:
            in_specs.append(slab(arr))
            args.append(arr)
        piece_widths = (QKG_W, MLSTM_V_W, MLSTM_V_W, ATTN_Q_W, 2 * ATTN_KV_W, d, d)
        pieces = [jax.ShapeDtypeStruct((d, w), BF16) for w in piece_widths]
        pieces += [jax.ShapeDtypeStruct(arr.shape, BF16) for arr in mixer_weights[1:]]
        out_shape += pieces
        out_specs += [slab(p) for p in pieces]
    res = pl.pallas_call(
        functools.partial(_ffn_body, final_norm=final_norm, prep=prep),
        grid=(steps,),
        in_specs=in_specs,
        out_specs=out_specs,
        out_shape=out_shape,
        scratch_shapes=[pltpu.VMEM((FFN_ROWS, D_FF), BF16)],
        compiler_params=pltpu.CompilerParams(
            dimension_semantics=("arbitrary",), vmem_limit_bytes=(3 * VMEM_BYTES_V7X) // 4),
        name="ffn_final" if final_norm else "ffn",
    )(*args)
    return (res[0], res[1:]) if prep else res[0]


Slot = collections.namedtuple("Slot", "xs h q kT vext so qaT kc vcT g")


def _slot_shapes(rows, d):
    return Slot(
        xs=pltpu.VMEM((rows, d), F32),
        h=pltpu.VMEM((rows, d), BF16),
        q=pltpu.VMEM((rows, MLSTM_QK_W), BF16),
        kT=pltpu.VMEM((MLSTM_QK_W, rows), F32),
        vext=pltpu.VMEM((rows, MLSTM_HEADS * VEXT_W), BF16),
        so=pltpu.VMEM((rows, MLSTM_V_W), F32),
        qaT=pltpu.VMEM((ATTN_Q_W, rows), BF16),
        kc=pltpu.VMEM((rows, LANES), BF16),
        vcT=pltpu.VMEM((LANES, rows), BF16),
        g=pltpu.VMEM((GATE_ROWS, rows), F32),
    )


def _softcap(a):
    return GATE_SOFTCAP * jnp.tanh(a / GATE_SOFTCAP)


def _log_sigmoid(a):
    return jnp.minimum(a, 0.0) - jnp.log1p(jnp.exp(-jnp.abs(a)))


def _lane_cumsum(a):
    lane = lax.broadcasted_iota(jnp.int32, a.shape, 1)
    d = 1
    while d < LANES:
        a = a + jnp.where(lane >= d, pltpu.roll(a, d, 1), 0.0)
        d *= 2
    return a


def _rope_tables(pos_row, inv_col):
    ang = inv_col * pos_row
    c, s = jnp.cos(ang), jnp.sin(ang)
    one, zero = jnp.ones_like(c), jnp.zeros_like(c)
    per_head = ATTN_HEAD_DIM // 8

    def tile(first, second, rest):
        rows = []
        for _ in range(LANES // ATTN_HEAD_DIM):
            rows += [first, second] + [rest] * (per_head - 2)
        return jnp.concatenate(rows, axis=0).T
    return tile(c, c, one), tile(zero, s, zero), tile(-s, zero, zero)


def _rope(a, tables):
    cos_t, sin_up, sin_dn = tables
    half = ROPE_DIM // 2
    return a * cos_t + pltpu.roll(a, half, 1) * sin_up + pltpu.roll(a, LANES - half, 1) * sin_dn


def _rep(a, shape):
    return jnp.broadcast_to(a, shape)


def _mixer_step(t_cur, ins, cur, nxt, shared, outs):
    (x_ref, pos_ref, ng_ref, inv_ref, bif_ref, ong_ref, sink_ref,
     wqk_ref, wv_ref, wo_ref, wqa_ref, wkva_ref, wgm_ref, wga_ref,
     wbrm_ref, wbra_ref, wout_ref) = ins[:N_MIXER_IN]
    cast_in = ins[N_MIXER_IN:]
    o_ref, cast_out = outs[0], outs[1:]
    pk_scr, pvT_scr, sgm_scr, sga_scr, hm_scr, oa_scr, mg_scr, c_scr, m_scr = shared
    rows = MIX_ROWS
    n_chunks = rows // CHUNK
    n_blocks = rows // WINDOW
    group = ATTN_Q_HEADS // ATTN_KV_HEADS

    shared_vals = {}

    def p_norm():
        x_new = x_ref[0]
        nxt.xs[...] = x_new
        nxt.h[...] = _rms(x_new, ng_ref[...]).astype(BF16)

    def side_cast():
        for src, dst in zip(cast_in, cast_out):
            dst[...] = src[...].astype(BF16)

    def p_qk():
        qkg = jnp.dot(nxt.h[...], wqk_ref[...], preferred_element_type=F32)
        nxt.q[...] = qkg[:, 0:MLSTM_QK_W].astype(BF16)
        nxt.kT[...] = qkg[:, MLSTM_QK_W:2 * MLSTM_QK_W].T
        gates_t = qkg[:, 2 * MLSTM_QK_W:QKG_W].T
        nxt.g[...] = _softcap(gates_t[0:GATE_ROWS] + bif_ref[...])

    def p_v():
        v = jnp.dot(nxt.h[...], wv_ref[...], preferred_element_type=F32)
        for hd in range(MLSTM_HEADS):
            nxt.vext[:, hd * VEXT_W:hd * VEXT_W + MLSTM_V_DIM] = (
                v[:, hd * MLSTM_V_DIM:(hd + 1) * MLSTM_V_DIM].astype(BF16))
            nxt.vext[:, hd * VEXT_W + MLSTM_V_DIM:(hd + 1) * VEXT_W] = jnp.ones((rows, MLSTM_V_DIM), BF16)

    def p_so():
        nxt.so[...] = jax.nn.sigmoid(jnp.dot(nxt.h[...], wo_ref[...], preferred_element_type=F32))

    def p_qa():
        shared_vals["tables"] = _rope_tables(pos_ref[0].astype(F32), inv_ref[...])
        qa = jnp.dot(nxt.h[...], wqa_ref[...], preferred_element_type=F32)
        for p in range(ATTN_Q_W // LANES):
            nxt.qaT[p * LANES:(p + 1) * LANES, :] = _rope(
                qa[:, p * LANES:(p + 1) * LANES], shared_vals["tables"]).T.astype(BF16)

    def p_kva():
        kva = jnp.dot(nxt.h[...], wkva_ref[...], preferred_element_type=F32)
        ka = _rope(kva[:, 0:ATTN_KV_W], shared_vals["tables"])
        va = kva[:, ATTN_KV_W:2 * ATTN_KV_W]
        nxt.kc[...] = ka.astype(BF16)
        nxt.vcT[...] = va.T.astype(BF16)

    def gate_piece(idx):
        gref, wref = (sgm_scr, wgm_ref) if idx < 2 else (sga_scr, wga_ref)
        c0 = (idx % 2) * MIX_COL_CHUNK
        gref[:, c0:c0 + MIX_COL_CHUNK] = jax.nn.sigmoid(
            jnp.dot(cur.h[...], wref[:, c0:c0 + MIX_COL_CHUNK], preferred_element_type=F32))

    qi = lax.broadcasted_iota(jnp.int32, (CHUNK, CHUNK), 0)
    si = lax.broadcasted_iota(jnp.int32, (CHUNK, CHUNK), 1)
    tri = si <= qi
    zeros_c = jnp.zeros((MLSTM_QK_DIM, VEXT_W), BF16)
    ong = ong_ref[...]
    tile_shape = (CHUNK, CHUNK)
    rep_shape = (MLSTM_HEADS, LANES)

    chunk_stats = []
    m_vec = m_scr[0:MLSTM_HEADS, :]
    for c in range(n_chunks):
        g_c = cur.g[:, c * CHUNK:(c + 1) * CHUNK]
        i_r = g_c[0:MLSTM_HEADS]
        logf_r = _log_sigmoid(g_c[MLSTM_HEADS:GATE_ROWS])
        b_r = _lane_cumsum(logf_r)
        b_last = _rep(b_r[:, CHUNK - 1:CHUNK], rep_shape)
        a_r = b_last - b_r + i_r
        a_max = _rep(jnp.max(a_r, axis=1, keepdims=True), rep_shape)
        m_new = jnp.maximum(b_last + m_vec, a_max)
        chunk_stats.append(dict(
            logf_r=logf_r, ib_r=i_r - b_r, m_prev=m_vec,
            decay=jnp.exp(b_last + m_vec - m_new), scale=jnp.exp(a_max - m_new),
            w_loc=jnp.exp(a_r - a_max)))
        m_vec = m_new
    c_state = [c_scr[hd] for hd in range(MLSTM_HEADS)]
    c_next = [None] * MLSTM_HEADS
    s_qk = {}

    def mlstm_scores(c):
        rs = slice(c * CHUNK, (c + 1) * CHUNK)
        for pair in range(MLSTM_HEADS // 2):
            q2 = cur.q[rs, pair * PAIR_W:(pair + 1) * PAIR_W]
            k_even, k_odd = (cur.kT[hd * MLSTM_QK_DIM:(hd + 1) * MLSTM_QK_DIM, rs].astype(BF16)
                             for hd in (2 * pair, 2 * pair + 1))
            zk = jnp.zeros_like(k_even)
            k_both = jnp.concatenate([jnp.concatenate([k_even, zk], axis=1),
                                      jnp.concatenate([zk, k_odd], axis=1)], axis=0)
            s2 = jnp.dot(q2, k_both, preferred_element_type=F32)
            s_qk[c, 2 * pair] = s2[:, 0:CHUNK]
            s_qk[c, 2 * pair + 1] = s2[:, CHUNK:2 * CHUNK]

    def mlstm_state(c):
        rs = slice(c * CHUNK, (c + 1) * CHUNK)
        st = chunk_stats[c]
        for hd in range(MLSTM_HEADS):
            vext = cur.vext[rs, hd * VEXT_W:(hd + 1) * VEXT_W]
            ktw = (cur.kT[hd * MLSTM_QK_DIM:(hd + 1) * MLSTM_QK_DIM, rs] * st["w_loc"][hd:hd + 1]).astype(BF16)
            u = jnp.dot(ktw, vext, preferred_element_type=F32)
            decay = jnp.concatenate([st["decay"][hd:hd + 1]] * 2, axis=1)
            scale = jnp.concatenate([st["scale"][hd:hd + 1]] * 2, axis=1)
            c_next[hd] = decay * c_state[hd] + scale * u

    def mlstm_out(c):
        rs = slice(c * CHUNK, (c + 1) * CHUNK)
        st = chunk_stats[c]
        for hd in range(MLSTM_HEADS):
            pair, odd = divmod(hd, 2)
            q2 = cur.q[rs, pair * PAIR_W:(pair + 1) * PAIR_W]
            vext = cur.vext[rs, hd * VEXT_W:(hd + 1) * VEXT_W]
            b_col = _rep(jnp.sum(jnp.where(tri, st["logf_r"][hd:hd + 1], 0.0), axis=1, keepdims=True),
                         tile_shape)
            log_d = jnp.where(tri, b_col + st["ib_r"][hd:hd + 1], -jnp.inf)
            inter_log = b_col + st["m_prev"][hd:hd + 1]
            m_j = jnp.maximum(inter_log, _rep(jnp.max(log_d, axis=1, keepdims=True), tile_shape))
            sqk = (s_qk.pop((c, hd)) * jnp.exp(log_d - m_j)).astype(BF16)
            c_bf = c_state[hd].astype(BF16)
            c_pad = jnp.concatenate([zeros_c, c_bf] if odd else [c_bf, zeros_c], axis=0)
            q_inter = (q2 * jnp.exp(inter_log - m_j)).astype(BF16)
            tot = jnp.dot(jnp.concatenate([sqk, q_inter], axis=1), jnp.concatenate([vext, c_pad], axis=0),
                          preferred_element_type=F32)
            num = tot[:, 0:MLSTM_V_DIM]
            den = tot[:, MLSTM_V_DIM:VEXT_W]
            hh = num / jnp.maximum(jnp.abs(den), jnp.exp(-m_j))
            vs = slice(hd * MLSTM_V_DIM, (hd + 1) * MLSTM_V_DIM)
            hn = hh * lax.rsqrt(jnp.mean(hh * hh, axis=1, keepdims=True) + NORM_EPS) * ong[:, vs]
            hm_scr[rs, vs] = (hn * cur.so[rs, vs]).astype(BF16)
            c_state[hd] = c_next[hd]

    si2 = lax.broadcasted_iota(jnp.int32, (2 * WINDOW, WINDOW), 0)
    qi2 = lax.broadcasted_iota(jnp.int32, (2 * WINDOW, WINDOW), 1)
    band = (si2 > qi2) & (si2 <= qi2 + WINDOW)
    band_first = band & ((si2 >= WINDOW) | (t_cur > 0))
    zero_q = jnp.zeros((ATTN_HEAD_DIM, group * WINDOW), BF16)
    ones_rows = jnp.ones((SUM_ROWS, 2 * WINDOW), BF16)
    attn = {}

    def attn_scores(nb):
        qs = slice(nb * WINDOW, (nb + 1) * WINDOW)
        mask = band_first if nb == 0 else band
        if nb == 0:
            k_win = jnp.concatenate([pk_scr[...], cur.kc[0:WINDOW, :]], axis=0)
        else:
            k_win = cur.kc[(nb - 1) * WINDOW:(nb + 1) * WINDOW, :]
        for kh in range(ATTN_KV_HEADS):
            q_t = jnp.concatenate([cur.qaT[(kh * group + j) * ATTN_HEAD_DIM:(kh * group + j + 1) * ATTN_HEAD_DIM, qs]
                                   for j in range(group)], axis=1)
            rhs = jnp.concatenate([q_t, zero_q] if kh == 0 else [zero_q, q_t], axis=0)
            s_t = jnp.dot(k_win, rhs, preferred_element_type=F32)
            p_cols, sk_cols = [], []
            for j in range(group):
                sink = sink_ref[kh * group + j]
                sc = jnp.where(mask, s_t[:, j * WINDOW:(j + 1) * WINDOW], -jnp.inf)
                mx = jnp.maximum(jnp.max(sc, axis=0, keepdims=True), sink)
                p_cols.append(jnp.exp(sc - mx).astype(BF16))
                sk_cols.append(jnp.exp(sink - mx))
            attn[nb, kh] = (jnp.concatenate(p_cols, axis=1), jnp.concatenate(sk_cols, axis=1))

    def attn_values(nb):
        qs = slice(nb * WINDOW, (nb + 1) * WINDOW)
        if nb == 0:
            v_win = jnp.concatenate([pvT_scr[...], cur.vcT[:, 0:WINDOW]], axis=1)
        else:
            v_win = cur.vcT[:, (nb - 1) * WINDOW:(nb + 1) * WINDOW]
        for kh in range(ATTN_KV_HEADS):
            probs_t, sink_term = attn.pop((nb, kh))
            lhs = jnp.concatenate([v_win[kh * ATTN_HEAD_DIM:(kh + 1) * ATTN_HEAD_DIM], ones_rows], axis=0)
            o_t = jnp.dot(lhs, probs_t, preferred_element_type=F32)
            on_t = o_t[0:ATTN_HEAD_DIM] / (o_t[ATTN_HEAD_DIM:ATTN_HEAD_DIM + 1] + sink_term)
            for j2 in range(group // 2):
                pair_t = jnp.concatenate([on_t[:, (2 * j2) * WINDOW:(2 * j2 + 1) * WINDOW],
                                          on_t[:, (2 * j2 + 1) * WINDOW:(2 * j2 + 2) * WINDOW]], axis=0)
                pair = kh * (group // 2) + j2
                oa_scr[qs, pair * LANES:(pair + 1) * LANES] = pair_t.T.astype(BF16)

    def branch_attn(idx):
        c0 = idx * MIX_COL_CHUNK
        y_a = jnp.dot(oa_scr[...], wbra_ref[:, c0:c0 + MIX_COL_CHUNK], preferred_element_type=F32)
        sga_scr[:, c0:c0 + MIX_COL_CHUNK] = sga_scr[:, c0:c0 + MIX_COL_CHUNK] * y_a

    gate_piece(0)
    p_norm()
    side_cast()
    attn_scores(0); mlstm_scores(0); mlstm_state(0)
    gate_piece(1)
    attn_values(0); mlstm_out(0)
    p_qk()
    attn_scores(1); mlstm_scores(1); mlstm_state(1)
    p_v()
    attn_values(1); mlstm_out(1)
    p_so()
    attn_scores(2); mlstm_scores(2); mlstm_state(2)
    gate_piece(2)
    attn_values(2); mlstm_out(2)
    p_qa()
    attn_scores(3); mlstm_scores(3); mlstm_state(3)
    gate_piece(3)
    attn_values(3)
    p_kva()
    mlstm_out(3)
    branch_attn(0); branch_attn(1)

    for hd in range(MLSTM_HEADS):
        c_scr[hd] = c_state[hd]
    m_scr[0:MLSTM_HEADS, :] = m_vec
    pk_scr[...] = cur.kc[rows - WINDOW:rows, :]
    pvT_scr[...] = cur.vcT[:, rows - WINDOW:rows]

    for c0, c1 in _col_chunks(D_MODEL, MIX_COL_CHUNK):
        y_m = jnp.dot(hm_scr[...], wbrm_ref[:, c0:c1], preferred_element_type=F32)
        mg_scr[:, c0:c1] = (sgm_scr[:, c0:c1] * y_m + sga_scr[:, c0:c1]).astype(BF16)
    o_ref[0] = cur.xs[...] + jnp.dot(mg_scr[...], wout_ref[...], preferred_element_type=F32)


def _mixer_body(*refs, tiles_per_seq, n_cast):
    n_in = N_MIXER_IN + n_cast
    n_out = 1 + n_cast
    n_slot = len(Slot._fields)
    ins = refs[:n_in]
    outs = refs[n_in:n_in + n_out]
    scr = refs[n_in + n_out:]
    slot_a, slot_b = Slot(*scr[:n_slot]), Slot(*scr[n_slot:2 * n_slot])
    shared = scr[2 * n_slot:]
    prev_blocks = shared[0:2]
    c_scr, m_scr = shared[-2:]
    s = pl.program_id(0)
    t_cur = lax.rem(s + tiles_per_seq - 1, tiles_per_seq)

    @pl.when(s == 0)
    def _():
        for ref in list(slot_b) + list(shared):
            ref[...] = jnp.zeros_like(ref)

    @pl.when(t_cur == 0)
    def _():
        c_scr[...] = jnp.zeros_like(c_scr)
        m_scr[...] = jnp.zeros_like(m_scr)
        for ref in prev_blocks:
            ref[...] = jnp.zeros_like(ref)

    even = lax.rem(s, 2) == 0

    @pl.when(even)
    def _():
        _mixer_step(t_cur, ins, slot_b, slot_a, shared, outs)

    @pl.when(jnp.logical_not(even))
    def _():
        _mixer_step(t_cur, ins, slot_a, slot_b, shared, outs)


def _cast_slabs(arr, steps):
    rows = arr.shape[0]
    slab = 16
    while rows % slab or rows // slab > steps:
        slab += 16
    n_slabs = rows // slab
    return pl.BlockSpec((slab, arr.shape[1]), lambda i: (jnp.minimum(i, n_slabs - 1), 0))


def _mixer(x, positions, norm_g, b_i, b_f, out_norm_g, sinks, weights, cast_weights):
    b, s, d = x.shape
    w_qk, w_v, w_o, w_qa, w_kva, w_gm, w_ga, w_brm, w_bra, w_outp = weights
    b_if = jnp.concatenate([b_i, b_f]).reshape(GATE_ROWS, 1)
    half = ROPE_DIM // 2
    inv = (ROPE_THETA ** (-jnp.arange(half, dtype=F32) * 2.0 / ROPE_DIM)).reshape(half, 1)

    rows = MIX_ROWS
    tiles_per_seq = s // rows
    n_tiles = b * tiles_per_seq
    def in_tile(i):
        j = jnp.minimum(i, n_tiles - 1)
        return j // tiles_per_seq, j % tiles_per_seq

    def out_tile(i):
        j = jnp.maximum(i - 1, 0)
        return j // tiles_per_seq, j % tiles_per_seq

    in_specs = [
        pl.BlockSpec((1, rows, d), lambda i: (*in_tile(i), 0)),
        pl.BlockSpec((1, 1, rows), lambda i: (in_tile(i)[0], 0, in_tile(i)[1])),
        _const_spec((1, d)), _const_spec((half, 1)), _const_spec((GATE_ROWS, 1)),
        _const_spec((1, MLSTM_V_W)),
        pl.BlockSpec(memory_space=pltpu.SMEM),
        _const_spec((d, QKG_W)), _const_spec((d, MLSTM_V_W)),
        _const_spec((d, MLSTM_V_W)), _const_spec((d, ATTN_Q_W)),
        _const_spec((d, 2 * ATTN_KV_W)), _const_spec((d, d)), _const_spec((d, d)),
        _const_spec((MLSTM_V_W, d)), _const_spec((ATTN_Q_W, d)), _const_spec((d, d)),
    ]
    slot = list(_slot_shapes(rows, d))
    shared = [
        pltpu.VMEM((WINDOW, LANES), BF16),
        pltpu.VMEM((LANES, WINDOW), BF16),
        pltpu.VMEM((rows, d), F32),
        pltpu.VMEM((rows, d), F32),
        pltpu.VMEM((rows, MLSTM_V_W), BF16),
        pltpu.VMEM((rows, ATTN_Q_W), BF16),
        pltpu.VMEM((rows, d), BF16),
        pltpu.VMEM((MLSTM_HEADS, MLSTM_QK_DIM, VEXT_W), F32),
        pltpu.VMEM((8, LANES), F32),
    ]
    assert len(in_specs) == N_MIXER_IN
    steps = n_tiles + 1
    cast_specs = [_cast_slabs(arr, steps) for arr in cast_weights]
    res = pl.pallas_call(
        functools.partial(_mixer_body, tiles_per_seq=tiles_per_seq, n_cast=len(cast_weights)),
        grid=(steps,),
        in_specs=in_specs + cast_specs,
        out_specs=[pl.BlockSpec((1, rows, d), lambda i: (*out_tile(i), 0))] + cast_specs,
        out_shape=[jax.ShapeDtypeStruct((b, s, d), F32)]
        + [jax.ShapeDtypeStruct(arr.shape, BF16) for arr in cast_weights],
        scratch_shapes=slot + slot + shared,
        compiler_params=pltpu.CompilerParams(
            dimension_semantics=("arbitrary",), vmem_limit_bytes=(7 * VMEM_BYTES_V7X) // 8),
        name="mixer",
    )(x, positions.reshape(b, 1, s), norm_g.reshape(1, d), inv, b_if, out_norm_g.reshape(1, MLSTM_V_W),
      sinks, w_qk, w_v, w_o, w_qa, w_kva, w_gm, w_ga, w_brm, w_bra, w_outp, *cast_weights)
    return res[0], res[1:]


def kernel(x, positions, ffn1_norm_g, ffn1_w_gate, ffn1_w_up, ffn1_w_down, mix_norm_g, w_in, mlstm_b_i, mlstm_b_f, mlstm_out_norm_g, attn_sinks, w_branch_mlstm, w_branch_attn, w_out, ffn2_norm_g, ffn2_w_gate, ffn2_w_up, ffn2_w_down, final_norm_g):
    b, s, d = x.shape
    depth = w_in.shape[0]
    for l in range(depth):
        last = l == depth - 1
        x2, mixer_w = _ffn(x.reshape(b * s, d), ffn1_norm_g[l], ffn1_w_gate[l].astype(BF16),
                           ffn1_w_up[l].astype(BF16), ffn1_w_down[l].astype(BF16),
                           mixer_weights=(w_in[l], w_branch_mlstm[l], w_branch_attn[l], w_out[l]))
        x3, ffn2_w = _mixer(x2.reshape(b, s, d), positions, mix_norm_g[l], mlstm_b_i[l], mlstm_b_f[l],
                            mlstm_out_norm_g[l], attn_sinks[l], mixer_w,
                            (ffn2_w_gate[l], ffn2_w_up[l], ffn2_w_down[l]))
        x = _ffn(x3.reshape(b * s, d), ffn2_norm_g[l], *ffn2_w,
                 final_g=final_norm_g if last else None).reshape(b, s, d)
    return x
```

```python
import collections
import functools

import jax
import jax.numpy as jnp
from jax import lax
from jax.experimental import pallas as pl
from jax.experimental.pallas import tpu as pltpu

D_MODEL = 1024
D_FF = 2816
MLSTM_HEADS = 4
MLSTM_QK_DIM = 64
MLSTM_V_DIM = 128
GATE_SOFTCAP = 15.0
ATTN_Q_HEADS = 8
ATTN_KV_HEADS = 2
ATTN_HEAD_DIM = 64
WINDOW = 128
ROPE_DIM = ATTN_HEAD_DIM // 4
ROPE_THETA = 500000.0
NORM_EPS = 1e-6

MLSTM_QK_W = MLSTM_HEADS * MLSTM_QK_DIM
MLSTM_V_W = MLSTM_HEADS * MLSTM_V_DIM
ATTN_Q_W = ATTN_Q_HEADS * ATTN_HEAD_DIM
ATTN_KV_W = ATTN_KV_HEADS * ATTN_HEAD_DIM
IN_WIDTHS = (MLSTM_QK_W, MLSTM_QK_W, MLSTM_V_W, MLSTM_V_W, MLSTM_HEADS, MLSTM_HEADS,
             ATTN_Q_W, ATTN_KV_W, ATTN_KV_W, D_MODEL, D_MODEL)

LANES = 128
MXU_COLS = 256
VMEM_BYTES_V7X = 64 * 1024 * 1024

FFN_ROWS = 1024
FFN_COL_CHUNK = 2 * MXU_COLS
MIX_ROWS = 512
MIX_COL_CHUNK = 2 * MXU_COLS
CHUNK = LANES
PAIR_W = 2 * MLSTM_QK_DIM
VEXT_W = 2 * MLSTM_V_DIM
SUM_ROWS = 16
GATE_ROWS = 2 * MLSTM_HEADS
QKG_W = 2 * MLSTM_QK_W + LANES
N_MIXER_IN = 9
MIX_PIECES = (("qkg", QKG_W), ("v", MLSTM_V_W), ("o", MLSTM_V_W), ("qa", ATTN_Q_W), ("kva", 2 * ATTN_KV_W),
              ("gm", D_MODEL), ("ga", D_MODEL), ("out", D_MODEL))
MIX_W = sum(w for _, w in MIX_PIECES)

F32 = jnp.float32
BF16 = jnp.bfloat16


def _rms(x, g):
    return x * lax.rsqrt(jnp.mean(x * x, axis=-1, keepdims=True) + NORM_EPS) * g


def _col_chunks(total, width):
    return [(c, min(c + width, total)) for c in range(0, total, width)]


def _const_spec(shape):
    return pl.BlockSpec(shape, lambda *_: (0,) * len(shape), pipeline_mode=pl.Buffered(1))


def _mix_piece(ref, name):
    start = 0
    for piece, width in MIX_PIECES:
        if piece == name:
            return ref.at[:, start:start + width]
        start += width
    raise KeyError(name)


def _prep_mixer_weights(win_ref, wbrm_ref, wbra_ref, wout_ref, wmix_o, wbr_o):
    starts = [0]
    for w in IN_WIDTHS:
        starts.append(starts[-1] + w)
    col = lambda i: win_ref[:, starts[i]:starts[i + 1]]
    gate_pad = jnp.zeros((win_ref.shape[0], LANES - GATE_ROWS), F32)
    pieces = dict(
        qkg=jnp.concatenate([col(0) * (MLSTM_QK_DIM ** -0.5), col(1), col(4), col(5), gate_pad], axis=1),
        v=col(2), o=col(3), qa=col(6) * (ATTN_HEAD_DIM ** -0.5),
        kva=jnp.concatenate([col(7), col(8)], axis=1), gm=col(9), ga=col(10), out=wout_ref[...])
    for name, _ in MIX_PIECES:
        _mix_piece(wmix_o, name)[...] = pieces[name].astype(BF16)
    wbr_o[:, 0:D_MODEL] = wbrm_ref[...].astype(BF16)
    wbr_o[:, D_MODEL:2 * D_MODEL] = wbra_ref[...].astype(BF16)


def _ffn_body(*refs, final_norm, prep):
    n_in = 5 + int(final_norm) + (4 if prep else 0)
    x_ref, g_ref, wg_ref, wu_ref, wd_ref = refs[:5]
    o_ref = refs[n_in]
    act_ref = refs[-1]
    x = x_ref[...]
    h = _rms(x, g_ref[...]).astype(BF16)
    for c0, c1 in _col_chunks(D_FF, FFN_COL_CHUNK):
        g = jnp.dot(h, wg_ref[:, c0:c1], preferred_element_type=F32)
        u = jnp.dot(h, wu_ref[:, c0:c1], preferred_element_type=F32)
        act_ref[:, c0:c1] = (g * jax.nn.sigmoid(g) * u).astype(BF16)
    if prep:
        _prep_mixer_weights(*refs[n_in - 4:n_in], *refs[n_in + 1:-1])
    r = x + 0.5 * jnp.dot(act_ref[...], wd_ref[...], preferred_element_type=F32)
    if final_norm:
        r = _rms(r, refs[5][...])
    o_ref[...] = r


def _ffn(x2, norm_g, w_gate, w_up, w_down, final_g=None, mixer_weights=None):
    n, d = x2.shape
    assert n % FFN_ROWS == 0 and d == D_MODEL and w_gate.shape == (d, D_FF) and w_down.shape == (D_FF, d)
    steps = n // FFN_ROWS
    final_norm = final_g is not None
    prep = mixer_weights is not None
    row_spec = pl.BlockSpec((FFN_ROWS, d), lambda i: (i, 0))
    in_specs = [row_spec, _const_spec((1, d)), _const_spec((d, D_FF)), _const_spec((d, D_FF)),
                _const_spec((D_FF, d))]
    args = [x2, norm_g.reshape(1, d), w_gate, w_up, w_down]
    out_specs = [row_spec]
    out_shape = [jax.ShapeDtypeStruct((n, d), F32)]
    if final_norm:
        in_specs.append(_const_spec((1, d)))
        args.append(final_g.reshape(1, d))
    if prep:
        slab = lambda arr: pl.BlockSpec((arr.shape[0] // steps, arr.shape[1]), lambda i: (i, 0))
        for arr in mixer_weights:
            in_specs.append(slab(arr))
            args.append(arr)
        packed = [jax.ShapeDtypeStruct((d, MIX_W), BF16), jax.ShapeDtypeStruct((MLSTM_V_W, 2 * d), BF16)]
        out_shape += packed
        out_specs += [slab(p) for p in packed]
    res = pl.pallas_call(
        functools.partial(_ffn_body, final_norm=final_norm, prep=prep),
        grid=(steps,),
        in_specs=in_specs,
        out_specs=out_specs,
        out_shape=out_shape,
        scratch_shapes=[pltpu.VMEM((FFN_ROWS, D_FF), BF16)],
        compiler_params=pltpu.CompilerParams(
            dimension_semantics=("arbitrary",), vmem_limit_bytes=(3 * VMEM_BYTES_V7X) // 4),
        name="ffn_final" if final_norm else "ffn",
    )(*args)
    return (res[0], res[1:]) if prep else res[0]


Slot = collections.namedtuple("Slot", "xs h q kT vext so qaT kc vcT g")


def _slot_shapes(rows, d):
    return Slot(
        xs=pltpu.VMEM((rows, d), F32),
        h=pltpu.VMEM((rows, d), BF16),
        q=pltpu.VMEM((rows, MLSTM_QK_W), BF16),
        kT=pltpu.VMEM((MLSTM_QK_W, rows), F32),
        vext=pltpu.VMEM((rows, MLSTM_HEADS * VEXT_W), BF16),
        so=pltpu.VMEM((rows, MLSTM_V_W), F32),
        qaT=pltpu.VMEM((ATTN_Q_W, rows), BF16),
        kc=pltpu.VMEM((rows, LANES), BF16),
        vcT=pltpu.VMEM((LANES, rows), BF16),
        g=pltpu.VMEM((GATE_ROWS, rows), F32),
    )


def _softcap(a):
    return GATE_SOFTCAP * jnp.tanh(a / GATE_SOFTCAP)


def _log_sigmoid(a):
    return jnp.minimum(a, 0.0) - jnp.log1p(jnp.exp(-jnp.abs(a)))


def _lane_cumsum(a):
    lane = lax.broadcasted_iota(jnp.int32, a.shape, 1)
    d = 1
    while d < LANES:
        a = a + jnp.where(lane >= d, pltpu.roll(a, d, 1), 0.0)
        d *= 2
    return a


def _rope_tables(pos_row, inv_col):
    ang = inv_col * pos_row
    c, s = jnp.cos(ang), jnp.sin(ang)
    one, zero = jnp.ones_like(c), jnp.zeros_like(c)
    per_head = ATTN_HEAD_DIM // 8

    def tile(first, second, rest):
        rows = []
        for _ in range(LANES // ATTN_HEAD_DIM):
            rows += [first, second] + [rest] * (per_head - 2)
        return jnp.concatenate(rows, axis=0).T
    return tile(c, c, one), tile(zero, s, zero), tile(-s, zero, zero)


def _rope(a, tables):
    cos_t, sin_up, sin_dn = tables
    half = ROPE_DIM // 2
    return a * cos_t + pltpu.roll(a, half, 1) * sin_up + pltpu.roll(a, LANES - half, 1) * sin_dn


def _rep(a, shape):
    return jnp.broadcast_to(a, shape)


def _mixer_step(t_cur, ins, cur, nxt, shared, outs):
    x_ref, pos_ref, ng_ref, inv_ref, bif_ref, ong_ref, sink_ref, wmix_ref, wbr_ref = ins[:N_MIXER_IN]
    wqk_ref, wv_ref, wo_ref, wqa_ref, wkva_ref, wgm_ref, wga_ref, wout_ref = (
        _mix_piece(wmix_ref, name) for name, _ in MIX_PIECES)
    wbrm_ref, wbra_ref = wbr_ref.at[:, 0:D_MODEL], wbr_ref.at[:, D_MODEL:2 * D_MODEL]
    cast_in = ins[N_MIXER_IN:]
    o_ref, cast_out = outs[0], outs[1:]
    pk_scr, pvT_scr, sgm_scr, sga_scr, hm_scr, oa_scr, mg_scr, c_scr, m_scr = shared
    rows = MIX_ROWS
    n_chunks = rows // CHUNK
    n_blocks = rows // WINDOW
    group = ATTN_Q_HEADS // ATTN_KV_HEADS

    shared_vals = {}

    def p_norm():
        x_new = x_ref[0]
        nxt.xs[...] = x_new
        nxt.h[...] = _rms(x_new, ng_ref[...]).astype(BF16)

    def side_cast():
        for src, dst in zip(cast_in, cast_out):
            dst[...] = src[...].astype(BF16)

    def p_qk():
        qkg = jnp.dot(nxt.h[...], wqk_ref[...], preferred_element_type=F32)
        nxt.q[...] = qkg[:, 0:MLSTM_QK_W].astype(BF16)
        nxt.kT[...] = qkg[:, MLSTM_QK_W:2 * MLSTM_QK_W].T
        gates_t = qkg[:, 2 * MLSTM_QK_W:QKG_W].T
        nxt.g[...] = _softcap(gates_t[0:GATE_ROWS] + bif_ref[...])

    def p_v():
        v = jnp.dot(nxt.h[...], wv_ref[...], preferred_element_type=F32)
        for hd in range(MLSTM_HEADS):
            nxt.vext[:, hd * VEXT_W:hd * VEXT_W + MLSTM_V_DIM] = (
                v[:, hd * MLSTM_V_DIM:(hd + 1) * MLSTM_V_DIM].astype(BF16))
            nxt.vext[:, hd * VEXT_W + MLSTM_V_DIM:(hd + 1) * VEXT_W] = jnp.ones((rows, MLSTM_V_DIM), BF16)

    def p_so():
        nxt.so[...] = jax.nn.sigmoid(jnp.dot(nxt.h[...], wo_ref[...], preferred_element_type=F32))

    def p_qa():
        shared_vals["tables"] = _rope_tables(pos_ref[0].astype(F32), inv_ref[...])
        qa = jnp.dot(nxt.h[...], wqa_ref[...], preferred_element_type=F32)
        for p in range(ATTN_Q_W // LANES):
            nxt.qaT[p * LANES:(p + 1) * LANES, :] = _rope(
                qa[:, p * LANES:(p + 1) * LANES], shared_vals["tables"]).T.astype(BF16)

    def p_kva():
        kva = jnp.dot(nxt.h[...], wkva_ref[...], preferred_element_type=F32)
        ka = _rope(kva[:, 0:ATTN_KV_W], shared_vals["tables"])
        va = kva[:, ATTN_KV_W:2 * ATTN_KV_W]
        nxt.kc[...] = ka.astype(BF16)
        nxt.vcT[...] = va.T.astype(BF16)

    def gate_piece(idx):
        gref, wref = (sgm_scr, wgm_ref) if idx < 2 else (sga_scr, wga_ref)
        c0 = (idx % 2) * MIX_COL_CHUNK
        gref[:, c0:c0 + MIX_COL_CHUNK] = jax.nn.sigmoid(
            jnp.dot(cur.h[...], wref[:, c0:c0 + MIX_COL_CHUNK], preferred_element_type=F32))

    qi = lax.broadcasted_iota(jnp.int32, (CHUNK, CHUNK), 0)
    si = lax.broadcasted_iota(jnp.int32, (CHUNK, CHUNK), 1)
    tri = si <= qi
    zeros_c = jnp.zeros((MLSTM_QK_DIM, VEXT_W), BF16)
    ong = ong_ref[...]
    tile_shape = (CHUNK, CHUNK)
    rep_shape = (MLSTM_HEADS, LANES)

    chunk_stats = []
    m_vec = m_scr[0:MLSTM_HEADS, :]
    for c in range(n_chunks):
        g_c = cur.g[:, c * CHUNK:(c + 1) * CHUNK]
        i_r = g_c[0:MLSTM_HEADS]
        logf_r = _log_sigmoid(g_c[MLSTM_HEADS:GATE_ROWS])
        b_r = _lane_cumsum(logf_r)
        b_last = _rep(b_r[:, CHUNK - 1:CHUNK], rep_shape)
        a_r = b_last - b_r + i_r
        a_max = _rep(jnp.max(a_r, axis=1, keepdims=True), rep_shape)
        m_new = jnp.maximum(b_last + m_vec, a_max)
        chunk_stats.append(dict(
            logf_r=logf_r, ib_r=i_r - b_r, m_prev=m_vec,
            decay=jnp.exp(b_last + m_vec - m_new), scale=jnp.exp(a_max - m_new),
            w_loc=jnp.exp(a_r - a_max)))
        m_vec = m_new
    c_state = [c_scr[hd] for hd in range(MLSTM_HEADS)]
    c_next = [None] * MLSTM_HEADS
    s_qk = {}

    def mlstm_scores(c):
        rs = slice(c * CHUNK, (c + 1) * CHUNK)
        for pair in range(MLSTM_HEADS // 2):
            q2 = cur.q[rs, pair * PAIR_W:(pair + 1) * PAIR_W]
            k_even, k_odd = (cur.kT[hd * MLSTM_QK_DIM:(hd + 1) * MLSTM_QK_DIM, rs].astype(BF16)
                             for hd in (2 * pair, 2 * pair + 1))
            zk = jnp.zeros_like(k_even)
            k_both = jnp.concatenate([jnp.concatenate([k_even, zk], axis=1),
                                      jnp.concatenate([zk, k_odd], axis=1)], axis=0)
            s2 = jnp.dot(q2, k_both, preferred_element_type=F32)
            s_qk[c, 2 * pair] = s2[:, 0:CHUNK]
            s_qk[c, 2 * pair + 1] = s2[:, CHUNK:2 * CHUNK]

    def mlstm_state(c):
        rs = slice(c * CHUNK, (c + 1) * CHUNK)
        st = chunk_stats[c]
        for hd in range(MLSTM_HEADS):
            vext = cur.vext[rs, hd * VEXT_W:(hd + 1) * VEXT_W]
            ktw = (cur.kT[hd * MLSTM_QK_DIM:(hd + 1) * MLSTM_QK_DIM, rs] * st["w_loc"][hd:hd + 1]).astype(BF16)
            u = jnp.dot(ktw, vext, preferred_element_type=F32)
            decay = jnp.concatenate([st["decay"][hd:hd + 1]] * 2, axis=1)
            scale = jnp.concatenate([st["scale"][hd:hd + 1]] * 2, axis=1)
            c_next[hd] = decay * c_state[hd] + scale * u

    def mlstm_out(c):
        rs = slice(c * CHUNK, (c + 1) * CHUNK)
        st = chunk_stats[c]
        for hd in range(MLSTM_HEADS):
            pair, odd = divmod(hd, 2)
            q2 = cur.q[rs, pair * PAIR_W:(pair + 1) * PAIR_W]
            vext = cur.vext[rs, hd * VEXT_W:(hd + 1) * VEXT_W]
            b_col = _rep(jnp.sum(jnp.where(tri, st["logf_r"][hd:hd + 1], 0.0), axis=1, keepdims=True),
                         tile_shape)
            log_d = jnp.where(tri, b_col + st["ib_r"][hd:hd + 1], -jnp.inf)
            inter_log = b_col + st["m_prev"][hd:hd + 1]
            m_j = jnp.maximum(inter_log, _rep(jnp.max(log_d, axis=1, keepdims=True), tile_shape))
            sqk = (s_qk.pop((c, hd)) * jnp.exp(log_d - m_j)).astype(BF16)
            c_bf = c_state[hd].astype(BF16)
            c_pad = jnp.concatenate([zeros_c, c_bf] if odd else [c_bf, zeros_c], axis=0)
            q_inter = (q2 * jnp.exp(inter_log - m_j)).astype(BF16)
            tot = jnp.dot(jnp.concatenate([sqk, q_inter], axis=1), jnp.concatenate([vext, c_pad], axis=0),
                          preferred_element_type=F32)
            num = tot[:, 0:MLSTM_V_DIM]
            den = tot[:, MLSTM_V_DIM:VEXT_W]
            hh = num / jnp.maximum(jnp.abs(den), jnp.exp(-m_j))
            vs = slice(hd * MLSTM_V_DIM, (hd + 1) * MLSTM_V_DIM)
            hn = hh * lax.rsqrt(jnp.mean(hh * hh, axis=1, keepdims=True) + NORM_EPS) * ong[:, vs]
            hm_scr[rs, vs] = (hn * cur.so[rs, vs]).astype(BF16)
            c_state[hd] = c_next[hd]

    si2 = lax.broadcasted_iota(jnp.int32, (2 * WINDOW, WINDOW), 0)
    qi2 = lax.broadcasted_iota(jnp.int32, (2 * WINDOW, WINDOW), 1)
    band = (si2 > qi2) & (si2 <= qi2 + WINDOW)
    band_first = band & ((si2 >= WINDOW) | (t_cur > 0))
    zero_q = jnp.zeros((ATTN_HEAD_DIM, group * WINDOW), BF16)
    ones_rows = jnp.ones((SUM_ROWS, 2 * WINDOW), BF16)
    attn = {}

    def attn_scores(nb):
        qs = slice(nb * WINDOW, (nb + 1) * WINDOW)
        mask = band_first if nb == 0 else band
        if nb == 0:
            k_win = jnp.concatenate([pk_scr[...], cur.kc[0:WINDOW, :]], axis=0)
        else:
            k_win = cur.kc[(nb - 1) * WINDOW:(nb + 1) * WINDOW, :]
        for kh in range(ATTN_KV_HEADS):
            q_t = jnp.concatenate([cur.qaT[(kh * group + j) * ATTN_HEAD_DIM:(kh * group + j + 1) * ATTN_HEAD_DIM, qs]
                                   for j in range(group)], axis=1)
            rhs = jnp.concatenate([q_t, zero_q] if kh == 0 else [zero_q, q_t], axis=0)
            s_t = jnp.dot(k_win, rhs, preferred_element_type=F32)
            p_cols, sk_cols = [], []
            for j in range(group):
                sink = sink_ref[kh * group + j]
                sc = jnp.where(mask, s_t[:, j * WINDOW:(j + 1) * WINDOW], -jnp.inf)
                mx = jnp.maximum(jnp.max(sc, axis=0, keepdims=True), sink)
                p_cols.append(jnp.exp(sc - mx).astype(BF16))
                sk_cols.append(jnp.exp(sink - mx))
            attn[nb, kh] = (jnp.concatenate(p_cols, axis=1), jnp.concatenate(sk_cols, axis=1))

    def attn_values(nb):
        qs = slice(nb * WINDOW, (nb + 1) * WINDOW)
        if nb == 0:
            v_win = jnp.concatenate([pvT_scr[...], cur.vcT[:, 0:WINDOW]], axis=1)
        else:
            v_win = cur.vcT[:, (nb - 1) * WINDOW:(nb + 1) * WINDOW]
        for kh in range(ATTN_KV_HEADS):
            probs_t, sink_term = attn.pop((nb, kh))
            lhs = jnp.concatenate([v_win[kh * ATTN_HEAD_DIM:(kh + 1) * ATTN_HEAD_DIM], ones_rows], axis=0)
            o_t = jnp.dot(lhs, probs_t, preferred_element_type=F32)
            on_t = o_t[0:ATTN_HEAD_DIM] / (o_t[ATTN_HEAD_DIM:ATTN_HEAD_DIM + 1] + sink_term)
            for j2 in range(group // 2):
                pair_t = jnp.concatenate([on_t[:, (2 * j2) * WINDOW:(2 * j2 + 1) * WINDOW],
                                          on_t[:, (2 * j2 + 1) * WINDOW:(2 * j2 + 2) * WINDOW]], axis=0)
                pair = kh * (group // 2) + j2
                oa_scr[qs, pair * LANES:(pair + 1) * LANES] = pair_t.T.astype(BF16)

    def branch_attn(idx):
        c0 = idx * MIX_COL_CHUNK
        y_a = jnp.dot(oa_scr[...], wbra_ref[:, c0:c0 + MIX_COL_CHUNK], preferred_element_type=F32)
        sga_scr[:, c0:c0 + MIX_COL_CHUNK] = sga_scr[:, c0:c0 + MIX_COL_CHUNK] * y_a

    gate_piece(0)
    p_norm()
    side_cast()
    attn_scores(0); mlstm_scores(0); mlstm_state(0)
    gate_piece(1)
    attn_values(0); mlstm_out(0)
    p_qk()
    attn_scores(1); mlstm_scores(1); mlstm_state(1)
    p_v()
    attn_values(1); mlstm_out(1)
    p_so()
    attn_scores(2); mlstm_scores(2); mlstm_state(2)
    gate_piece(2)
    attn_values(2); mlstm_out(2)
    p_qa()
    attn_scores(3); mlstm_scores(3); mlstm_state(3)
    gate_piece(3)
    attn_values(3)
    p_kva()
    mlstm_out(3)
    branch_attn(0); branch_attn(1)

    for hd in range(MLSTM_HEADS):
        c_scr[hd] = c_state[hd]
    m_scr[0:MLSTM_HEADS, :] = m_vec
    pk_scr[...] = cur.kc[rows - WINDOW:rows, :]
    pvT_scr[...] = cur.vcT[:, rows - WINDOW:rows]

    for c0, c1 in _col_chunks(D_MODEL, MIX_COL_CHUNK):
        y_m = jnp.dot(hm_scr[...], wbrm_ref[:, c0:c1], preferred_element_type=F32)
        mg_scr[:, c0:c1] = (sgm_scr[:, c0:c1] * y_m + sga_scr[:, c0:c1]).astype(BF16)
    o_ref[0] = cur.xs[...] + jnp.dot(mg_scr[...], wout_ref[...], preferred_element_type=F32)


def _mixer_body(*refs, tiles_per_seq, n_cast):
    n_in = N_MIXER_IN + n_cast
    n_out = 1 + n_cast
    n_slot = len(Slot._fields)
    ins = refs[:n_in]
    outs = refs[n_in:n_in + n_out]
    scr = refs[n_in + n_out:]
    slot_a, slot_b = Slot(*scr[:n_slot]), Slot(*scr[n_slot:2 * n_slot])
    shared = scr[2 * n_slot:]
    prev_blocks = shared[0:2]
    c_scr, m_scr = shared[-2:]
    s = pl.program_id(0)
    t_cur = lax.rem(s + tiles_per_seq - 1, tiles_per_seq)

    @pl.when(s == 0)
    def _():
        for ref in list(slot_b) + list(shared):
            ref[...] = jnp.zeros_like(ref)

    @pl.when(t_cur == 0)
    def _():
        c_scr[...] = jnp.zeros_like(c_scr)
        m_scr[...] = jnp.zeros_like(m_scr)
        for ref in prev_blocks:
            ref[...] = jnp.zeros_like(ref)

    even = lax.rem(s, 2) == 0

    @pl.when(even)
    def _():
        _mixer_step(t_cur, ins, slot_b, slot_a, shared, outs)

    @pl.when(jnp.logical_not(even))
    def _():
        _mixer_step(t_cur, ins, slot_a, slot_b, shared, outs)


def _cast_slabs(arr, steps):
    rows = arr.shape[0]
    slab = 16
    while rows % slab or rows // slab > steps:
        slab += 16
    n_slabs = rows // slab
    return pl.BlockSpec((slab, arr.shape[1]), lambda i: (jnp.minimum(i, n_slabs - 1), 0))


def _mixer(x, positions, norm_g, b_i, b_f, out_norm_g, sinks, weights, cast_weights):
    b, s, d = x.shape
    assert s % MIX_ROWS == 0 and d == D_MODEL and positions.shape == (b, s)
    w_mix, w_br = weights
    b_if = jnp.concatenate([b_i, b_f]).reshape(GATE_ROWS, 1)
    half = ROPE_DIM // 2
    inv = (ROPE_THETA ** (-jnp.arange(half, dtype=F32) * 2.0 / ROPE_DIM)).reshape(half, 1)

    rows = MIX_ROWS
    tiles_per_seq = s // rows
    n_tiles = b * tiles_per_seq
    def in_tile(i):
        j = jnp.minimum(i, n_tiles - 1)
        return j // tiles_per_seq, j % tiles_per_seq

    def out_tile(i):
        j = jnp.maximum(i - 1, 0)
        return j // tiles_per_seq, j % tiles_per_seq

    in_specs = [
        pl.BlockSpec((1, rows, d), lambda i: (*in_tile(i), 0)),
        pl.BlockSpec((1, 1, rows), lambda i: (in_tile(i)[0], 0, in_tile(i)[1])),
        _const_spec((1, d)), _const_spec((half, 1)), _const_spec((GATE_ROWS, 1)),
        _const_spec((1, MLSTM_V_W)),
        pl.BlockSpec(memory_space=pltpu.SMEM),
        _const_spec((d, MIX_W)), _const_spec((MLSTM_V_W, 2 * d)),
    ]
    slot = list(_slot_shapes(rows, d))
    shared = [
        pltpu.VMEM((WINDOW, LANES), BF16),
        pltpu.VMEM((LANES, WINDOW), BF16),
        pltpu.VMEM((rows, d), F32),
        pltpu.VMEM((rows, d), F32),
        pltpu.VMEM((rows, MLSTM_V_W), BF16),
        pltpu.VMEM((rows, ATTN_Q_W), BF16),
        pltpu.VMEM((rows, d), BF16),
        pltpu.VMEM((MLSTM_HEADS, MLSTM_QK_DIM, VEXT_W), F32),
        pltpu.VMEM((8, LANES), F32),
    ]
    assert len(in_specs) == N_MIXER_IN
    steps = n_tiles + 1
    cast_specs = [_cast_slabs(arr, steps) for arr in cast_weights]
    res = pl.pallas_call(
        functools.partial(_mixer_body, tiles_per_seq=tiles_per_seq, n_cast=len(cast_weights)),
        grid=(steps,),
        in_specs=in_specs + cast_specs,
        out_specs=[pl.BlockSpec((1, rows, d), lambda i: (*out_tile(i), 0))] + cast_specs,
        out_shape=[jax.ShapeDtypeStruct((b, s, d), F32)]
        + [jax.ShapeDtypeStruct(arr.shape, BF16) for arr in cast_weights],
        scratch_shapes=slot + slot + shared,
        compiler_params=pltpu.CompilerParams(
            dimension_semantics=("arbitrary",), vmem_limit_bytes=(7 * VMEM_BYTES_V7X) // 8),
        name="mixer",
    )(x, positions.reshape(b, 1, s), norm_g.reshape(1, d), inv, b_if, out_norm_g.reshape(1, MLSTM_V_W),
      sinks, w_mix, w_br, *cast_weights)
    return res[0], res[1:]


def kernel(x, positions, ffn1_norm_g, ffn1_w_gate, ffn1_w_up, ffn1_w_down, mix_norm_g, w_in, mlstm_b_i, mlstm_b_f, mlstm_out_norm_g, attn_sinks, w_branch_mlstm, w_branch_attn, w_out, ffn2_norm_g, ffn2_w_gate, ffn2_w_up, ffn2_w_down, final_norm_g):
    b, s, d = x.shape
    depth = w_in.shape[0]
    for l in range(depth):
        last = l == depth - 1
        x2, mixer_w = _ffn(x.reshape(b * s, d), ffn1_norm_g[l], ffn1_w_gate[l].astype(BF16),
                           ffn1_w_up[l].astype(BF16), ffn1_w_down[l].astype(BF16),
                           mixer_weights=(w_in[l], w_branch_mlstm[l], w_branch_attn[l], w_out[l]))
        x3, ffn2_w = _mixer(x2.reshape(b, s, d), positions, mix_norm_g[l], mlstm_b_i[l], mlstm_b_f[l],
                            mlstm_out_norm_g[l], attn_sinks[l], mixer_w,
                            (ffn2_w_gate[l], ffn2_w_up[l], ffn2_w_down[l]))
        x = _ffn(x3.reshape(b * s, d), ffn2_norm_g[l], *ffn2_w,
                 final_g=final_norm_g if last else None).reshape(b, s, d)
    return x
```

```python
import collections
import functools

import jax
import jax.numpy as jnp
from jax import lax
from jax.experimental import pallas as pl
from jax.experimental.pallas import tpu as pltpu

D_MODEL = 1024
D_FF = 2816
MLSTM_HEADS = 4
MLSTM_QK_DIM = 64
MLSTM_V_DIM = 128
GATE_SOFTCAP = 15.0
ATTN_Q_HEADS = 8
ATTN_KV_HEADS = 2
ATTN_HEAD_DIM = 64
WINDOW = 128
ROPE_DIM = ATTN_HEAD_DIM // 4
ROPE_THETA = 500000.0
NORM_EPS = 1e-6

MLSTM_QK_W = MLSTM_HEADS * MLSTM_QK_DIM
MLSTM_V_W = MLSTM_HEADS * MLSTM_V_DIM
ATTN_Q_W = ATTN_Q_HEADS * ATTN_HEAD_DIM
ATTN_KV_W = ATTN_KV_HEADS * ATTN_HEAD_DIM
IN_WIDTHS = (MLSTM_QK_W, MLSTM_QK_W, MLSTM_V_W, MLSTM_V_W, MLSTM_HEADS, MLSTM_HEADS,
             ATTN_Q_W, ATTN_KV_W, ATTN_KV_W, D_MODEL, D_MODEL)

LANES = 128
MXU_COLS = 256
VMEM_BYTES_V7X = 64 * 1024 * 1024

FFN_ROWS = 1024
FFN_COL_CHUNK = 2 * MXU_COLS
MIX_ROWS = 512
MIX_COL_CHUNK = 2 * MXU_COLS
CHUNK = LANES
PAIR_W = 2 * MLSTM_QK_DIM
VEXT_W = 2 * MLSTM_V_DIM
SUM_ROWS = 16
GATE_ROWS = 2 * MLSTM_HEADS
QKG_W = 2 * MLSTM_QK_W + LANES
N_MIXER_IN = 9
MIX_PIECES = (("qkg", QKG_W), ("v", MLSTM_V_W), ("o", MLSTM_V_W), ("qa", ATTN_Q_W), ("kva", 2 * ATTN_KV_W),
              ("gm", D_MODEL), ("ga", D_MODEL), ("out", D_MODEL))
MIX_W = sum(w for _, w in MIX_PIECES)
PREP_STEPS = D_MODEL // LANES

F32 = jnp.float32
BF16 = jnp.bfloat16


def _rms(x, g):
    return x * lax.rsqrt(jnp.mean(x * x, axis=-1, keepdims=True) + NORM_EPS) * g


def _col_chunks(total, width):
    return [(c, min(c + width, total)) for c in range(0, total, width)]


def _const_spec(shape):
    return pl.BlockSpec(shape, lambda *_: (0,) * len(shape), pipeline_mode=pl.Buffered(1))


def _mix_piece(ref, name):
    start = 0
    for piece, width in MIX_PIECES:
        if piece == name:
            return ref.at[:, start:start + width]
        start += width
    raise KeyError(name)


def _prep_mixer_weights(win_t_ref, wbrm_ref, wbra_ref, wout_ref, wmix_o, wbr_o):
    starts = [0]
    for w in IN_WIDTHS:
        starts.append(starts[-1] + w)
    col = lambda i: win_t_ref[starts[i]:starts[i + 1], :].T
    gates = win_t_ref[starts[4]:starts[4] + LANES, :].T
    gates = jnp.where(lax.broadcasted_iota(jnp.int32, gates.shape, 1) < GATE_ROWS, gates, 0.0)
    pieces = dict(
        qkg=jnp.concatenate([col(0) * (MLSTM_QK_DIM ** -0.5), col(1), gates], axis=1),
        v=col(2), o=col(3), qa=col(6) * (ATTN_HEAD_DIM ** -0.5),
        kva=jnp.concatenate([col(7), col(8)], axis=1), gm=col(9), ga=col(10), out=wout_ref[...])
    for name, _ in MIX_PIECES:
        _mix_piece(wmix_o, name)[...] = pieces[name].astype(BF16)
    wbr_o[:, 0:D_MODEL] = wbrm_ref[...].astype(BF16)
    wbr_o[:, D_MODEL:2 * D_MODEL] = wbra_ref[...].astype(BF16)


def _ffn_body(*refs, final_norm, prep):
    n_in = 5 + int(final_norm) + (4 if prep else 0)
    x_ref, g_ref, wg_ref, wu_ref, wd_ref = refs[:5]
    o_ref = refs[n_in]
    act_ref = refs[-1]
    if prep:
        @pl.when(pl.program_id(0) < PREP_STEPS)
        def _():
            _prep_mixer_weights(*refs[n_in - 4:n_in], *refs[n_in + 1:-1])
    x = x_ref[...]
    h = _rms(x, g_ref[...]).astype(BF16)
    for c0, c1 in _col_chunks(D_FF, FFN_COL_CHUNK):
        g = jnp.dot(h, wg_ref[:, c0:c1], preferred_element_type=F32)
        u = jnp.dot(h, wu_ref[:, c0:c1], preferred_element_type=F32)
        act_ref[:, c0:c1] = (g * jax.nn.sigmoid(g) * u).astype(BF16)
    r = x + 0.5 * jnp.dot(act_ref[...], wd_ref[...], preferred_element_type=F32)
    if final_norm:
        r = _rms(r, refs[5][...])
    o_ref[...] = r


def _ffn(x2, norm_g, w_gate, w_up, w_down, final_g=None, mixer_weights=None):
    n, d = x2.shape
    assert n % FFN_ROWS == 0 and d == D_MODEL and w_gate.shape == (d, D_FF) and w_down.shape == (D_FF, d)
    steps = n // FFN_ROWS
    final_norm = final_g is not None
    prep = mixer_weights is not None
    row_spec = pl.BlockSpec((FFN_ROWS, d), lambda i: (i, 0))
    in_specs = [row_spec, _const_spec((1, d)), _const_spec((d, D_FF)), _const_spec((d, D_FF)),
                _const_spec((D_FF, d))]
    args = [x2, norm_g.reshape(1, d), w_gate, w_up, w_down]
    out_specs = [row_spec]
    out_shape = [jax.ShapeDtypeStruct((n, d), F32)]
    if final_norm:
        in_specs.append(_const_spec((1, d)))
        args.append(final_g.reshape(1, d))
    if prep:
        assert steps >= PREP_STEPS
        last = lambda i: jnp.minimum(i, PREP_STEPS - 1)
        slab = lambda arr, **kw: pl.BlockSpec((arr.shape[0] // PREP_STEPS, arr.shape[1]), lambda i: (last(i), 0), **kw)
        once = dict(pipeline_mode=pl.Buffered(1))
        w_in_t = mixer_weights[0]
        in_specs.append(pl.BlockSpec((w_in_t.shape[0], LANES), lambda i: (0, last(i)), **once))
        in_specs += [slab(arr, **once) for arr in mixer_weights[1:]]
        args += list(mixer_weights)
        packed = [jax.ShapeDtypeStruct((d, MIX_W), BF16), jax.ShapeDtypeStruct((MLSTM_V_W, 2 * d), BF16)]
        out_shape += packed
        out_specs += [slab(p) for p in packed]
    res = pl.pallas_call(
        functools.partial(_ffn_body, final_norm=final_norm, prep=prep),
        grid=(steps,),
        in_specs=in_specs,
        out_specs=out_specs,
        out_shape=out_shape,
        scratch_shapes=[pltpu.VMEM((FFN_ROWS, D_FF), BF16)],
        compiler_params=pltpu.CompilerParams(
            dimension_semantics=("arbitrary",), vmem_limit_bytes=(7 * VMEM_BYTES_V7X) // 8),
        name="ffn_final" if final_norm else "ffn",
    )(*args)
    return (res[0], res[1:]) if prep else res[0]


Slot = collections.namedtuple("Slot", "xs h q kT vext so qaT kc vcT g")


def _slot_shapes(rows, d):
    return Slot(
        xs=pltpu.VMEM((rows, d), F32),
        h=pltpu.VMEM((rows, d), BF16),
        q=pltpu.VMEM((rows, MLSTM_QK_W), BF16),
        kT=pltpu.VMEM((MLSTM_QK_W, rows), F32),
        vext=pltpu.VMEM((rows, MLSTM_HEADS * VEXT_W), BF16),
        so=pltpu.VMEM((rows, MLSTM_V_W), F32),
        qaT=pltpu.VMEM((ATTN_Q_W, rows), BF16),
        kc=pltpu.VMEM((rows, LANES), BF16),
        vcT=pltpu.VMEM((LANES, rows), BF16),
        g=pltpu.VMEM((GATE_ROWS, rows), F32),
    )


def _softcap(a):
    return GATE_SOFTCAP * jnp.tanh(a / GATE_SOFTCAP)


def _log_sigmoid(a):
    return jnp.minimum(a, 0.0) - jnp.log1p(jnp.exp(-jnp.abs(a)))


def _lane_cumsum(a):
    lane = lax.broadcasted_iota(jnp.int32, a.shape, 1)
    d = 1
    while d < LANES:
        a = a + jnp.where(lane >= d, pltpu.roll(a, d, 1), 0.0)
        d *= 2
    return a


def _rope_tables(pos_row, inv_col):
    ang = inv_col * pos_row
    c, s = jnp.cos(ang), jnp.sin(ang)
    one, zero = jnp.ones_like(c), jnp.zeros_like(c)
    per_head = ATTN_HEAD_DIM // 8

    def tile(first, second, rest):
        rows = []
        for _ in range(LANES // ATTN_HEAD_DIM):
            rows += [first, second] + [rest] * (per_head - 2)
        return jnp.concatenate(rows, axis=0).T
    return tile(c, c, one), tile(zero, s, zero), tile(-s, zero, zero)


def _rope(a, tables):
    cos_t, sin_up, sin_dn = tables
    half = ROPE_DIM // 2
    return a * cos_t + pltpu.roll(a, half, 1) * sin_up + pltpu.roll(a, LANES - half, 1) * sin_dn


def _rep(a, shape):
    return jnp.broadcast_to(a, shape)


def _mixer_step(t_cur, ins, cur, nxt, shared, outs):
    x_ref, pos_ref, ng_ref, inv_ref, bif_ref, ong_ref, sink_ref, wmix_ref, wbr_ref = ins[:N_MIXER_IN]
    wqk_ref, wv_ref, wo_ref, wqa_ref, wkva_ref, wgm_ref, wga_ref, wout_ref = (
        _mix_piece(wmix_ref, name) for name, _ in MIX_PIECES)
    wbrm_ref, wbra_ref = wbr_ref.at[:, 0:D_MODEL], wbr_ref.at[:, D_MODEL:2 * D_MODEL]
    cast_in = ins[N_MIXER_IN:]
    o_ref, cast_out = outs[0], outs[1:]
    pk_scr, pvT_scr, sgm_scr, sga_scr, hm_scr, oa_scr, mg_scr, c_scr, m_scr = shared
    rows = MIX_ROWS
    n_chunks = rows // CHUNK
    n_blocks = rows // WINDOW
    group = ATTN_Q_HEADS // ATTN_KV_HEADS

    shared_vals = {}

    def p_norm():
        x_new = x_ref[0]
        nxt.xs[...] = x_new
        nxt.h[...] = _rms(x_new, ng_ref[...]).astype(BF16)

    def side_cast():
        for src, dst in zip(cast_in, cast_out):
            dst[...] = src[...].astype(BF16)

    def p_qk():
        qkg = jnp.dot(nxt.h[...], wqk_ref[...], preferred_element_type=F32)
        nxt.q[...] = qkg[:, 0:MLSTM_QK_W].astype(BF16)
        nxt.kT[...] = qkg[:, MLSTM_QK_W:2 * MLSTM_QK_W].T
        gates_t = qkg[:, 2 * MLSTM_QK_W:QKG_W].T
        nxt.g[...] = _softcap(gates_t[0:GATE_ROWS] + bif_ref[...])

    def p_v():
        v = jnp.dot(nxt.h[...], wv_ref[...], preferred_element_type=F32)
        for hd in range(MLSTM_HEADS):
            nxt.vext[:, hd * VEXT_W:hd * VEXT_W + MLSTM_V_DIM] = (
                v[:, hd * MLSTM_V_DIM:(hd + 1) * MLSTM_V_DIM].astype(BF16))
            nxt.vext[:, hd * VEXT_W + MLSTM_V_DIM:(hd + 1) * VEXT_W] = jnp.ones((rows, MLSTM_V_DIM), BF16)

    def p_so():
        nxt.so[...] = jax.nn.sigmoid(jnp.dot(nxt.h[...], wo_ref[...], preferred_element_type=F32))

    def p_qa():
        shared_vals["tables"] = _rope_tables(pos_ref[0].astype(F32), inv_ref[...])
        qa = jnp.dot(nxt.h[...], wqa_ref[...], preferred_element_type=F32)
        for p in range(ATTN_Q_W // LANES):
            nxt.qaT[p * LANES:(p + 1) * LANES, :] = _rope(
                qa[:, p * LANES:(p + 1) * LANES], shared_vals["tables"]).T.astype(BF16)

    def p_kva():
        kva = jnp.dot(nxt.h[...], wkva_ref[...], preferred_element_type=F32)
        ka = _rope(kva[:, 0:ATTN_KV_W], shared_vals["tables"])
        va = kva[:, ATTN_KV_W:2 * ATTN_KV_W]
        nxt.kc[...] = ka.astype(BF16)
        nxt.vcT[...] = va.T.astype(BF16)

    def gate_piece(idx):
        gref, wref = (sgm_scr, wgm_ref) if idx < 2 else (sga_scr, wga_ref)
        c0 = (idx % 2) * MIX_COL_CHUNK
        gref[:, c0:c0 + MIX_COL_CHUNK] = jax.nn.sigmoid(
            jnp.dot(cur.h[...], wref[:, c0:c0 + MIX_COL_CHUNK], preferred_element_type=F32))

    qi = lax.broadcasted_iota(jnp.int32, (CHUNK, CHUNK), 0)
    si = lax.broadcasted_iota(jnp.int32, (CHUNK, CHUNK), 1)
    tri = si <= qi
    zeros_c = jnp.zeros((MLSTM_QK_DIM, VEXT_W), BF16)
    ong = ong_ref[...]
    tile_shape = (CHUNK, CHUNK)
    rep_shape = (MLSTM_HEADS, LANES)

    chunk_stats = []
    m_vec = m_scr[0:MLSTM_HEADS, :]
    for c in range(n_chunks):
        g_c = cur.g[:, c * CHUNK:(c + 1) * CHUNK]
        i_r = g_c[0:MLSTM_HEADS]
        logf_r = _log_sigmoid(g_c[MLSTM_HEADS:GATE_ROWS])
        b_r = _lane_cumsum(logf_r)
        b_last = _rep(b_r[:, CHUNK - 1:CHUNK], rep_shape)
        a_r = b_last - b_r + i_r
        a_max = _rep(jnp.max(a_r, axis=1, keepdims=True), rep_shape)
        m_new = jnp.maximum(b_last + m_vec, a_max)
        chunk_stats.append(dict(
            logf_r=logf_r, ib_r=i_r - b_r, m_prev=m_vec,
            decay=jnp.exp(b_last + m_vec - m_new), scale=jnp.exp(a_max - m_new),
            w_loc=jnp.exp(a_r - a_max)))
        m_vec = m_new
    c_state = [c_scr[hd] for hd in range(MLSTM_HEADS)]
    c_next = [None] * MLSTM_HEADS
    s_qk = {}

    def mlstm_scores(c):
        rs = slice(c * CHUNK, (c + 1) * CHUNK)
        for pair in range(MLSTM_HEADS // 2):
            q2 = cur.q[rs, pair * PAIR_W:(pair + 1) * PAIR_W]
            k_even, k_odd = (cur.kT[hd * MLSTM_QK_DIM:(hd + 1) * MLSTM_QK_DIM, rs].astype(BF16)
                             for hd in (2 * pair, 2 * pair + 1))
            zk = jnp.zeros_like(k_even)
            k_both = jnp.concatenate([jnp.concatenate([k_even, zk], axis=1),
                                      jnp.concatenate([zk, k_odd], axis=1)], axis=0)
            s2 = jnp.dot(q2, k_both, preferred_element_type=F32)
            s_qk[c, 2 * pair] = s2[:, 0:CHUNK]
            s_qk[c, 2 * pair + 1] = s2[:, CHUNK:2 * CHUNK]

    def mlstm_state(c):
        rs = slice(c * CHUNK, (c + 1) * CHUNK)
        st = chunk_stats[c]
        for hd in range(MLSTM_HEADS):
            vext = cur.vext[rs, hd * VEXT_W:(hd + 1) * VEXT_W]
            ktw = (cur.kT[hd * MLSTM_QK_DIM:(hd + 1) * MLSTM_QK_DIM, rs] * st["w_loc"][hd:hd + 1]).astype(BF16)
            u = jnp.dot(ktw, vext, preferred_element_type=F32)
            decay = jnp.concatenate([st["decay"][hd:hd + 1]] * 2, axis=1)
            scale = jnp.concatenate([st["scale"][hd:hd + 1]] * 2, axis=1)
            c_next[hd] = decay * c_state[hd] + scale * u

    def mlstm_out(c):
        rs = slice(c * CHUNK, (c + 1) * CHUNK)
        st = chunk_stats[c]
        for hd in range(MLSTM_HEADS):
            pair, odd = divmod(hd, 2)
            q2 = cur.q[rs, pair * PAIR_W:(pair + 1) * PAIR_W]
            vext = cur.vext[rs, hd * VEXT_W:(hd + 1) * VEXT_W]
            b_col = _rep(jnp.sum(jnp.where(tri, st["logf_r"][hd:hd + 1], 0.0), axis=1, keepdims=True),
                         tile_shape)
            log_d = jnp.where(tri, b_col + st["ib_r"][hd:hd + 1], -jnp.inf)
            inter_log = b_col + st["m_prev"][hd:hd + 1]
            m_j = jnp.maximum(inter_log, _rep(jnp.max(log_d, axis=1, keepdims=True), tile_shape))
            sqk = (s_qk.pop((c, hd)) * jnp.exp(log_d - m_j)).astype(BF16)
            c_bf = c_state[hd].astype(BF16)
            c_pad = jnp.concatenate([zeros_c, c_bf] if odd else [c_bf, zeros_c], axis=0)
            q_inter = (q2 * jnp.exp(inter_log - m_j)).astype(BF16)
            tot = jnp.dot(jnp.concatenate([sqk, q_inter], axis=1), jnp.concatenate([vext, c_pad], axis=0),
                          preferred_element_type=F32)
            num = tot[:, 0:MLSTM_V_DIM]
            den = tot[:, MLSTM_V_DIM:VEXT_W]
            hh = num / jnp.maximum(jnp.abs(den), jnp.exp(-m_j))
            vs = slice(hd * MLSTM_V_DIM, (hd + 1) * MLSTM_V_DIM)
            hn = hh * lax.rsqrt(jnp.mean(hh * hh, axis=1, keepdims=True) + NORM_EPS) * ong[:, vs]
            hm_scr[rs, vs] = (hn * cur.so[rs, vs]).astype(BF16)
            c_state[hd] = c_next[hd]

    si2 = lax.broadcasted_iota(jnp.int32, (2 * WINDOW, WINDOW), 0)
    qi2 = lax.broadcasted_iota(jnp.int32, (2 * WINDOW, WINDOW), 1)
    band = (si2 > qi2) & (si2 <= qi2 + WINDOW)
    band_first = band & ((si2 >= WINDOW) | (t_cur > 0))
    zero_q = jnp.zeros((ATTN_HEAD_DIM, group * WINDOW), BF16)
    ones_rows = jnp.ones((SUM_ROWS, 2 * WINDOW), BF16)
    attn = {}

    def attn_scores(nb):
        qs = slice(nb * WINDOW, (nb + 1) * WINDOW)
        mask = band_first if nb == 0 else band
        if nb == 0:
            k_win = jnp.concatenate([pk_scr[...], cur.kc[0:WINDOW, :]], axis=0)
        else:
            k_win = cur.kc[(nb - 1) * WINDOW:(nb + 1) * WINDOW, :]
        for kh in range(ATTN_KV_HEADS):
            q_t = jnp.concatenate([cur.qaT[(kh * group + j) * ATTN_HEAD_DIM:(kh * group + j + 1) * ATTN_HEAD_DIM, qs]
                                   for j in range(group)], axis=1)
            rhs = jnp.concatenate([q_t, zero_q] if kh == 0 else [zero_q, q_t], axis=0)
            s_t = jnp.dot(k_win, rhs, preferred_element_type=F32)
            p_cols, sk_cols = [], []
            for j in range(group):
                sink = sink_ref[kh * group + j]
                sc = jnp.where(mask, s_t[:, j * WINDOW:(j + 1) * WINDOW], -jnp.inf)
                mx = jnp.maximum(jnp.max(sc, axis=0, keepdims=True), sink)
                p_cols.append(jnp.exp(sc - mx).astype(BF16))
                sk_cols.append(jnp.exp(sink - mx))
            attn[nb, kh] = (jnp.concatenate(p_cols, axis=1), jnp.concatenate(sk_cols, axis=1))

    def attn_values(nb):
        qs = slice(nb * WINDOW, (nb + 1) * WINDOW)
        if nb == 0:
            v_win = jnp.concatenate([pvT_scr[...], cur.vcT[:, 0:WINDOW]], axis=1)
        else:
            v_win = cur.vcT[:, (nb - 1) * WINDOW:(nb + 1) * WINDOW]
        for kh in range(ATTN_KV_HEADS):
            probs_t, sink_term = attn.pop((nb, kh))
            lhs = jnp.concatenate([v_win[kh * ATTN_HEAD_DIM:(kh + 1) * ATTN_HEAD_DIM], ones_rows], axis=0)
            o_t = jnp.dot(lhs, probs_t, preferred_element_type=F32)
            on_t = o_t[0:ATTN_HEAD_DIM] / (o_t[ATTN_HEAD_DIM:ATTN_HEAD_DIM + 1] + sink_term)
            for j2 in range(group // 2):
                pair_t = jnp.concatenate([on_t[:, (2 * j2) * WINDOW:(2 * j2 + 1) * WINDOW],
                                          on_t[:, (2 * j2 + 1) * WINDOW:(2 * j2 + 2) * WINDOW]], axis=0)
                pair = kh * (group // 2) + j2
                oa_scr[qs, pair * LANES:(pair + 1) * LANES] = pair_t.T.astype(BF16)

    def branch_attn(idx):
        c0 = idx * MIX_COL_CHUNK
        y_a = jnp.dot(oa_scr[...], wbra_ref[:, c0:c0 + MIX_COL_CHUNK], preferred_element_type=F32)
        sga_scr[:, c0:c0 + MIX_COL_CHUNK] = sga_scr[:, c0:c0 + MIX_COL_CHUNK] * y_a

    gate_piece(0)
    p_norm()
    side_cast()
    attn_scores(0); mlstm_scores(0); mlstm_state(0)
    gate_piece(1)
    attn_values(0); mlstm_out(0)
    p_qk()
    attn_scores(1); mlstm_scores(1); mlstm_state(1)
    p_v()
    attn_values(1); mlstm_out(1)
    p_so()
    attn_scores(2); mlstm_scores(2); mlstm_state(2)
    gate_piece(2)
    attn_values(2); mlstm_out(2)
    p_qa()
    attn_scores(3); mlstm_scores(3); mlstm_state(3)
    gate_piece(3)
    attn_values(3)
    p_kva()
    mlstm_out(3)
    branch_attn(0); branch_attn(1)

    for hd in range(MLSTM_HEADS):
        c_scr[hd] = c_state[hd]
    m_scr[0:MLSTM_HEADS, :] = m_vec
    pk_scr[...] = cur.kc[rows - WINDOW:rows, :]
    pvT_scr[...] = cur.vcT[:, rows - WINDOW:rows]

    for c0, c1 in _col_chunks(D_MODEL, MIX_COL_CHUNK):
        y_m = jnp.dot(hm_scr[...], wbrm_ref[:, c0:c1], preferred_element_type=F32)
        mg_scr[:, c0:c1] = (sgm_scr[:, c0:c1] * y_m + sga_scr[:, c0:c1]).astype(BF16)
    o_ref[0] = cur.xs[...] + jnp.dot(mg_scr[...], wout_ref[...], preferred_element_type=F32)


def _mixer_body(*refs, tiles_per_seq, n_cast):
    n_in = N_MIXER_IN + n_cast
    n_out = 1 + n_cast
    n_slot = len(Slot._fields)
    ins = refs[:n_in]
    outs = refs[n_in:n_in + n_out]
    scr = refs[n_in + n_out:]
    slot_a, slot_b = Slot(*scr[:n_slot]), Slot(*scr[n_slot:2 * n_slot])
    shared = scr[2 * n_slot:]
    prev_blocks = shared[0:2]
    c_scr, m_scr = shared[-2:]
    s = pl.program_id(0)
    t_cur = lax.rem(s + tiles_per_seq - 1, tiles_per_seq)

    @pl.when(s == 0)
    def _():
        for ref in list(slot_b) + list(shared):
            ref[...] = jnp.zeros_like(ref)

    @pl.when(t_cur == 0)
    def _():
        c_scr[...] = jnp.zeros_like(c_scr)
        m_scr[...] = jnp.zeros_like(m_scr)
        for ref in prev_blocks:
            ref[...] = jnp.zeros_like(ref)

    even = lax.rem(s, 2) == 0

    @pl.when(even)
    def _():
        _mixer_step(t_cur, ins, slot_b, slot_a, shared, outs)

    @pl.when(jnp.logical_not(even))
    def _():
        _mixer_step(t_cur, ins, slot_a, slot_b, shared, outs)


def _cast_slabs(arr, steps):
    rows = arr.shape[0]
    slab = 16
    while rows % slab or rows // slab > steps:
        slab += 16
    n_slabs = rows // slab
    return pl.BlockSpec((slab, arr.shape[1]), lambda i: (jnp.minimum(i, n_slabs - 1), 0))


def _mixer(x, positions, norm_g, b_i, b_f, out_norm_g, sinks, weights, cast_weights):
    b, s, d = x.shape
    assert s % MIX_ROWS == 0 and d == D_MODEL and positions.shape == (b, s)
    w_mix, w_br = weights
    b_if = jnp.concatenate([b_i, b_f]).reshape(GATE_ROWS, 1)
    half = ROPE_DIM // 2
    inv = (ROPE_THETA ** (-jnp.arange(half, dtype=F32) * 2.0 / ROPE_DIM)).reshape(half, 1)

    rows = MIX_ROWS
    tiles_per_seq = s // rows
    n_tiles = b * tiles_per_seq
    def in_tile(i):
        j = jnp.minimum(i, n_tiles - 1)
        return j // tiles_per_seq, j % tiles_per_seq

    def out_tile(i):
        j = jnp.maximum(i - 1, 0)
        return j // tiles_per_seq, j % tiles_per_seq

    in_specs = [
        pl.BlockSpec((1, rows, d), lambda i: (*in_tile(i), 0)),
        pl.BlockSpec((1, 1, rows), lambda i: (in_tile(i)[0], 0, in_tile(i)[1])),
        _const_spec((1, d)), _const_spec((half, 1)), _const_spec((GATE_ROWS, 1)),
        _const_spec((1, MLSTM_V_W)),
        pl.BlockSpec(memory_space=pltpu.SMEM),
        _const_spec((d, MIX_W)), _const_spec((MLSTM_V_W, 2 * d)),
    ]
    slot = list(_slot_shapes(rows, d))
    shared = [
        pltpu.VMEM((WINDOW, LANES), BF16),
        pltpu.VMEM((LANES, WINDOW), BF16),
        pltpu.VMEM((rows, d), F32),
        pltpu.VMEM((rows, d), F32),
        pltpu.VMEM((rows, MLSTM_V_W), BF16),
        pltpu.VMEM((rows, ATTN_Q_W), BF16),
        pltpu.VMEM((rows, d), BF16),
        pltpu.VMEM((MLSTM_HEADS, MLSTM_QK_DIM, VEXT_W), F32),
        pltpu.VMEM((8, LANES), F32),
    ]
    assert len(in_specs) == N_MIXER_IN
    steps = n_tiles + 1
    cast_specs = [_cast_slabs(arr, steps) for arr in cast_weights]
    res = pl.pallas_call(
        functools.partial(_mixer_body, tiles_per_seq=tiles_per_seq, n_cast=len(cast_weights)),
        grid=(steps,),
        in_specs=in_specs + cast_specs,
        out_specs=[pl.BlockSpec((1, rows, d), lambda i: (*out_tile(i), 0))] + cast_specs,
        out_shape=[jax.ShapeDtypeStruct((b, s, d), F32)]
        + [jax.ShapeDtypeStruct(arr.shape, BF16) for arr in cast_weights],
        scratch_shapes=slot + slot + shared,
        compiler_params=pltpu.CompilerParams(
            dimension_semantics=("arbitrary",), vmem_limit_bytes=(7 * VMEM_BYTES_V7X) // 8),
        name="mixer",
    )(x, positions.reshape(b, 1, s), norm_g.reshape(1, d), inv, b_if, out_norm_g.reshape(1, MLSTM_V_W),
      sinks, w_mix, w_br, *cast_weights)
    return res[0], res[1:]


def kernel(x, positions, ffn1_norm_g, ffn1_w_gate, ffn1_w_up, ffn1_w_down, mix_norm_g, w_in, mlstm_b_i, mlstm_b_f, mlstm_out_norm_g, attn_sinks, w_branch_mlstm, w_branch_attn, w_out, ffn2_norm_g, ffn2_w_gate, ffn2_w_up, ffn2_w_down, final_norm_g):
    b, s, d = x.shape
    depth = w_in.shape[0]
    for l in range(depth):
        last = l == depth - 1
        x2, mixer_w = _ffn(x.reshape(b * s, d), ffn1_norm_g[l], ffn1_w_gate[l].astype(BF16),
                           ffn1_w_up[l].astype(BF16), ffn1_w_down[l].astype(BF16),
                           mixer_weights=(w_in[l].T, w_branch_mlstm[l], w_branch_attn[l], w_out[l]))
        x3, ffn2_w = _mixer(x2.reshape(b, s, d), positions, mix_norm_g[l], mlstm_b_i[l], mlstm_b_f[l],
                            mlstm_out_norm_g[l], attn_sinks[l], mixer_w,
                            (ffn2_w_gate[l], ffn2_w_up[l], ffn2_w_down[l]))
        x = _ffn(x3.reshape(b * s, d), ffn2_norm_g[l], *ffn2_w,
                 final_g=final_norm_g if last else None).reshape(b, s, d)
    return x
```

```python
import collections
import functools

import jax
import jax.numpy as jnp
from jax import lax
from jax.experimental import pallas as pl
from jax.experimental.pallas import tpu as pltpu

D_MODEL = 1024
D_FF = 2816
MLSTM_HEADS = 4
MLSTM_QK_DIM = 64
MLSTM_V_DIM = 128
GATE_SOFTCAP = 15.0
ATTN_Q_HEADS = 8
ATTN_KV_HEADS = 2
ATTN_HEAD_DIM = 64
WINDOW = 128
ROPE_DIM = ATTN_HEAD_DIM // 4
ROPE_THETA = 500000.0
NORM_EPS = 1e-6

MLSTM_QK_W = MLSTM_HEADS * MLSTM_QK_DIM
MLSTM_V_W = MLSTM_HEADS * MLSTM_V_DIM
ATTN_Q_W = ATTN_Q_HEADS * ATTN_HEAD_DIM
ATTN_KV_W = ATTN_KV_HEADS * ATTN_HEAD_DIM
IN_WIDTHS = (MLSTM_QK_W, MLSTM_QK_W, MLSTM_V_W, MLSTM_V_W, MLSTM_HEADS, MLSTM_HEADS,
             ATTN_Q_W, ATTN_KV_W, ATTN_KV_W, D_MODEL, D_MODEL)

LANES = 128
SUBLANES = 8
MXU_COLS = 256
VMEM_BYTES_V7X = 64 * 1024 * 1024

FFN_ROWS = 1024
FFN_COL_CHUNK = 2 * MXU_COLS
MIX_ROWS = 512
MIX_COL_CHUNK = 2 * MXU_COLS
CHUNK = LANES
PAIR_W = 2 * MLSTM_QK_DIM
VEXT_W = 2 * MLSTM_V_DIM
SUM_ROWS = 16
GATE_ROWS = 2 * MLSTM_HEADS
QKG_W = 2 * MLSTM_QK_W + LANES
N_MIXER_IN = 9
MIX_PIECES = (("qkg", QKG_W), ("v", MLSTM_V_W), ("o", MLSTM_V_W), ("qa", ATTN_Q_W), ("kva", 2 * ATTN_KV_W),
              ("gm", D_MODEL), ("ga", D_MODEL), ("out", D_MODEL))
MIX_W = sum(w for _, w in MIX_PIECES)

F32 = jnp.float32
BF16 = jnp.bfloat16


def _rms(x, g):
    return x * lax.rsqrt(jnp.mean(x * x, axis=-1, keepdims=True) + NORM_EPS) * g


def _col_chunks(total, width):
    return [(c, min(c + width, total)) for c in range(0, total, width)]


def _const_spec(shape):
    return pl.BlockSpec(shape, lambda *_: (0,) * len(shape), pipeline_mode=pl.Buffered(1))


def _mix_piece(ref, name):
    start = 0
    for piece, width in MIX_PIECES:
        if piece == name:
            return ref.at[:, start:start + width]
        start += width
    raise KeyError(name)


def _prep_mixer_weights(win_ref, wbrm_ref, wbra_ref, wout_ref, wmix_o, wbr_o):
    starts = [0]
    for w in IN_WIDTHS:
        starts.append(starts[-1] + w)
    col = lambda i: win_ref[:, starts[i]:starts[i + 1]]
    gate_pad = jnp.zeros((win_ref.shape[0], LANES - GATE_ROWS), F32)
    pieces = dict(
        qkg=jnp.concatenate([col(0) * (MLSTM_QK_DIM ** -0.5), col(1), col(4), col(5), gate_pad], axis=1),
        v=col(2), o=col(3), qa=col(6) * (ATTN_HEAD_DIM ** -0.5),
        kva=jnp.concatenate([col(7), col(8)], axis=1), gm=col(9), ga=col(10), out=wout_ref[...])
    for name, _ in MIX_PIECES:
        _mix_piece(wmix_o, name)[...] = pieces[name].astype(BF16)
    wbr_o[:, 0:D_MODEL] = wbrm_ref[...].astype(BF16)
    wbr_o[:, D_MODEL:2 * D_MODEL] = wbra_ref[...].astype(BF16)


def _ffn_body(*refs, final_norm, prep):
    n_in = 5 + int(final_norm) + (4 if prep else 0)
    x_ref, g_ref, wg_ref, wu_ref, wd_ref = refs[:5]
    o_ref = refs[n_in]
    act_ref = refs[-1]
    x = x_ref[...]
    h = _rms(x, g_ref[...]).astype(BF16)
    for c0, c1 in _col_chunks(D_FF, FFN_COL_CHUNK):
        g = jnp.dot(h, wg_ref[:, c0:c1], preferred_element_type=F32)
        u = jnp.dot(h, wu_ref[:, c0:c1], preferred_element_type=F32)
        act_ref[:, c0:c1] = (g * jax.nn.sigmoid(g) * u).astype(BF16)
    if prep:
        _prep_mixer_weights(*refs[n_in - 4:n_in], *refs[n_in + 1:-1])
    r = x + 0.5 * jnp.dot(act_ref[...], wd_ref[...], preferred_element_type=F32)
    if final_norm:
        r = _rms(r, refs[5][...])
    o_ref[...] = r


def _ffn(x2, norm_g, w_gate, w_up, w_down, final_g=None, mixer_weights=None):
    n, d = x2.shape
    assert n % FFN_ROWS == 0 and d == D_MODEL and w_gate.shape == (d, D_FF) and w_down.shape == (D_FF, d)
    steps = n // FFN_ROWS
    final_norm = final_g is not None
    prep = mixer_weights is not None
    row_spec = pl.BlockSpec((FFN_ROWS, d), lambda i: (i, 0))
    in_specs = [row_spec, _const_spec((1, d)), _const_spec((d, D_FF)), _const_spec((d, D_FF)),
                _const_spec((D_FF, d))]
    args = [x2, norm_g.reshape(1, d), w_gate, w_up, w_down]
    out_specs = [row_spec]
    out_shape = [jax.ShapeDtypeStruct((n, d), F32)]
    if final_norm:
        in_specs.append(_const_spec((1, d)))
        args.append(final_g.reshape(1, d))
    if prep:
        slab = lambda arr: pl.BlockSpec((arr.shape[0] // steps, arr.shape[1]), lambda i: (i, 0))
        for arr in mixer_weights:
            in_specs.append(slab(arr))
            args.append(arr)
        packed = [jax.ShapeDtypeStruct((d, MIX_W), BF16), jax.ShapeDtypeStruct((MLSTM_V_W, 2 * d), BF16)]
        out_shape += packed
        out_specs += [slab(p) for p in packed]
    res = pl.pallas_call(
        functools.partial(_ffn_body, final_norm=final_norm, prep=prep),
        grid=(steps,),
        in_specs=in_specs,
        out_specs=out_specs,
        out_shape=out_shape,
        scratch_shapes=[pltpu.VMEM((FFN_ROWS, D_FF), BF16)],
        compiler_params=pltpu.CompilerParams(
            dimension_semantics=("arbitrary",), vmem_limit_bytes=(3 * VMEM_BYTES_V7X) // 4),
        name="ffn_final" if final_norm else "ffn",
    )(*args)
    return (res[0], res[1:]) if prep else res[0]


Slot = collections.namedtuple("Slot", "xs h q kT vext so qaT kc vcT g")


def _slot_shapes(rows, d):
    return Slot(
        xs=pltpu.VMEM((rows, d), F32),
        h=pltpu.VMEM((rows, d), BF16),
        q=pltpu.VMEM((rows, MLSTM_QK_W), BF16),
        kT=pltpu.VMEM((MLSTM_QK_W, rows), F32),
        vext=pltpu.VMEM((rows, MLSTM_HEADS * VEXT_W), BF16),
        so=pltpu.VMEM((rows, MLSTM_V_W), F32),
        qaT=pltpu.VMEM((ATTN_Q_W, rows), BF16),
        kc=pltpu.VMEM((rows, LANES), BF16),
        vcT=pltpu.VMEM((LANES, rows), BF16),
        g=pltpu.VMEM((GATE_ROWS, rows), F32),
    )


def _softcap(a):
    return GATE_SOFTCAP * jnp.tanh(a / GATE_SOFTCAP)


def _log_sigmoid(a):
    return jnp.minimum(a, 0.0) - jnp.log1p(jnp.exp(-jnp.abs(a)))


def _lane_cumsum(a):
    lane = lax.broadcasted_iota(jnp.int32, a.shape, 1)
    d = 1
    while d < LANES:
        a = a + jnp.where(lane >= d, pltpu.roll(a, d, 1), 0.0)
        d *= 2
    return a


def _rope_tables(pos_row, inv_col):
    ang = inv_col * pos_row
    c, s = jnp.cos(ang), jnp.sin(ang)
    one, zero = jnp.ones_like(c), jnp.zeros_like(c)
    per_head = ATTN_HEAD_DIM // SUBLANES

    def tile(first, second, rest):
        rows = []
        for _ in range(LANES // ATTN_HEAD_DIM):
            rows += [first, second] + [rest] * (per_head - 2)
        return jnp.concatenate(rows, axis=0).T
    return tile(c, c, one), tile(zero, s, zero), tile(-s, zero, zero)


def _rope(a, tables):
    cos_t, sin_up, sin_dn = tables
    half = ROPE_DIM // 2
    return a * cos_t + pltpu.roll(a, half, 1) * sin_up + pltpu.roll(a, LANES - half, 1) * sin_dn


def _rep(a, shape):
    return jnp.broadcast_to(a, shape)


def _mixer_step(t_cur, ins, cur, nxt, shared, outs):
    x_ref, pos_ref, ng_ref, inv_ref, bif_ref, ong_ref, sink_ref, wmix_ref, wbr_ref = ins[:N_MIXER_IN]
    wqk_ref, wv_ref, wo_ref, wqa_ref, wkva_ref, wgm_ref, wga_ref, wout_ref = (
        _mix_piece(wmix_ref, name) for name, _ in MIX_PIECES)
    wbrm_ref, wbra_ref = wbr_ref.at[:, 0:D_MODEL], wbr_ref.at[:, D_MODEL:2 * D_MODEL]
    cast_in = ins[N_MIXER_IN:]
    o_ref, cast_out = outs[0], outs[1:]
    pk_scr, pvT_scr, sgm_scr, sga_scr, hm_scr, oa_scr, mg_scr, c_scr, m_scr = shared
    rows = MIX_ROWS
    n_chunks = rows // CHUNK
    n_blocks = rows // WINDOW
    group = ATTN_Q_HEADS // ATTN_KV_HEADS

    shared_vals = {}

    def p_norm():
        x_new = x_ref[0]
        nxt.xs[...] = x_new
        nxt.h[...] = _rms(x_new, ng_ref[...]).astype(BF16)

    def side_cast():
        for src, dst in zip(cast_in, cast_out):
            dst[...] = src[...].astype(BF16)

    def p_qk():
        qkg = jnp.dot(nxt.h[...], wqk_ref[...], preferred_element_type=F32)
        nxt.q[...] = qkg[:, 0:MLSTM_QK_W].astype(BF16)
        nxt.kT[...] = qkg[:, MLSTM_QK_W:2 * MLSTM_QK_W].T
        gates_t = qkg[:, 2 * MLSTM_QK_W:QKG_W].T
        nxt.g[...] = _softcap(gates_t[0:GATE_ROWS] + bif_ref[...])

    def p_v():
        v = jnp.dot(nxt.h[...], wv_ref[...], preferred_element_type=F32)
        for hd in range(MLSTM_HEADS):
            nxt.vext[:, hd * VEXT_W:hd * VEXT_W + MLSTM_V_DIM] = (
                v[:, hd * MLSTM_V_DIM:(hd + 1) * MLSTM_V_DIM].astype(BF16))
            nxt.vext[:, hd * VEXT_W + MLSTM_V_DIM:(hd + 1) * VEXT_W] = jnp.ones((rows, MLSTM_V_DIM), BF16)

    def p_so():
        nxt.so[...] = jax.nn.sigmoid(jnp.dot(nxt.h[...], wo_ref[...], preferred_element_type=F32))

    def p_qa():
        shared_vals["tables"] = _rope_tables(pos_ref[0].astype(F32), inv_ref[...])
        qa = jnp.dot(nxt.h[...], wqa_ref[...], preferred_element_type=F32)
        for p in range(ATTN_Q_W // LANES):
            nxt.qaT[p * LANES:(p + 1) * LANES, :] = _rope(
                qa[:, p * LANES:(p + 1) * LANES], shared_vals["tables"]).T.astype(BF16)

    def p_kva():
        kva = jnp.dot(nxt.h[...], wkva_ref[...], preferred_element_type=F32)
        ka = _rope(kva[:, 0:ATTN_KV_W], shared_vals["tables"])
        va = kva[:, ATTN_KV_W:2 * ATTN_KV_W]
        nxt.kc[...] = ka.astype(BF16)
        nxt.vcT[...] = va.T.astype(BF16)

    def gate_piece(idx):
        per_gate = D_MODEL // MXU_COLS
        gref, wref = (sgm_scr, wgm_ref) if idx < per_gate else (sga_scr, wga_ref)
        c0 = (idx % per_gate) * MXU_COLS
        gref[:, c0:c0 + MXU_COLS] = jax.nn.sigmoid(
            jnp.dot(cur.h[...], wref[:, c0:c0 + MXU_COLS], preferred_element_type=F32))

    qi = lax.broadcasted_iota(jnp.int32, (CHUNK, CHUNK), 0)
    si = lax.broadcasted_iota(jnp.int32, (CHUNK, CHUNK), 1)
    tri = si <= qi
    zeros_c = jnp.zeros((MLSTM_QK_DIM, VEXT_W), BF16)
    ong = ong_ref[...]
    tile_shape = (CHUNK, CHUNK)
    rep_shape = (MLSTM_HEADS, LANES)

    chunk_stats = []
    m_vec = m_scr[0:MLSTM_HEADS, :]
    for c in range(n_chunks):
        g_c = cur.g[:, c * CHUNK:(c + 1) * CHUNK]
        i_r = g_c[0:MLSTM_HEADS]
        logf_r = _log_sigmoid(g_c[MLSTM_HEADS:GATE_ROWS])
        b_r = _lane_cumsum(logf_r)
        b_last = _rep(b_r[:, CHUNK - 1:CHUNK], rep_shape)
        a_r = b_last - b_r + i_r
        a_max = _rep(jnp.max(a_r, axis=1, keepdims=True), rep_shape)
        m_new = jnp.maximum(b_last + m_vec, a_max)
        chunk_stats.append(dict(
            logf_r=logf_r, ib_r=i_r - b_r, m_prev=m_vec,
            decay=jnp.exp(b_last + m_vec - m_new), scale=jnp.exp(a_max - m_new),
            w_loc=jnp.exp(a_r - a_max)))
        m_vec = m_new
    c_state = [c_scr[hd] for hd in range(MLSTM_HEADS)]
    c_next = [None] * MLSTM_HEADS
    s_qk = {}

    def mlstm_scores(c):
        rs = slice(c * CHUNK, (c + 1) * CHUNK)
        for pair in range(MLSTM_HEADS // 2):
            q2 = cur.q[rs, pair * PAIR_W:(pair + 1) * PAIR_W]
            k_even, k_odd = (cur.kT[hd * MLSTM_QK_DIM:(hd + 1) * MLSTM_QK_DIM, rs].astype(BF16)
                             for hd in (2 * pair, 2 * pair + 1))
            zk = jnp.zeros_like(k_even)
            k_both = jnp.concatenate([jnp.concatenate([k_even, zk], axis=1),
                                      jnp.concatenate([zk, k_odd], axis=1)], axis=0)
            s2 = jnp.dot(q2, k_both, preferred_element_type=F32)
            s_qk[c, 2 * pair] = s2[:, 0:CHUNK]
            s_qk[c, 2 * pair + 1] = s2[:, CHUNK:2 * CHUNK]

    def mlstm_state(c):
        rs = slice(c * CHUNK, (c + 1) * CHUNK)
        st = chunk_stats[c]
        for hd in range(MLSTM_HEADS):
            vext = cur.vext[rs, hd * VEXT_W:(hd + 1) * VEXT_W]
            ktw = (cur.kT[hd * MLSTM_QK_DIM:(hd + 1) * MLSTM_QK_DIM, rs] * st["w_loc"][hd:hd + 1]).astype(BF16)
            u = jnp.dot(ktw, vext, preferred_element_type=F32)
            decay = jnp.concatenate([st["decay"][hd:hd + 1]] * 2, axis=1)
            scale = jnp.concatenate([st["scale"][hd:hd + 1]] * 2, axis=1)
            c_next[hd] = decay * c_state[hd] + scale * u

    def mlstm_out(c):
        rs = slice(c * CHUNK, (c + 1) * CHUNK)
        st = chunk_stats[c]
        for hd in range(MLSTM_HEADS):
            pair, odd = divmod(hd, 2)
            q2 = cur.q[rs, pair * PAIR_W:(pair + 1) * PAIR_W]
            vext = cur.vext[rs, hd * VEXT_W:(hd + 1) * VEXT_W]
            b_col = _rep(jnp.sum(jnp.where(tri, st["logf_r"][hd:hd + 1], 0.0), axis=1, keepdims=True),
                         tile_shape)
            log_d = jnp.where(tri, b_col + st["ib_r"][hd:hd + 1], -jnp.inf)
            inter_log = b_col + st["m_prev"][hd:hd + 1]
            m_j = jnp.maximum(inter_log, _rep(jnp.max(log_d, axis=1, keepdims=True), tile_shape))
            sqk = (s_qk.pop((c, hd)) * jnp.exp(log_d - m_j)).astype(BF16)
            c_bf = c_state[hd].astype(BF16)
            c_pad = jnp.concatenate([zeros_c, c_bf] if odd else [c_bf, zeros_c], axis=0)
            q_inter = (q2 * jnp.exp(inter_log - m_j)).astype(BF16)
            tot = jnp.dot(jnp.concatenate([sqk, q_inter], axis=1), jnp.concatenate([vext, c_pad], axis=0),
                          preferred_element_type=F32)
            num = tot[:, 0:MLSTM_V_DIM]
            den = tot[:, MLSTM_V_DIM:VEXT_W]
            hh = num / jnp.maximum(jnp.abs(den), jnp.exp(-m_j))
            vs = slice(hd * MLSTM_V_DIM, (hd + 1) * MLSTM_V_DIM)
            hn = hh * lax.rsqrt(jnp.mean(hh * hh, axis=1, keepdims=True) + NORM_EPS) * ong[:, vs]
            hm_scr[rs, vs] = (hn * cur.so[rs, vs]).astype(BF16)
            c_state[hd] = c_next[hd]

    si2 = lax.broadcasted_iota(jnp.int32, (2 * WINDOW, WINDOW), 0)
    qi2 = lax.broadcasted_iota(jnp.int32, (2 * WINDOW, WINDOW), 1)
    band = (si2 > qi2) & (si2 <= qi2 + WINDOW)
    band_first = band & ((si2 >= WINDOW) | (t_cur > 0))
    zero_q = jnp.zeros((ATTN_HEAD_DIM, group * WINDOW), BF16)
    ones_rows = jnp.ones((SUM_ROWS, 2 * WINDOW), BF16)
    attn = {}

    def attn_scores(nb):
        qs = slice(nb * WINDOW, (nb + 1) * WINDOW)
        mask = band_first if nb == 0 else band
        if nb == 0:
            k_win = jnp.concatenate([pk_scr[...], cur.kc[0:WINDOW, :]], axis=0)
        else:
            k_win = cur.kc[(nb - 1) * WINDOW:(nb + 1) * WINDOW, :]
        for kh in range(ATTN_KV_HEADS):
            q_t = jnp.concatenate([cur.qaT[(kh * group + j) * ATTN_HEAD_DIM:(kh * group + j + 1) * ATTN_HEAD_DIM, qs]
                                   for j in range(group)], axis=1)
            rhs = jnp.concatenate([q_t, zero_q] if kh == 0 else [zero_q, q_t], axis=0)
            s_t = jnp.dot(k_win, rhs, preferred_element_type=F32)
            p_cols, sk_cols = [], []
            for j in range(group):
                sink = sink_ref[kh * group + j]
                sc = jnp.where(mask, s_t[:, j * WINDOW:(j + 1) * WINDOW], -jnp.inf)
                mx = jnp.maximum(jnp.max(sc, axis=0, keepdims=True), sink)
                p_cols.append(jnp.exp(sc - mx).astype(BF16))
                sk_cols.append(jnp.exp(sink - mx))
            attn[nb, kh] = (jnp.concatenate(p_cols, axis=1), jnp.concatenate(sk_cols, axis=1))

    def attn_values(nb):
        qs = slice(nb * WINDOW, (nb + 1) * WINDOW)
        if nb == 0:
            v_win = jnp.concatenate([pvT_scr[...], cur.vcT[:, 0:WINDOW]], axis=1)
        else:
            v_win = cur.vcT[:, (nb - 1) * WINDOW:(nb + 1) * WINDOW]
        for kh in range(ATTN_KV_HEADS):
            probs_t, sink_term = attn.pop((nb, kh))
            lhs = jnp.concatenate([v_win[kh * ATTN_HEAD_DIM:(kh + 1) * ATTN_HEAD_DIM], ones_rows], axis=0)
            o_t = jnp.dot(lhs, probs_t, preferred_element_type=F32)
            on_t = o_t[0:ATTN_HEAD_DIM] / (o_t[ATTN_HEAD_DIM:ATTN_HEAD_DIM + 1] + sink_term)
            for j2 in range(group // 2):
                pair_t = jnp.concatenate([on_t[:, (2 * j2) * WINDOW:(2 * j2 + 1) * WINDOW],
                                          on_t[:, (2 * j2 + 1) * WINDOW:(2 * j2 + 2) * WINDOW]], axis=0)
                pair = kh * (group // 2) + j2
                oa_scr[qs, pair * LANES:(pair + 1) * LANES] = pair_t.T.astype(BF16)

    def branch_attn(idx):
        c0 = idx * MIX_COL_CHUNK
        y_a = jnp.dot(oa_scr[...], wbra_ref[:, c0:c0 + MIX_COL_CHUNK], preferred_element_type=F32)
        sga_scr[:, c0:c0 + MIX_COL_CHUNK] = sga_scr[:, c0:c0 + MIX_COL_CHUNK] * y_a

    gate_piece(0)
    p_norm()
    side_cast()
    attn_scores(0); gate_piece(1); mlstm_scores(0); mlstm_state(0)
    p_qk()
    attn_values(0); gate_piece(2); mlstm_out(0)
    attn_scores(1); gate_piece(3); mlstm_scores(1); mlstm_state(1)
    p_v()
    attn_values(1); gate_piece(4); mlstm_out(1)
    attn_scores(2); p_so(); mlstm_scores(2); mlstm_state(2)
    attn_values(2); gate_piece(5); mlstm_out(2)
    attn_scores(3); p_qa(); mlstm_scores(3); mlstm_state(3)
    attn_values(3); gate_piece(6)
    p_kva()
    mlstm_out(3); gate_piece(7)
    branch_attn(0); branch_attn(1)

    for hd in range(MLSTM_HEADS):
        c_scr[hd] = c_state[hd]
    m_scr[0:MLSTM_HEADS, :] = m_vec
    pk_scr[...] = cur.kc[rows - WINDOW:rows, :]
    pvT_scr[...] = cur.vcT[:, rows - WINDOW:rows]

    for c0, c1 in _col_chunks(D_MODEL, MIX_COL_CHUNK):
        y_m = jnp.dot(hm_scr[...], wbrm_ref[:, c0:c1], preferred_element_type=F32)
        mg_scr[:, c0:c1] = (sgm_scr[:, c0:c1] * y_m + sga_scr[:, c0:c1]).astype(BF16)
    o_ref[0] = cur.xs[...] + jnp.dot(mg_scr[...], wout_ref[...], preferred_element_type=F32)


def _mixer_body(*refs, tiles_per_seq, n_cast):
    n_in = N_MIXER_IN + n_cast
    n_out = 1 + n_cast
    n_slot = len(Slot._fields)
    ins = refs[:n_in]
    outs = refs[n_in:n_in + n_out]
    scr = refs[n_in + n_out:]
    slot_a, slot_b = Slot(*scr[:n_slot]), Slot(*scr[n_slot:2 * n_slot])
    shared = scr[2 * n_slot:]
    prev_blocks = shared[0:2]
    c_scr, m_scr = shared[-2:]
    s = pl.program_id(0)
    t_cur = lax.rem(s + tiles_per_seq - 1, tiles_per_seq)

    @pl.when(s == 0)
    def _():
        for ref in list(slot_b) + list(shared):
            ref[...] = jnp.zeros_like(ref)

    @pl.when(t_cur == 0)
    def _():
        c_scr[...] = jnp.zeros_like(c_scr)
        m_scr[...] = jnp.zeros_like(m_scr)
        for ref in prev_blocks:
            ref[...] = jnp.zeros_like(ref)

    even = lax.rem(s, 2) == 0

    @pl.when(even)
    def _():
        _mixer_step(t_cur, ins, slot_b, slot_a, shared, outs)

    @pl.when(jnp.logical_not(even))
    def _():
        _mixer_step(t_cur, ins, slot_a, slot_b, shared, outs)


def _cast_slabs(arr, steps):
    rows = arr.shape[0]
    slab = 16
    while rows % slab or rows // slab > steps:
        slab += 16
    n_slabs = rows // slab
    return pl.BlockSpec((slab, arr.shape[1]), lambda i: (jnp.minimum(i, n_slabs - 1), 0))


def _mixer(x, positions, norm_g, b_i, b_f, out_norm_g, sinks, weights, cast_weights):
    b, s, d = x.shape
    assert s % MIX_ROWS == 0 and d == D_MODEL and positions.shape == (b, s)
    w_mix, w_br = weights
    b_if = jnp.concatenate([b_i, b_f]).reshape(GATE_ROWS, 1)
    half = ROPE_DIM // 2
    inv = (ROPE_THETA ** (-jnp.arange(half, dtype=F32) * 2.0 / ROPE_DIM)).reshape(half, 1)

    rows = MIX_ROWS
    tiles_per_seq = s // rows
    n_tiles = b * tiles_per_seq
    def in_tile(i):
        j = jnp.minimum(i, n_tiles - 1)
        return j // tiles_per_seq, j % tiles_per_seq

    def out_tile(i):
        j = jnp.maximum(i - 1, 0)
        return j // tiles_per_seq, j % tiles_per_seq

    in_specs = [
        pl.BlockSpec((1, rows, d), lambda i: (*in_tile(i), 0)),
        pl.BlockSpec((1, 1, rows), lambda i: (in_tile(i)[0], 0, in_tile(i)[1])),
        _const_spec((1, d)), _const_spec((half, 1)), _const_spec((GATE_ROWS, 1)),
        _const_spec((1, MLSTM_V_W)),
        pl.BlockSpec(memory_space=pltpu.SMEM),
        _const_spec((d, MIX_W)), _const_spec((MLSTM_V_W, 2 * d)),
    ]
    slot = list(_slot_shapes(rows, d))
    shared = [
        pltpu.VMEM((WINDOW, LANES), BF16),
        pltpu.VMEM((LANES, WINDOW), BF16),
        pltpu.VMEM((rows, d), F32),
        pltpu.VMEM((rows, d), F32),
        pltpu.VMEM((rows, MLSTM_V_W), BF16),
        pltpu.VMEM((rows, ATTN_Q_W), BF16),
        pltpu.VMEM((rows, d), BF16),
        pltpu.VMEM((MLSTM_HEADS, MLSTM_QK_DIM, VEXT_W), F32),
        pltpu.VMEM((SUBLANES, LANES), F32),
    ]
    assert len(in_specs) == N_MIXER_IN
    steps = n_tiles + 1
    cast_specs = [_cast_slabs(arr, steps) for arr in cast_weights]
    res = pl.pallas_call(
        functools.partial(_mixer_body, tiles_per_seq=tiles_per_seq, n_cast=len(cast_weights)),
        grid=(steps,),
        in_specs=in_specs + cast_specs,
        out_specs=[pl.BlockSpec((1, rows, d), lambda i: (*out_tile(i), 0))] + cast_specs,
        out_shape=[jax.ShapeDtypeStruct((b, s, d), F32)]
        + [jax.ShapeDtypeStruct(arr.shape, BF16) for arr in cast_weights],
        scratch_shapes=slot + slot + shared,
        compiler_params=pltpu.CompilerParams(
            dimension_semantics=("arbitrary",), vmem_limit_bytes=(7 * VMEM_BYTES_V7X) // 8),
        name="mixer",
    )(x, positions.reshape(b, 1, s), norm_g.reshape(1, d), inv, b_if, out_norm_g.reshape(1, MLSTM_V_W),
      sinks, w_mix, w_br, *cast_weights)
    return res[0], res[1:]


def kernel(x, positions, ffn1_norm_g, ffn1_w_gate, ffn1_w_up, ffn1_w_down, mix_norm_g, w_in, mlstm_b_i, mlstm_b_f, mlstm_out_norm_g, attn_sinks, w_branch_mlstm, w_branch_attn, w_out, ffn2_norm_g, ffn2_w_gate, ffn2_w_up, ffn2_w_down, final_norm_g):
    b, s, d = x.shape
    depth = w_in.shape[0]
    for l in range(depth):
        last = l == depth - 1
        x2, mixer_w = _ffn(x.reshape(b * s, d), ffn1_norm_g[l], ffn1_w_gate[l].astype(BF16),
                           ffn1_w_up[l].astype(BF16), ffn1_w_down[l].astype(BF16),
                           mixer_weights=(w_in[l], w_branch_mlstm[l], w_branch_attn[l], w_out[l]))
        x3, ffn2_w = _mixer(x2.reshape(b, s, d), positions, mix_norm_g[l], mlstm_b_i[l], mlstm_b_f[l],
                            mlstm_out_norm_g[l], attn_sinks[l], mixer_w,
                            (ffn2_w_gate[l], ffn2_w_up[l], ffn2_w_down[l]))
        x = _ffn(x3.reshape(b * s, d), ffn2_norm_g[l], *ffn2_w,
                 final_g=final_norm_g if last else None).reshape(b, s, d)
    return x
```

```python
import collections
import functools

import jax
import jax.numpy as jnp
from jax import lax
from jax.experimental import pallas as pl
from jax.experimental.pallas import tpu as pltpu

D_MODEL = 1024
D_FF = 2816
MLSTM_HEADS = 4
MLSTM_QK_DIM = 64
MLSTM_V_DIM = 128
GATE_SOFTCAP = 15.0
ATTN_Q_HEADS = 8
ATTN_KV_HEADS = 2
ATTN_HEAD_DIM = 64
WINDOW = 128
ROPE_DIM = ATTN_HEAD_DIM // 4
ROPE_THETA = 500000.0
NORM_EPS = 1e-6

MLSTM_QK_W = MLSTM_HEADS * MLSTM_QK_DIM
MLSTM_V_W = MLSTM_HEADS * MLSTM_V_DIM
ATTN_Q_W = ATTN_Q_HEADS * ATTN_HEAD_DIM
ATTN_KV_W = ATTN_KV_HEADS * ATTN_HEAD_DIM
IN_WIDTHS = (MLSTM_QK_W, MLSTM_QK_W, MLSTM_V_W, MLSTM_V_W, MLSTM_HEADS, MLSTM_HEADS,
             ATTN_Q_W, ATTN_KV_W, ATTN_KV_W, D_MODEL, D_MODEL)

LANES = 128
SUBLANES = 8
MXU_COLS = 256
VMEM_BYTES_V7X = 64 * 1024 * 1024

FFN_ROWS = 1024
FFN_COL_CHUNK = 2 * MXU_COLS
STAGE_ROWS_IN = 128
STAGE_ROWS_OUT = 352
MIX_ROWS = 512
MIX_COL_CHUNK = 2 * MXU_COLS
CHUNK = LANES
PAIR_W = 2 * MLSTM_QK_DIM
VEXT_W = 2 * MLSTM_V_DIM
SUM_ROWS = 16
GATE_ROWS = 2 * MLSTM_HEADS
QKG_W = 2 * MLSTM_QK_W + LANES
N_MIXER_IN = 9
MIX_PIECES = (("qkg", QKG_W), ("v", MLSTM_V_W), ("o", MLSTM_V_W), ("qa", ATTN_Q_W), ("kva", 2 * ATTN_KV_W),
              ("gm", D_MODEL), ("ga", D_MODEL), ("out", D_MODEL))
MIX_W = sum(w for _, w in MIX_PIECES)

F32 = jnp.float32
BF16 = jnp.bfloat16


def _rms(x, g):
    return x * lax.rsqrt(jnp.mean(x * x, axis=-1, keepdims=True) + NORM_EPS) * g


def _col_chunks(total, width):
    return [(c, min(c + width, total)) for c in range(0, total, width)]


def _const_spec(shape):
    return pl.BlockSpec(shape, lambda *_: (0,) * len(shape), pipeline_mode=pl.Buffered(1))


def _mix_piece(ref, name):
    start = 0
    for piece, width in MIX_PIECES:
        if piece == name:
            return ref.at[:, start:start + width]
        start += width
    raise KeyError(name)


def _prep_mixer_weights(win_ref, wbrm_ref, wbra_ref, wout_ref, wmix_o, wbr_o):
    starts = [0]
    for w in IN_WIDTHS:
        starts.append(starts[-1] + w)
    col = lambda i: win_ref[:, starts[i]:starts[i + 1]]
    gate_pad = jnp.zeros((win_ref.shape[0], LANES - GATE_ROWS), F32)
    pieces = dict(
        qkg=jnp.concatenate([col(0) * (MLSTM_QK_DIM ** -0.5), col(1), col(4), col(5), gate_pad], axis=1),
        v=col(2), o=col(3), qa=col(6) * (ATTN_HEAD_DIM ** -0.5),
        kva=jnp.concatenate([col(7), col(8)], axis=1), gm=col(9), ga=col(10), out=wout_ref[...])
    for name, _ in MIX_PIECES:
        _mix_piece(wmix_o, name)[...] = pieces[name].astype(BF16)
    wbr_o[:, 0:D_MODEL] = wbrm_ref[...].astype(BF16)
    wbr_o[:, D_MODEL:2 * D_MODEL] = wbra_ref[...].astype(BF16)


def _stream_cast(src_hbm, dst_ref, stage_ref, sem_ref, chunk_rows):
    n_chunks = src_hbm.shape[0] // chunk_rows

    def copy(k):
        return pltpu.make_async_copy(src_hbm.at[k * chunk_rows:(k + 1) * chunk_rows], stage_ref.at[k % 2],
                                     sem_ref.at[k % 2])
    copy(0).start()
    for k in range(n_chunks):
        if k + 1 < n_chunks:
            copy(k + 1).start()
        copy(k).wait()
        dst_ref[k * chunk_rows:(k + 1) * chunk_rows, :] = stage_ref[k % 2].astype(BF16)


def _ffn_body(*refs, final_norm, prep, f32_weights):
    n_in = 5 + int(final_norm) + (4 if prep else 0)
    x_ref, g_ref, wg_ref, wu_ref, wd_ref = refs[:5]
    o_ref = refs[n_in]
    act_ref = refs[-1]
    if f32_weights:
        wg_scr, wu_scr, wd_scr, stage_in, stage_out, sems = refs[-7:-1]

        @pl.when(pl.program_id(0) == 0)
        def _():
            _stream_cast(wg_ref, wg_scr, stage_in, sems, STAGE_ROWS_IN)
            _stream_cast(wu_ref, wu_scr, stage_in, sems, STAGE_ROWS_IN)
            _stream_cast(wd_ref, wd_scr, stage_out, sems, STAGE_ROWS_OUT)
        wg_ref, wu_ref, wd_ref = wg_scr, wu_scr, wd_scr
    x = x_ref[...]
    h = _rms(x, g_ref[...]).astype(BF16)
    for c0, c1 in _col_chunks(D_FF, FFN_COL_CHUNK):
        g = jnp.dot(h, wg_ref[:, c0:c1], preferred_element_type=F32)
        u = jnp.dot(h, wu_ref[:, c0:c1], preferred_element_type=F32)
        act_ref[:, c0:c1] = (g * jax.nn.sigmoid(g) * u).astype(BF16)
    if prep:
        _prep_mixer_weights(*refs[n_in - 4:n_in], *refs[n_in + 1:n_in + 3])
    r = x + 0.5 * jnp.dot(act_ref[...], wd_ref[...], preferred_element_type=F32)
    if final_norm:
        r = _rms(r, refs[5][...])
    o_ref[...] = r


def _ffn(x2, norm_g, w_gate, w_up, w_down, final_g=None, mixer_weights=None):
    n, d = x2.shape
    assert n % FFN_ROWS == 0 and d == D_MODEL and w_gate.shape == (d, D_FF) and w_down.shape == (D_FF, d)
    steps = n // FFN_ROWS
    final_norm = final_g is not None
    prep = mixer_weights is not None
    f32_weights = w_gate.dtype == F32
    assert w_up.dtype == w_gate.dtype and w_down.dtype == w_gate.dtype
    row_spec = pl.BlockSpec((FFN_ROWS, d), lambda i: (i, 0))
    weight_specs = ([pl.BlockSpec(memory_space=pl.ANY)] * 3 if f32_weights else
                    [_const_spec((d, D_FF)), _const_spec((d, D_FF)), _const_spec((D_FF, d))])
    in_specs = [row_spec, _const_spec((1, d))] + weight_specs
    args = [x2, norm_g.reshape(1, d), w_gate, w_up, w_down]
    out_specs = [row_spec]
    out_shape = [jax.ShapeDtypeStruct((n, d), F32)]
    if final_norm:
        in_specs.append(_const_spec((1, d)))
        args.append(final_g.reshape(1, d))
    if prep:
        slab = lambda arr: pl.BlockSpec((arr.shape[0] // steps, arr.shape[1]), lambda i: (i, 0))
        for arr in mixer_weights:
            in_specs.append(slab(arr))
            args.append(arr)
        packed = [jax.ShapeDtypeStruct((d, MIX_W), BF16), jax.ShapeDtypeStruct((MLSTM_V_W, 2 * d), BF16)]
        out_shape += packed
        out_specs += [slab(p) for p in packed]
    scratch = []
    if f32_weights:
        assert d % STAGE_ROWS_IN == 0 and D_FF % STAGE_ROWS_OUT == 0
        scratch = [pltpu.VMEM((d, D_FF), BF16), pltpu.VMEM((d, D_FF), BF16), pltpu.VMEM((D_FF, d), BF16),
                   pltpu.VMEM((2, STAGE_ROWS_IN, D_FF), F32), pltpu.VMEM((2, STAGE_ROWS_OUT, d), F32),
                   pltpu.SemaphoreType.DMA((2,))]
    scratch.append(pltpu.VMEM((FFN_ROWS, D_FF), BF16))
    vmem_eighths = 7 if f32_weights else 6
    res = pl.pallas_call(
        functools.partial(_ffn_body, final_norm=final_norm, prep=prep, f32_weights=f32_weights),
        grid=(steps,),
        in_specs=in_specs,
        out_specs=out_specs,
        out_shape=out_shape,
        scratch_shapes=scratch,
        compiler_params=pltpu.CompilerParams(
            dimension_semantics=("arbitrary",), vmem_limit_bytes=(vmem_eighths * VMEM_BYTES_V7X) // 8),
        name="ffn_final" if final_norm else "ffn",
    )(*args)
    return (res[0], res[1:]) if prep else res[0]


Slot = collections.namedtuple("Slot", "xs h q kT vext so qaT kc vcT g")


def _slot_shapes(rows, d):
    return Slot(
        xs=pltpu.VMEM((rows, d), F32),
        h=pltpu.VMEM((rows, d), BF16),
        q=pltpu.VMEM((rows, MLSTM_QK_W), BF16),
        kT=pltpu.VMEM((MLSTM_QK_W, rows), F32),
        vext=pltpu.VMEM((rows, MLSTM_HEADS * VEXT_W), BF16),
        so=pltpu.VMEM((rows, MLSTM_V_W), F32),
        qaT=pltpu.VMEM((ATTN_Q_W, rows), BF16),
        kc=pltpu.VMEM((rows, LANES), BF16),
        vcT=pltpu.VMEM((LANES, rows), BF16),
        g=pltpu.VMEM((GATE_ROWS, rows), F32),
    )


def _softcap(a):
    return GATE_SOFTCAP * jnp.tanh(a / GATE_SOFTCAP)


def _log_sigmoid(a):
    return jnp.minimum(a, 0.0) - jnp.log1p(jnp.exp(-jnp.abs(a)))


def _lane_cumsum(a):
    lane = lax.broadcasted_iota(jnp.int32, a.shape, 1)
    d = 1
    while d < LANES:
        a = a + jnp.where(lane >= d, pltpu.roll(a, d, 1), 0.0)
        d *= 2
    return a


def _rope_tables(pos_row, inv_col):
    ang = inv_col * pos_row
    c, s = jnp.cos(ang), jnp.sin(ang)
    one, zero = jnp.ones_like(c), jnp.zeros_like(c)
    per_head = ATTN_HEAD_DIM // SUBLANES

    def tile(first, second, rest):
        rows = []
        for _ in range(LANES // ATTN_HEAD_DIM):
            rows += [first, second] + [rest] * (per_head - 2)
        return jnp.concatenate(rows, axis=0).T
    return tile(c, c, one), tile(zero, s, zero), tile(-s, zero, zero)


def _rope(a, tables):
    cos_t, sin_up, sin_dn = tables
    half = ROPE_DIM // 2
    return a * cos_t + pltpu.roll(a, half, 1) * sin_up + pltpu.roll(a, LANES - half, 1) * sin_dn


def _rep(a, shape):
    return jnp.broadcast_to(a, shape)


def _mixer_step(t_cur, ins, cur, nxt, shared, outs):
    x_ref, pos_ref, ng_ref, inv_ref, bif_ref, ong_ref, sink_ref, wmix_ref, wbr_ref = ins[:N_MIXER_IN]
    wqk_ref, wv_ref, wo_ref, wqa_ref, wkva_ref, wgm_ref, wga_ref, wout_ref = (
        _mix_piece(wmix_ref, name) for name, _ in MIX_PIECES)
    wbrm_ref, wbra_ref = wbr_ref.at[:, 0:D_MODEL], wbr_ref.at[:, D_MODEL:2 * D_MODEL]
    cast_in = ins[N_MIXER_IN:]
    o_ref, cast_out = outs[0], outs[1:]
    pk_scr, pvT_scr, sgm_scr, sga_scr, hm_scr, oa_scr, mg_scr, c_scr, m_scr = shared
    rows = MIX_ROWS
    n_chunks = rows // CHUNK
    n_blocks = rows // WINDOW
    group = ATTN_Q_HEADS // ATTN_KV_HEADS

    shared_vals = {}

    def p_norm():
        x_new = x_ref[0]
        nxt.xs[...] = x_new
        nxt.h[...] = _rms(x_new, ng_ref[...]).astype(BF16)

    def side_cast():
        for src, dst in zip(cast_in, cast_out):
            dst[...] = src[...].astype(BF16)

    def p_qk():
        qkg = jnp.dot(nxt.h[...], wqk_ref[...], preferred_element_type=F32)
        nxt.q[...] = qkg[:, 0:MLSTM_QK_W].astype(BF16)
        nxt.kT[...] = qkg[:, MLSTM_QK_W:2 * MLSTM_QK_W].T
        gates_t = qkg[:, 2 * MLSTM_QK_W:QKG_W].T
        nxt.g[...] = _softcap(gates_t[0:GATE_ROWS] + bif_ref[...])

    def p_v():
        v = jnp.dot(nxt.h[...], wv_ref[...], preferred_element_type=F32)
        for hd in range(MLSTM_HEADS):
            nxt.vext[:, hd * VEXT_W:hd * VEXT_W + MLSTM_V_DIM] = (
                v[:, hd * MLSTM_V_DIM:(hd + 1) * MLSTM_V_DIM].astype(BF16))
            nxt.vext[:, hd * VEXT_W + MLSTM_V_DIM:(hd + 1) * VEXT_W] = jnp.ones((rows, MLSTM_V_DIM), BF16)

    def p_so():
        nxt.so[...] = jax.nn.sigmoid(jnp.dot(nxt.h[...], wo_ref[...], preferred_element_type=F32))

    def p_qa():
        shared_vals["tables"] = _rope_tables(pos_ref[0].astype(F32), inv_ref[...])
        qa = jnp.dot(nxt.h[...], wqa_ref[...], preferred_element_type=F32)
        for p in range(ATTN_Q_W // LANES):
            nxt.qaT[p * LANES:(p + 1) * LANES, :] = _rope(
                qa[:, p * LANES:(p + 1) * LANES], shared_vals["tables"]).T.astype(BF16)

    def p_kva():
        kva = jnp.dot(nxt.h[...], wkva_ref[...], preferred_element_type=F32)
        ka = _rope(kva[:, 0:ATTN_KV_W], shared_vals["tables"])
        va = kva[:, ATTN_KV_W:2 * ATTN_KV_W]
        nxt.kc[...] = ka.astype(BF16)
        nxt.vcT[...] = va.T.astype(BF16)

    def gate_piece(idx):
        per_gate = D_MODEL // MXU_COLS
        gref, wref = (sgm_scr, wgm_ref) if idx < per_gate else (sga_scr, wga_ref)
        c0 = (idx % per_gate) * MXU_COLS
        gref[:, c0:c0 + MXU_COLS] = jax.nn.sigmoid(
            jnp.dot(cur.h[...], wref[:, c0:c0 + MXU_COLS], preferred_element_type=F32))

    qi = lax.broadcasted_iota(jnp.int32, (CHUNK, CHUNK), 0)
    si = lax.broadcasted_iota(jnp.int32, (CHUNK, CHUNK), 1)
    tri = si <= qi
    zeros_c = jnp.zeros((MLSTM_QK_DIM, VEXT_W), BF16)
    ong = ong_ref[...]
    tile_shape = (CHUNK, CHUNK)
    rep_shape = (MLSTM_HEADS, LANES)

    chunk_stats = []
    m_vec = m_scr[0:MLSTM_HEADS, :]
    for c in range(n_chunks):
        g_c = cur.g[:, c * CHUNK:(c + 1) * CHUNK]
        i_r = g_c[0:MLSTM_HEADS]
        logf_r = _log_sigmoid(g_c[MLSTM_HEADS:GATE_ROWS])
        b_r = _lane_cumsum(logf_r)
        b_last = _rep(b_r[:, CHUNK - 1:CHUNK], rep_shape)
        a_r = b_last - b_r + i_r
        a_max = _rep(jnp.max(a_r, axis=1, keepdims=True), rep_shape)
        m_new = jnp.maximum(b_last + m_vec, a_max)
        chunk_stats.append(dict(
            logf_r=logf_r, ib_r=i_r - b_r, m_prev=m_vec,
            decay=jnp.exp(b_last + m_vec - m_new), scale=jnp.exp(a_max - m_new),
            w_loc=jnp.exp(a_r - a_max)))
        m_vec = m_new
    c_state = [c_scr[hd] for hd in range(MLSTM_HEADS)]
    c_next = [None] * MLSTM_HEADS
    s_qk = {}

    def mlstm_scores(c):
        rs = slice(c * CHUNK, (c + 1) * CHUNK)
        for pair in range(MLSTM_HEADS // 2):
            q2 = cur.q[rs, pair * PAIR_W:(pair + 1) * PAIR_W]
            k_even, k_odd = (cur.kT[hd * MLSTM_QK_DIM:(hd + 1) * MLSTM_QK_DIM, rs].astype(BF16)
                             for hd in (2 * pair, 2 * pair + 1))
            zk = jnp.zeros_like(k_even)
            k_both = jnp.concatenate([jnp.concatenate([k_even, zk], axis=1),
                                      jnp.concatenate([zk, k_odd], axis=1)], axis=0)
            s2 = jnp.dot(q2, k_both, preferred_element_type=F32)
            s_qk[c, 2 * pair] = s2[:, 0:CHUNK]
            s_qk[c, 2 * pair + 1] = s2[:, CHUNK:2 * CHUNK]

    def mlstm_state(c):
        rs = slice(c * CHUNK, (c + 1) * CHUNK)
        st = chunk_stats[c]
        for hd in range(MLSTM_HEADS):
            vext = cur.vext[rs, hd * VEXT_W:(hd + 1) * VEXT_W]
            ktw = (cur.kT[hd * MLSTM_QK_DIM:(hd + 1) * MLSTM_QK_DIM, rs] * st["w_loc"][hd:hd + 1]).astype(BF16)
            u = jnp.dot(ktw, vext, preferred_element_type=F32)
            decay = jnp.concatenate([st["decay"][hd:hd + 1]] * 2, axis=1)
            scale = jnp.concatenate([st["scale"][hd:hd + 1]] * 2, axis=1)
            c_next[hd] = decay * c_state[hd] + scale * u

    def mlstm_out(c):
        rs = slice(c * CHUNK, (c + 1) * CHUNK)
        st = chunk_stats[c]
        for hd in range(MLSTM_HEADS):
            pair, odd = divmod(hd, 2)
            q2 = cur.q[rs, pair * PAIR_W:(pair + 1) * PAIR_W]
            vext = cur.vext[rs, hd * VEXT_W:(hd + 1) * VEXT_W]
            b_col = _rep(jnp.sum(jnp.where(tri, st["logf_r"][hd:hd + 1], 0.0), axis=1, keepdims=True),
                         tile_shape)
            log_d = jnp.where(tri, b_col + st["ib_r"][hd:hd + 1], -jnp.inf)
            inter_log = b_col + st["m_prev"][hd:hd + 1]
            m_j = jnp.maximum(inter_log, _rep(jnp.max(log_d, axis=1, keepdims=True), tile_shape))
            sqk = (s_qk.pop((c, hd)) * jnp.exp(log_d - m_j)).astype(BF16)
            c_bf = c_state[hd].astype(BF16)
            c_pad = jnp.concatenate([zeros_c, c_bf] if odd else [c_bf, zeros_c], axis=0)
            q_inter = (q2 * jnp.exp(inter_log - m_j)).astype(BF16)
            tot = jnp.dot(jnp.concatenate([sqk, q_inter], axis=1), jnp.concatenate([vext, c_pad], axis=0),
                          preferred_element_type=F32)
            num = tot[:, 0:MLSTM_V_DIM]
            den = tot[:, MLSTM_V_DIM:VEXT_W]
            hh = num / jnp.maximum(jnp.abs(den), jnp.exp(-m_j))
            vs = slice(hd * MLSTM_V_DIM, (hd + 1) * MLSTM_V_DIM)
            hn = hh * lax.rsqrt(jnp.mean(hh * hh, axis=1, keepdims=True) + NORM_EPS) * ong[:, vs]
            hm_scr[rs, vs] = (hn * cur.so[rs, vs]).astype(BF16)
            c_state[hd] = c_next[hd]

    si2 = lax.broadcasted_iota(jnp.int32, (2 * WINDOW, WINDOW), 0)
    qi2 = lax.broadcasted_iota(jnp.int32, (2 * WINDOW, WINDOW), 1)
    band = (si2 > qi2) & (si2 <= qi2 + WINDOW)
    band_first = band & ((si2 >= WINDOW) | (t_cur > 0))
    zero_q = jnp.zeros((ATTN_HEAD_DIM, group * WINDOW), BF16)
    ones_rows = jnp.ones((SUM_ROWS, 2 * WINDOW), BF16)
    attn = {}

    def attn_scores(nb):
        qs = slice(nb * WINDOW, (nb + 1) * WINDOW)
        mask = band_first if nb == 0 else band
        if nb == 0:
            k_win = jnp.concatenate([pk_scr[...], cur.kc[0:WINDOW, :]], axis=0)
        else:
            k_win = cur.kc[(nb - 1) * WINDOW:(nb + 1) * WINDOW, :]
        for kh in range(ATTN_KV_HEADS):
            q_t = jnp.concatenate([cur.qaT[(kh * group + j) * ATTN_HEAD_DIM:(kh * group + j + 1) * ATTN_HEAD_DIM, qs]
                                   for j in range(group)], axis=1)
            rhs = jnp.concatenate([q_t, zero_q] if kh == 0 else [zero_q, q_t], axis=0)
            s_t = jnp.dot(k_win, rhs, preferred_element_type=F32)
            p_cols, sk_cols = [], []
            for j in range(group):
                sink = sink_ref[kh * group + j]
                sc = jnp.where(mask, s_t[:, j * WINDOW:(j + 1) * WINDOW], -jnp.inf)
                mx = jnp.maximum(jnp.max(sc, axis=0, keepdims=True), sink)
                p_cols.append(jnp.exp(sc - mx).astype(BF16))
                sk_cols.append(jnp.exp(sink - mx))
            attn[nb, kh] = (jnp.concatenate(p_cols, axis=1), jnp.concatenate(sk_cols, axis=1))

    def attn_values(nb):
        qs = slice(nb * WINDOW, (nb + 1) * WINDOW)
        if nb == 0:
            v_win = jnp.concatenate([pvT_scr[...], cur.vcT[:, 0:WINDOW]], axis=1)
        else:
            v_win = cur.vcT[:, (nb - 1) * WINDOW:(nb + 1) * WINDOW]
        for kh in range(ATTN_KV_HEADS):
            probs_t, sink_term = attn.pop((nb, kh))
            lhs = jnp.concatenate([v_win[kh * ATTN_HEAD_DIM:(kh + 1) * ATTN_HEAD_DIM], ones_rows], axis=0)
            o_t = jnp.dot(lhs, probs_t, preferred_element_type=F32)
            on_t = o_t[0:ATTN_HEAD_DIM] / (o_t[ATTN_HEAD_DIM:ATTN_HEAD_DIM + 1] + sink_term)
            for j2 in range(group // 2):
                pair_t = jnp.concatenate([on_t[:, (2 * j2) * WINDOW:(2 * j2 + 1) * WINDOW],
                                          on_t[:, (2 * j2 + 1) * WINDOW:(2 * j2 + 2) * WINDOW]], axis=0)
                pair = kh * (group // 2) + j2
                oa_scr[qs, pair * LANES:(pair + 1) * LANES] = pair_t.T.astype(BF16)

    def branch_attn(idx):
        c0 = idx * MIX_COL_CHUNK
        y_a = jnp.dot(oa_scr[...], wbra_ref[:, c0:c0 + MIX_COL_CHUNK], preferred_element_type=F32)
        sga_scr[:, c0:c0 + MIX_COL_CHUNK] = sga_scr[:, c0:c0 + MIX_COL_CHUNK] * y_a

    gate_piece(0)
    p_norm()
    side_cast()
    attn_scores(0); gate_piece(1); mlstm_scores(0); mlstm_state(0)
    p_qk()
    attn_values(0); gate_piece(2); mlstm_out(0)
    attn_scores(1); gate_piece(3); mlstm_scores(1); mlstm_state(1)
    p_v()
    attn_values(1); gate_piece(4); mlstm_out(1)
    attn_scores(2); p_so(); mlstm_scores(2); mlstm_state(2)
    attn_values(2); gate_piece(5); mlstm_out(2)
    attn_scores(3); p_qa(); mlstm_scores(3); mlstm_state(3)
    attn_values(3); gate_piece(6)
    p_kva()
    mlstm_out(3); gate_piece(7)
    branch_attn(0); branch_attn(1)

    for hd in range(MLSTM_HEADS):
        c_scr[hd] = c_state[hd]
    m_scr[0:MLSTM_HEADS, :] = m_vec
    pk_scr[...] = cur.kc[rows - WINDOW:rows, :]
    pvT_scr[...] = cur.vcT[:, rows - WINDOW:rows]

    for c0, c1 in _col_chunks(D_MODEL, MIX_COL_CHUNK):
        y_m = jnp.dot(hm_scr[...], wbrm_ref[:, c0:c1], preferred_element_type=F32)
        mg_scr[:, c0:c1] = (sgm_scr[:, c0:c1] * y_m + sga_scr[:, c0:c1]).astype(BF16)
    o_ref[0] = cur.xs[...] + jnp.dot(mg_scr[...], wout_ref[...], preferred_element_type=F32)


def _mixer_body(*refs, tiles_per_seq, n_cast):
    n_in = N_MIXER_IN + n_cast
    n_out = 1 + n_cast
    n_slot = len(Slot._fields)
    ins = refs[:n_in]
    outs = refs[n_in:n_in + n_out]
    scr = refs[n_in + n_out:]
    slot_a, slot_b = Slot(*scr[:n_slot]), Slot(*scr[n_slot:2 * n_slot])
    shared = scr[2 * n_slot:]
    prev_blocks = shared[0:2]
    c_scr, m_scr = shared[-2:]
    s = pl.program_id(0)
    t_cur = lax.rem(s + tiles_per_seq - 1, tiles_per_seq)

    @pl.when(s == 0)
    def _():
        for ref in list(slot_b) + list(shared):
            ref[...] = jnp.zeros_like(ref)

    @pl.when(t_cur == 0)
    def _():
        c_scr[...] = jnp.zeros_like(c_scr)
        m_scr[...] = jnp.zeros_like(m_scr)
        for ref in prev_blocks:
            ref[...] = jnp.zeros_like(ref)

    even = lax.rem(s, 2) == 0

    @pl.when(even)
    def _():
        _mixer_step(t_cur, ins, slot_b, slot_a, shared, outs)

    @pl.when(jnp.logical_not(even))
    def _():
        _mixer_step(t_cur, ins, slot_a, slot_b, shared, outs)


def _cast_slabs(arr, steps):
    rows = arr.shape[0]
    slab = 16
    while rows % slab or rows // slab > steps:
        slab += 16
    n_slabs = rows // slab
    return pl.BlockSpec((slab, arr.shape[1]), lambda i: (jnp.minimum(i, n_slabs - 1), 0))


def _mixer(x, positions, norm_g, b_i, b_f, out_norm_g, sinks, weights, cast_weights):
    b, s, d = x.shape
    assert s % MIX_ROWS == 0 and d == D_MODEL and positions.shape == (b, s)
    w_mix, w_br = weights
    b_if = jnp.concatenate([b_i, b_f]).reshape(GATE_ROWS, 1)
    half = ROPE_DIM // 2
    inv = (ROPE_THETA ** (-jnp.arange(half, dtype=F32) * 2.0 / ROPE_DIM)).reshape(half, 1)

    rows = MIX_ROWS
    tiles_per_seq = s // rows
    n_tiles = b * tiles_per_seq
    def in_tile(i):
        j = jnp.minimum(i, n_tiles - 1)
        return j // tiles_per_seq, j % tiles_per_seq

    def out_tile(i):
        j = jnp.maximum(i - 1, 0)
        return j // tiles_per_seq, j % tiles_per_seq

    in_specs = [
        pl.BlockSpec((1, rows, d), lambda i: (*in_tile(i), 0)),
        pl.BlockSpec((1, 1, rows), lambda i: (in_tile(i)[0], 0, in_tile(i)[1])),
        _const_spec((1, d)), _const_spec((half, 1)), _const_spec((GATE_ROWS, 1)),
        _const_spec((1, MLSTM_V_W)),
        pl.BlockSpec(memory_space=pltpu.SMEM),
        _const_spec((d, MIX_W)), _const_spec((MLSTM_V_W, 2 * d)),
    ]
    slot = list(_slot_shapes(rows, d))
    shared = [
        pltpu.VMEM((WINDOW, LANES), BF16),
        pltpu.VMEM((LANES, WINDOW), BF16),
        pltpu.VMEM((rows, d), F32),
        pltpu.VMEM((rows, d), F32),
        pltpu.VMEM((rows, MLSTM_V_W), BF16),
        pltpu.VMEM((rows, ATTN_Q_W), BF16),
        pltpu.VMEM((rows, d), BF16),
        pltpu.VMEM((MLSTM_HEADS, MLSTM_QK_DIM, VEXT_W), F32),
        pltpu.VMEM((SUBLANES, LANES), F32),
    ]
    assert len(in_specs) == N_MIXER_IN
    steps = n_tiles + 1
    cast_specs = [_cast_slabs(arr, steps) for arr in cast_weights]
    res = pl.pallas_call(
        functools.partial(_mixer_body, tiles_per_seq=tiles_per_seq, n_cast=len(cast_weights)),
        grid=(steps,),
        in_specs=in_specs + cast_specs,
        out_specs=[pl.BlockSpec((1, rows, d), lambda i: (*out_tile(i), 0))] + cast_specs,
        out_shape=[jax.ShapeDtypeStruct((b, s, d), F32)]
        + [jax.ShapeDtypeStruct(arr.shape, BF16) for arr in cast_weights],
        scratch_shapes=slot + slot + shared,
        compiler_params=pltpu.CompilerParams(
            dimension_semantics=("arbitrary",), vmem_limit_bytes=(7 * VMEM_BYTES_V7X) // 8),
        name="mixer",
    )(x, positions.reshape(b, 1, s), norm_g.reshape(1, d), inv, b_if, out_norm_g.reshape(1, MLSTM_V_W),
      sinks, w_mix, w_br, *cast_weights)
    return res[0], res[1:]


def kernel(x, positions, ffn1_norm_g, ffn1_w_gate, ffn1_w_up, ffn1_w_down, mix_norm_g, w_in, mlstm_b_i, mlstm_b_f, mlstm_out_norm_g, attn_sinks, w_branch_mlstm, w_branch_attn, w_out, ffn2_norm_g, ffn2_w_gate, ffn2_w_up, ffn2_w_down, final_norm_g):
    b, s, d = x.shape
    depth = w_in.shape[0]
    for l in range(depth):
        last = l == depth - 1
        x2, mixer_w = _ffn(x.reshape(b * s, d), ffn1_norm_g[l], ffn1_w_gate[l], ffn1_w_up[l], ffn1_w_down[l],
                           mixer_weights=(w_in[l], w_branch_mlstm[l], w_branch_attn[l], w_out[l]))
        x3, ffn2_w = _mixer(x2.reshape(b, s, d), positions, mix_norm_g[l], mlstm_b_i[l], mlstm_b_f[l],
                            mlstm_out_norm_g[l], attn_sinks[l], mixer_w,
                            (ffn2_w_gate[l], ffn2_w_up[l], ffn2_w_down[l]))
        x = _ffn(x3.reshape(b * s, d), ffn2_norm_g[l], *ffn2_w,
                 final_g=final_norm_g if last else None).reshape(b, s, d)
    return x
```

```python
import collections
import functools

import jax
import jax.numpy as jnp
from jax import lax
from jax.experimental import pallas as pl
from jax.experimental.pallas import tpu as pltpu

D_MODEL = 1024
D_FF = 2816
MLSTM_HEADS = 4
MLSTM_QK_DIM = 64
MLSTM_V_DIM = 128
GATE_SOFTCAP = 15.0
ATTN_Q_HEADS = 8
ATTN_KV_HEADS = 2
ATTN_HEAD_DIM = 64
WINDOW = 128
ROPE_DIM = ATTN_HEAD_DIM // 4
ROPE_THETA = 500000.0
NORM_EPS = 1e-6

MLSTM_QK_W = MLSTM_HEADS * MLSTM_QK_DIM
MLSTM_V_W = MLSTM_HEADS * MLSTM_V_DIM
ATTN_Q_W = ATTN_Q_HEADS * ATTN_HEAD_DIM
ATTN_KV_W = ATTN_KV_HEADS * ATTN_HEAD_DIM
IN_WIDTHS = (MLSTM_QK_W, MLSTM_QK_W, MLSTM_V_W, MLSTM_V_W, MLSTM_HEADS, MLSTM_HEADS,
             ATTN_Q_W, ATTN_KV_W, ATTN_KV_W, D_MODEL, D_MODEL)

LANES = 128
SUBLANES = 8
MXU_COLS = 256
VMEM_BYTES_V7X = 64 * 1024 * 1024

FFN_ROWS = 1024
FFN_COL_CHUNK = 2 * MXU_COLS
STAGE_SLOTS = 4
STAGE_ROWS_IN = 64
STAGE_ROWS_OUT = 176
MIX_ROWS = 512
MIX_COL_CHUNK = 2 * MXU_COLS
CHUNK = LANES
PAIR_W = 2 * MLSTM_QK_DIM
VEXT_W = 2 * MLSTM_V_DIM
SUM_ROWS = 16
GATE_ROWS = 2 * MLSTM_HEADS
QKG_W = 2 * MLSTM_QK_W + LANES
N_MIXER_IN = 9
MIX_PIECES = (("qkg", QKG_W), ("v", MLSTM_V_W), ("o", MLSTM_V_W), ("qa", ATTN_Q_W), ("kva", 2 * ATTN_KV_W),
              ("gm", D_MODEL), ("ga", D_MODEL), ("out", D_MODEL))
MIX_W = sum(w for _, w in MIX_PIECES)

F32 = jnp.float32
BF16 = jnp.bfloat16


def _rms(x, g):
    return x * lax.rsqrt(jnp.mean(x * x, axis=-1, keepdims=True) + NORM_EPS) * g


def _col_chunks(total, width):
    return [(c, min(c + width, total)) for c in range(0, total, width)]


def _const_spec(shape):
    return pl.BlockSpec(shape, lambda *_: (0,) * len(shape), pipeline_mode=pl.Buffered(1))


def _mix_piece(ref, name):
    start = 0
    for piece, width in MIX_PIECES:
        if piece == name:
            return ref.at[:, start:start + width]
        start += width
    raise KeyError(name)


def _prep_mixer_weights(win_ref, wbrm_ref, wbra_ref, wout_ref, wmix_o, wbr_o):
    starts = [0]
    for w in IN_WIDTHS:
        starts.append(starts[-1] + w)
    col = lambda i: win_ref[:, starts[i]:starts[i + 1]]
    gate_pad = jnp.zeros((win_ref.shape[0], LANES - GATE_ROWS), F32)
    pieces = dict(
        qkg=jnp.concatenate([col(0) * (MLSTM_QK_DIM ** -0.5), col(1), col(4), col(5), gate_pad], axis=1),
        v=col(2), o=col(3), qa=col(6) * (ATTN_HEAD_DIM ** -0.5),
        kva=jnp.concatenate([col(7), col(8)], axis=1), gm=col(9), ga=col(10), out=wout_ref[...])
    for name, _ in MIX_PIECES:
        _mix_piece(wmix_o, name)[...] = pieces[name].astype(BF16)
    wbr_o[:, 0:D_MODEL] = wbrm_ref[...].astype(BF16)
    wbr_o[:, D_MODEL:2 * D_MODEL] = wbra_ref[...].astype(BF16)


def _stream_cast(src_hbm, dst_ref, stage_ref, sem_ref, chunk_rows):
    n_chunks = src_hbm.shape[0] // chunk_rows
    ahead = STAGE_SLOTS - 1

    def copy(k):
        slot = k % STAGE_SLOTS
        return pltpu.make_async_copy(src_hbm.at[k * chunk_rows:(k + 1) * chunk_rows], stage_ref.at[slot],
                                     sem_ref.at[slot])
    for k in range(min(ahead, n_chunks)):
        copy(k).start()
    for k in range(n_chunks):
        if k + ahead < n_chunks:
            copy(k + ahead).start()
        copy(k).wait()
        dst_ref[k * chunk_rows:(k + 1) * chunk_rows, :] = stage_ref[k % STAGE_SLOTS].astype(BF16)


def _ffn_body(*refs, final_norm, prep, f32_weights):
    n_in = 5 + int(final_norm) + (4 if prep else 0)
    x_ref, g_ref, wg_ref, wu_ref, wd_ref = refs[:5]
    o_ref = refs[n_in]
    act_ref = refs[-1]
    if f32_weights:
        wg_scr, wu_scr, wd_scr, stage_in, stage_out, sems = refs[-7:-1]

        @pl.when(pl.program_id(0) == 0)
        def _():
            _stream_cast(wg_ref, wg_scr, stage_in, sems, STAGE_ROWS_IN)
            _stream_cast(wu_ref, wu_scr, stage_in, sems, STAGE_ROWS_IN)
            _stream_cast(wd_ref, wd_scr, stage_out, sems, STAGE_ROWS_OUT)
        wg_ref, wu_ref, wd_ref = wg_scr, wu_scr, wd_scr
    x = x_ref[...]
    h = _rms(x, g_ref[...]).astype(BF16)
    for c0, c1 in _col_chunks(D_FF, FFN_COL_CHUNK):
        g = jnp.dot(h, wg_ref[:, c0:c1], preferred_element_type=F32)
        u = jnp.dot(h, wu_ref[:, c0:c1], preferred_element_type=F32)
        act_ref[:, c0:c1] = (g * jax.nn.sigmoid(g) * u).astype(BF16)
    if prep:
        _prep_mixer_weights(*refs[n_in - 4:n_in], *refs[n_in + 1:n_in + 3])
    r = x + 0.5 * jnp.dot(act_ref[...], wd_ref[...], preferred_element_type=F32)
    if final_norm:
        r = _rms(r, refs[5][...])
    o_ref[...] = r


def _ffn(x2, norm_g, w_gate, w_up, w_down, final_g=None, mixer_weights=None):
    n, d = x2.shape
    assert n % FFN_ROWS == 0 and d == D_MODEL and w_gate.shape == (d, D_FF) and w_down.shape == (D_FF, d)
    steps = n // FFN_ROWS
    final_norm = final_g is not None
    prep = mixer_weights is not None
    f32_weights = w_gate.dtype == F32
    assert w_up.dtype == w_gate.dtype and w_down.dtype == w_gate.dtype
    row_spec = pl.BlockSpec((FFN_ROWS, d), lambda i: (i, 0))
    weight_specs = ([pl.BlockSpec(memory_space=pl.ANY)] * 3 if f32_weights else
                    [_const_spec((d, D_FF)), _const_spec((d, D_FF)), _const_spec((D_FF, d))])
    in_specs = [row_spec, _const_spec((1, d))] + weight_specs
    args = [x2, norm_g.reshape(1, d), w_gate, w_up, w_down]
    out_specs = [row_spec]
    out_shape = [jax.ShapeDtypeStruct((n, d), F32)]
    if final_norm:
        in_specs.append(_const_spec((1, d)))
        args.append(final_g.reshape(1, d))
    if prep:
        slab = lambda arr: pl.BlockSpec((arr.shape[0] // steps, arr.shape[1]), lambda i: (i, 0))
        for arr in mixer_weights:
            in_specs.append(slab(arr))
            args.append(arr)
        packed = [jax.ShapeDtypeStruct((d, MIX_W), BF16), jax.ShapeDtypeStruct((MLSTM_V_W, 2 * d), BF16)]
        out_shape += packed
        out_specs += [slab(p) for p in packed]
    scratch = []
    if f32_weights:
        assert d % STAGE_ROWS_IN == 0 and D_FF % STAGE_ROWS_OUT == 0
        scratch = [pltpu.VMEM((d, D_FF), BF16), pltpu.VMEM((d, D_FF), BF16), pltpu.VMEM((D_FF, d), BF16),
                   pltpu.VMEM((STAGE_SLOTS, STAGE_ROWS_IN, D_FF), F32),
                   pltpu.VMEM((STAGE_SLOTS, STAGE_ROWS_OUT, d), F32),
                   pltpu.SemaphoreType.DMA((STAGE_SLOTS,))]
    scratch.append(pltpu.VMEM((FFN_ROWS, D_FF), BF16))
    vmem_eighths = 7 if f32_weights else 6
    res = pl.pallas_call(
        functools.partial(_ffn_body, final_norm=final_norm, prep=prep, f32_weights=f32_weights),
        grid=(steps,),
        in_specs=in_specs,
        out_specs=out_specs,
        out_shape=out_shape,
        scratch_shapes=scratch,
        compiler_params=pltpu.CompilerParams(
            dimension_semantics=("arbitrary",), vmem_limit_bytes=(vmem_eighths * VMEM_BYTES_V7X) // 8),
        name="ffn_final" if final_norm else "ffn",
    )(*args)
    return (res[0], res[1:]) if prep else res[0]


Slot = collections.namedtuple("Slot", "xs h q kT vext so qaT kc vcT g")


def _slot_shapes(rows, d):
    return Slot(
        xs=pltpu.VMEM((rows, d), F32),
        h=pltpu.VMEM((rows, d), BF16),
        q=pltpu.VMEM((rows, MLSTM_QK_W), BF16),
        kT=pltpu.VMEM((MLSTM_QK_W, rows), F32),
        vext=pltpu.VMEM((rows, MLSTM_HEADS * VEXT_W), BF16),
        so=pltpu.VMEM((rows, MLSTM_V_W), F32),
        qaT=pltpu.VMEM((ATTN_Q_W, rows), BF16),
        kc=pltpu.VMEM((rows, LANES), BF16),
        vcT=pltpu.VMEM((LANES, rows), BF16),
        g=pltpu.VMEM((GATE_ROWS, rows), F32),
    )


def _softcap(a):
    return GATE_SOFTCAP * jnp.tanh(a / GATE_SOFTCAP)


def _log_sigmoid(a):
    return jnp.minimum(a, 0.0) - jnp.log1p(jnp.exp(-jnp.abs(a)))


def _lane_cumsum(a):
    lane = lax.broadcasted_iota(jnp.int32, a.shape, 1)
    d = 1
    while d < LANES:
        a = a + jnp.where(lane >= d, pltpu.roll(a, d, 1), 0.0)
        d *= 2
    return a


def _rope_tables(pos_row, inv_col):
    ang = inv_col * pos_row
    c, s = jnp.cos(ang), jnp.sin(ang)
    one, zero = jnp.ones_like(c), jnp.zeros_like(c)
    per_head = ATTN_HEAD_DIM // SUBLANES

    def tile(first, second, rest):
        rows = []
        for _ in range(LANES // ATTN_HEAD_DIM):
            rows += [first, second] + [rest] * (per_head - 2)
        return jnp.concatenate(rows, axis=0).T
    return tile(c, c, one), tile(zero, s, zero), tile(-s, zero, zero)


def _rope(a, tables):
    cos_t, sin_up, sin_dn = tables
    half = ROPE_DIM // 2
    return a * cos_t + pltpu.roll(a, half, 1) * sin_up + pltpu.roll(a, LANES - half, 1) * sin_dn


def _rep(a, shape):
    return jnp.broadcast_to(a, shape)


def _mixer_step(t_cur, ins, cur, nxt, shared, outs):
    x_ref, pos_ref, ng_ref, inv_ref, bif_ref, ong_ref, sink_ref, wmix_ref, wbr_ref = ins[:N_MIXER_IN]
    wqk_ref, wv_ref, wo_ref, wqa_ref, wkva_ref, wgm_ref, wga_ref, wout_ref = (
        _mix_piece(wmix_ref, name) for name, _ in MIX_PIECES)
    wbrm_ref, wbra_ref = wbr_ref.at[:, 0:D_MODEL], wbr_ref.at[:, D_MODEL:2 * D_MODEL]
    cast_in = ins[N_MIXER_IN:]
    o_ref, cast_out = outs[0], outs[1:]
    pk_scr, pvT_scr, sgm_scr, sga_scr, hm_scr, oa_scr, mg_scr, c_scr, m_scr = shared
    rows = MIX_ROWS
    n_chunks = rows // CHUNK
    n_blocks = rows // WINDOW
    group = ATTN_Q_HEADS // ATTN_KV_HEADS

    shared_vals = {}

    def p_norm():
        x_new = x_ref[0]
        nxt.xs[...] = x_new
        nxt.h[...] = _rms(x_new, ng_ref[...]).astype(BF16)

    def side_cast():
        for src, dst in zip(cast_in, cast_out):
            dst[...] = src[...].astype(BF16)

    def p_qk():
        qkg = jnp.dot(nxt.h[...], wqk_ref[...], preferred_element_type=F32)
        nxt.q[...] = qkg[:, 0:MLSTM_QK_W].astype(BF16)
        nxt.kT[...] = qkg[:, MLSTM_QK_W:2 * MLSTM_QK_W].T
        gates_t = qkg[:, 2 * MLSTM_QK_W:QKG_W].T
        nxt.g[...] = _softcap(gates_t[0:GATE_ROWS] + bif_ref[...])

    def p_v():
        v = jnp.dot(nxt.h[...], wv_ref[...], preferred_element_type=F32)
        for hd in range(MLSTM_HEADS):
            nxt.vext[:, hd * VEXT_W:hd * VEXT_W + MLSTM_V_DIM] = (
                v[:, hd * MLSTM_V_DIM:(hd + 1) * MLSTM_V_DIM].astype(BF16))
            nxt.vext[:, hd * VEXT_W + MLSTM_V_DIM:(hd + 1) * VEXT_W] = jnp.ones((rows, MLSTM_V_DIM), BF16)

    def p_so():
        nxt.so[...] = jax.nn.sigmoid(jnp.dot(nxt.h[...], wo_ref[...], preferred_element_type=F32))

    def p_qa():
        shared_vals["tables"] = _rope_tables(pos_ref[0].astype(F32), inv_ref[...])
        qa = jnp.dot(nxt.h[...], wqa_ref[...], preferred_element_type=F32)
        for p in range(ATTN_Q_W // LANES):
            nxt.qaT[p * LANES:(p + 1) * LANES, :] = _rope(
                qa[:, p * LANES:(p + 1) * LANES], shared_vals["tables"]).T.astype(BF16)

    def p_kva():
        kva = jnp.dot(nxt.h[...], wkva_ref[...], preferred_element_type=F32)
        ka = _rope(kva[:, 0:ATTN_KV_W], shared_vals["tables"])
        va = kva[:, ATTN_KV_W:2 * ATTN_KV_W]
        nxt.kc[...] = ka.astype(BF16)
        nxt.vcT[...] = va.T.astype(BF16)

    def gate_piece(idx):
        per_gate = D_MODEL // MXU_COLS
        gref, wref = (sgm_scr, wgm_ref) if idx < per_gate else (sga_scr, wga_ref)
        c0 = (idx % per_gate) * MXU_COLS
        gref[:, c0:c0 + MXU_COLS] = jax.nn.sigmoid(
            jnp.dot(cur.h[...], wref[:, c0:c0 + MXU_COLS], preferred_element_type=F32))

    qi = lax.broadcasted_iota(jnp.int32, (CHUNK, CHUNK), 0)
    si = lax.broadcasted_iota(jnp.int32, (CHUNK, CHUNK), 1)
    tri = si <= qi
    zeros_c = jnp.zeros((MLSTM_QK_DIM, VEXT_W), BF16)
    ong = ong_ref[...]
    tile_shape = (CHUNK, CHUNK)
    rep_shape = (MLSTM_HEADS, LANES)

    chunk_stats = []
    m_vec = m_scr[0:MLSTM_HEADS, :]
    for c in range(n_chunks):
        g_c = cur.g[:, c * CHUNK:(c + 1) * CHUNK]
        i_r = g_c[0:MLSTM_HEADS]
        logf_r = _log_sigmoid(g_c[MLSTM_HEADS:GATE_ROWS])
        b_r = _lane_cumsum(logf_r)
        b_last = _rep(b_r[:, CHUNK - 1:CHUNK], rep_shape)
        a_r = b_last - b_r + i_r
        a_max = _rep(jnp.max(a_r, axis=1, keepdims=True), rep_shape)
        m_new = jnp.maximum(b_last + m_vec, a_max)
        chunk_stats.append(dict(
            logf_r=logf_r, ib_r=i_r - b_r, m_prev=m_vec,
            decay=jnp.exp(b_last + m_vec - m_new), scale=jnp.exp(a_max - m_new),
            w_loc=jnp.exp(a_r - a_max)))
        m_vec = m_new
    c_state = [c_scr[hd] for hd in range(MLSTM_HEADS)]
    c_next = [None] * MLSTM_HEADS
    s_qk = {}

    def mlstm_scores(c):
        rs = slice(c * CHUNK, (c + 1) * CHUNK)
        for pair in range(MLSTM_HEADS // 2):
            q2 = cur.q[rs, pair * PAIR_W:(pair + 1) * PAIR_W]
            k_even, k_odd = (cur.kT[hd * MLSTM_QK_DIM:(hd + 1) * MLSTM_QK_DIM, rs].astype(BF16)
                             for hd in (2 * pair, 2 * pair + 1))
            zk = jnp.zeros_like(k_even)
            k_both = jnp.concatenate([jnp.concatenate([k_even, zk], axis=1),
                                      jnp.concatenate([zk, k_odd], axis=1)], axis=0)
            s2 = jnp.dot(q2, k_both, preferred_element_type=F32)
            s_qk[c, 2 * pair] = s2[:, 0:CHUNK]
            s_qk[c, 2 * pair + 1] = s2[:, CHUNK:2 * CHUNK]

    def mlstm_state(c):
        rs = slice(c * CHUNK, (c + 1) * CHUNK)
        st = chunk_stats[c]
        for hd in range(MLSTM_HEADS):
            vext = cur.vext[rs, hd * VEXT_W:(hd + 1) * VEXT_W]
            ktw = (cur.kT[hd * MLSTM_QK_DIM:(hd + 1) * MLSTM_QK_DIM, rs] * st["w_loc"][hd:hd + 1]).astype(BF16)
            u = jnp.dot(ktw, vext, preferred_element_type=F32)
            decay = jnp.concatenate([st["decay"][hd:hd + 1]] * 2, axis=1)
            scale = jnp.concatenate([st["scale"][hd:hd + 1]] * 2, axis=1)
            c_next[hd] = decay * c_state[hd] + scale * u

    def mlstm_out(c):
        rs = slice(c * CHUNK, (c + 1) * CHUNK)
        st = chunk_stats[c]
        for hd in range(MLSTM_HEADS):
            pair, odd = divmod(hd, 2)
            q2 = cur.q[rs, pair * PAIR_W:(pair + 1) * PAIR_W]
            vext = cur.vext[rs, hd * VEXT_W:(hd + 1) * VEXT_W]
            b_col = _rep(jnp.sum(jnp.where(tri, st["logf_r"][hd:hd + 1], 0.0), axis=1, keepdims=True),
                         tile_shape)
            log_d = jnp.where(tri, b_col + st["ib_r"][hd:hd + 1], -jnp.inf)
            inter_log = b_col + st["m_prev"][hd:hd + 1]
            m_j = jnp.maximum(inter_log, _rep(jnp.max(log_d, axis=1, keepdims=True), tile_shape))
            sqk = (s_qk.pop((c, hd)) * jnp.exp(log_d - m_j)).astype(BF16)
            c_bf = c_state[hd].astype(BF16)
            c_pad = jnp.concatenate([zeros_c, c_bf] if odd else [c_bf, zeros_c], axis=0)
            q_inter = (q2 * jnp.exp(inter_log - m_j)).astype(BF16)
            tot = jnp.dot(jnp.concatenate([sqk, q_inter], axis=1), jnp.concatenate([vext, c_pad], axis=0),
                          preferred_element_type=F32)
            num = tot[:, 0:MLSTM_V_DIM]
            den = tot[:, MLSTM_V_DIM:VEXT_W]
            hh = num / jnp.maximum(jnp.abs(den), jnp.exp(-m_j))
            vs = slice(hd * MLSTM_V_DIM, (hd + 1) * MLSTM_V_DIM)
            hn = hh * lax.rsqrt(jnp.mean(hh * hh, axis=1, keepdims=True) + NORM_EPS) * ong[:, vs]
            hm_scr[rs, vs] = (hn * cur.so[rs, vs]).astype(BF16)
            c_state[hd] = c_next[hd]

    si2 = lax.broadcasted_iota(jnp.int32, (2 * WINDOW, WINDOW), 0)
    qi2 = lax.broadcasted_iota(jnp.int32, (2 * WINDOW, WINDOW), 1)
    band = (si2 > qi2) & (si2 <= qi2 + WINDOW)
    band_first = band & ((si2 >= WINDOW) | (t_cur > 0))
    zero_q = jnp.zeros((ATTN_HEAD_DIM, group * WINDOW), BF16)
    ones_rows = jnp.ones((SUM_ROWS, 2 * WINDOW), BF16)
    attn = {}

    def attn_scores(nb):
        qs = slice(nb * WINDOW, (nb + 1) * WINDOW)
        mask = band_first if nb == 0 else band
        if nb == 0:
            k_win = jnp.concatenate([pk_scr[...], cur.kc[0:WINDOW, :]], axis=0)
        else:
            k_win = cur.kc[(nb - 1) * WINDOW:(nb + 1) * WINDOW, :]
        for kh in range(ATTN_KV_HEADS):
            q_t = jnp.concatenate([cur.qaT[(kh * group + j) * ATTN_HEAD_DIM:(kh * group + j + 1) * ATTN_HEAD_DIM, qs]
                                   for j in range(group)], axis=1)
            rhs = jnp.concatenate([q_t, zero_q] if kh == 0 else [zero_q, q_t], axis=0)
            s_t = jnp.dot(k_win, rhs, preferred_element_type=F32)
            p_cols, sk_cols = [], []
            for j in range(group):
                sink = sink_ref[kh * group + j]
                sc = jnp.where(mask, s_t[:, j * WINDOW:(j + 1) * WINDOW], -jnp.inf)
                mx = jnp.maximum(jnp.max(sc, axis=0, keepdims=True), sink)
                p_cols.append(jnp.exp(sc - mx).astype(BF16))
                sk_cols.append(jnp.exp(sink - mx))
            attn[nb, kh] = (jnp.concatenate(p_cols, axis=1), jnp.concatenate(sk_cols, axis=1))

    def attn_values(nb):
        qs = slice(nb * WINDOW, (nb + 1) * WINDOW)
        if nb == 0:
            v_win = jnp.concatenate([pvT_scr[...], cur.vcT[:, 0:WINDOW]], axis=1)
        else:
            v_win = cur.vcT[:, (nb - 1) * WINDOW:(nb + 1) * WINDOW]
        for kh in range(ATTN_KV_HEADS):
            probs_t, sink_term = attn.pop((nb, kh))
            lhs = jnp.concatenate([v_win[kh * ATTN_HEAD_DIM:(kh + 1) * ATTN_HEAD_DIM], ones_rows], axis=0)
            o_t = jnp.dot(lhs, probs_t, preferred_element_type=F32)
            on_t = o_t[0:ATTN_HEAD_DIM] / (o_t[ATTN_HEAD_DIM:ATTN_HEAD_DIM + 1] + sink_term)
            for j2 in range(group // 2):
                pair_t = jnp.concatenate([on_t[:, (2 * j2) * WINDOW:(2 * j2 + 1) * WINDOW],
                                          on_t[:, (2 * j2 + 1) * WINDOW:(2 * j2 + 2) * WINDOW]], axis=0)
                pair = kh * (group // 2) + j2
                oa_scr[qs, pair * LANES:(pair + 1) * LANES] = pair_t.T.astype(BF16)

    def branch_attn(idx):
        c0 = idx * MIX_COL_CHUNK
        y_a = jnp.dot(oa_scr[...], wbra_ref[:, c0:c0 + MIX_COL_CHUNK], preferred_element_type=F32)
        sga_scr[:, c0:c0 + MIX_COL_CHUNK] = sga_scr[:, c0:c0 + MIX_COL_CHUNK] * y_a

    gate_piece(0)
    p_norm()
    side_cast()
    attn_scores(0); gate_piece(1); mlstm_scores(0); mlstm_state(0)
    p_qk()
    attn_values(0); gate_piece(2); mlstm_out(0)
    attn_scores(1); gate_piece(3); mlstm_scores(1); mlstm_state(1)
    p_v()
    attn_values(1); gate_piece(4); mlstm_out(1)
    attn_scores(2); p_so(); mlstm_scores(2); mlstm_state(2)
    attn_values(2); gate_piece(5); mlstm_out(2)
    attn_scores(3); p_qa(); mlstm_scores(3); mlstm_state(3)
    attn_values(3); gate_piece(6)
    p_kva()
    mlstm_out(3); gate_piece(7)
    branch_attn(0); branch_attn(1)

    for hd in range(MLSTM_HEADS):
        c_scr[hd] = c_state[hd]
    m_scr[0:MLSTM_HEADS, :] = m_vec
    pk_scr[...] = cur.kc[rows - WINDOW:rows, :]
    pvT_scr[...] = cur.vcT[:, rows - WINDOW:rows]

    for c0, c1 in _col_chunks(D_MODEL, MIX_COL_CHUNK):
        y_m = jnp.dot(hm_scr[...], wbrm_ref[:, c0:c1], preferred_element_type=F32)
        mg_scr[:, c0:c1] = (sgm_scr[:, c0:c1] * y_m + sga_scr[:, c0:c1]).astype(BF16)
    o_ref[0] = cur.xs[...] + jnp.dot(mg_scr[...], wout_ref[...], preferred_element_type=F32)


def _mixer_body(*refs, tiles_per_seq, n_cast):
    n_in = N_MIXER_IN + n_cast
    n_out = 1 + n_cast
    n_slot = len(Slot._fields)
    ins = refs[:n_in]
    outs = refs[n_in:n_in + n_out]
    scr = refs[n_in + n_out:]
    slot_a, slot_b = Slot(*scr[:n_slot]), Slot(*scr[n_slot:2 * n_slot])
    shared = scr[2 * n_slot:]
    prev_blocks = shared[0:2]
    c_scr, m_scr = shared[-2:]
    s = pl.program_id(0)
    t_cur = lax.rem(s + tiles_per_seq - 1, tiles_per_seq)

    @pl.when(s == 0)
    def _():
        for ref in list(slot_b) + list(shared):
            ref[...] = jnp.zeros_like(ref)

    @pl.when(t_cur == 0)
    def _():
        c_scr[...] = jnp.zeros_like(c_scr)
        m_scr[...] = jnp.zeros_like(m_scr)
        for ref in prev_blocks:
            ref[...] = jnp.zeros_like(ref)

    even = lax.rem(s, 2) == 0

    @pl.when(even)
    def _():
        _mixer_step(t_cur, ins, slot_b, slot_a, shared, outs)

    @pl.when(jnp.logical_not(even))
    def _():
        _mixer_step(t_cur, ins, slot_a, slot_b, shared, outs)


def _cast_slabs(arr, steps):
    rows = arr.shape[0]
    slab = 16
    while rows % slab or rows // slab > steps:
        slab += 16
    n_slabs = rows // slab
    return pl.BlockSpec((slab, arr.shape[1]), lambda i: (jnp.minimum(i, n_slabs - 1), 0))


def _mixer(x, positions, norm_g, b_i, b_f, out_norm_g, sinks, weights, cast_weights):
    b, s, d = x.shape
    assert s % MIX_ROWS == 0 and d == D_MODEL and positions.shape == (b, s)
    w_mix, w_br = weights
    b_if = jnp.concatenate([b_i, b_f]).reshape(GATE_ROWS, 1)
    half = ROPE_DIM // 2
    inv = (ROPE_THETA ** (-jnp.arange(half, dtype=F32) * 2.0 / ROPE_DIM)).reshape(half, 1)

    rows = MIX_ROWS
    tiles_per_seq = s // rows
    n_tiles = b * tiles_per_seq
    def in_tile(i):
        j = jnp.minimum(i, n_tiles - 1)
        return j // tiles_per_seq, j % tiles_per_seq

    def out_tile(i):
        j = jnp.maximum(i - 1, 0)
        return j // tiles_per_seq, j % tiles_per_seq

    in_specs = [
        pl.BlockSpec((1, rows, d), lambda i: (*in_tile(i), 0)),
        pl.BlockSpec((1, 1, rows), lambda i: (in_tile(i)[0], 0, in_tile(i)[1])),
        _const_spec((1, d)), _const_spec((half, 1)), _const_spec((GATE_ROWS, 1)),
        _const_spec((1, MLSTM_V_W)),
        pl.BlockSpec(memory_space=pltpu.SMEM),
        _const_spec((d, MIX_W)), _const_spec((MLSTM_V_W, 2 * d)),
    ]
    slot = list(_slot_shapes(rows, d))
    shared = [
        pltpu.VMEM((WINDOW, LANES), BF16),
        pltpu.VMEM((LANES, WINDOW), BF16),
        pltpu.VMEM((rows, d), F32),
        pltpu.VMEM((rows, d), F32),
        pltpu.VMEM((rows, MLSTM_V_W), BF16),
        pltpu.VMEM((rows, ATTN_Q_W), BF16),
        pltpu.VMEM((rows, d), BF16),
        pltpu.VMEM((MLSTM_HEADS, MLSTM_QK_DIM, VEXT_W), F32),
        pltpu.VMEM((SUBLANES, LANES), F32),
    ]
    assert len(in_specs) == N_MIXER_IN
    steps = n_tiles + 1
    cast_specs = [_cast_slabs(arr, steps) for arr in cast_weights]
    res = pl.pallas_call(
        functools.partial(_mixer_body, tiles_per_seq=tiles_per_seq, n_cast=len(cast_weights)),
        grid=(steps,),
        in_specs=in_specs + cast_specs,
        out_specs=[pl.BlockSpec((1, rows, d), lambda i: (*out_tile(i), 0))] + cast_specs,
        out_shape=[jax.ShapeDtypeStruct((b, s, d), F32)]
        + [jax.ShapeDtypeStruct(arr.shape, BF16) for arr in cast_weights],
        scratch_shapes=slot + slot + shared,
        compiler_params=pltpu.CompilerParams(
            dimension_semantics=("arbitrary",), vmem_limit_bytes=(7 * VMEM_BYTES_V7X) // 8),
        name="mixer",
    )(x, positions.reshape(b, 1, s), norm_g.reshape(1, d), inv, b_if, out_norm_g.reshape(1, MLSTM_V_W),
      sinks, w_mix, w_br, *cast_weights)
    return res[0], res[1:]


def kernel(x, positions, ffn1_norm_g, ffn1_w_gate, ffn1_w_up, ffn1_w_down, mix_norm_g, w_in, mlstm_b_i, mlstm_b_f, mlstm_out_norm_g, attn_sinks, w_branch_mlstm, w_branch_attn, w_out, ffn2_norm_g, ffn2_w_gate, ffn2_w_up, ffn2_w_down, final_norm_g):
    b, s, d = x.shape
    depth = w_in.shape[0]
    for l in range(depth):
        last = l == depth - 1
        x2, mixer_w = _ffn(x.reshape(b * s, d), ffn1_norm_g[l], ffn1_w_gate[l], ffn1_w_up[l], ffn1_w_down[l],
                           mixer_weights=(w_in[l], w_branch_mlstm[l], w_branch_attn[l], w_out[l]))
        x3, ffn2_w = _mixer(x2.reshape(b, s, d), positions, mix_norm_g[l], mlstm_b_i[l], mlstm_b_f[l],
                            mlstm_out_norm_g[l], attn_sinks[l], mixer_w,
                            (ffn2_w_gate[l], ffn2_w_up[l], ffn2_w_down[l]))
        x = _ffn(x3.reshape(b * s, d), ffn2_norm_g[l], *ffn2_w,
                 final_g=final_norm_g if last else None).reshape(b, s, d)
    return x
```

```python
import collections
import functools

import jax
import jax.numpy as jnp
from jax import lax
from jax.experimental import pallas as pl
from jax.experimental.pallas import tpu as pltpu

D_MODEL = 1024
D_FF = 2816
MLSTM_HEADS = 4
MLSTM_QK_DIM = 64
MLSTM_V_DIM = 128
GATE_SOFTCAP = 15.0
ATTN_Q_HEADS = 8
ATTN_KV_HEADS = 2
ATTN_HEAD_DIM = 64
WINDOW = 128
ROPE_DIM = ATTN_HEAD_DIM // 4
ROPE_THETA = 500000.0
NORM_EPS = 1e-6

MLSTM_QK_W = MLSTM_HEADS * MLSTM_QK_DIM
MLSTM_V_W = MLSTM_HEADS * MLSTM_V_DIM
ATTN_Q_W = ATTN_Q_HEADS * ATTN_HEAD_DIM
ATTN_KV_W = ATTN_KV_HEADS * ATTN_HEAD_DIM
IN_WIDTHS = (MLSTM_QK_W, MLSTM_QK_W, MLSTM_V_W, MLSTM_V_W, MLSTM_HEADS, MLSTM_HEADS,
             ATTN_Q_W, ATTN_KV_W, ATTN_KV_W, D_MODEL, D_MODEL)

LANES = 128
SUBLANES = 8
MXU_COLS = 256
VMEM_BYTES_V7X = 64 * 1024 * 1024

FFN_ROWS = 1024
FFN_COL_CHUNK = 2 * MXU_COLS
STAGE_SLOTS = 4
STAGE_ROWS_IN = 64
STAGE_ROWS_OUT = 176
MIX_ROWS = 512
MIX_COL_CHUNK = 2 * MXU_COLS
CHUNK = LANES
PAIR_W = 2 * MLSTM_QK_DIM
VEXT_W = 2 * MLSTM_V_DIM
SUM_ROWS = 16
GATE_ROWS = 2 * MLSTM_HEADS
QKG_W = 2 * MLSTM_QK_W + LANES
N_MIXER_IN = 9
PITCH_PAD = LANES
MIX_PIECES = (("qkg", QKG_W), ("v", MLSTM_V_W), ("o", MLSTM_V_W), ("qa", ATTN_Q_W), ("kva", 2 * ATTN_KV_W),
              ("gm", D_MODEL), ("ga", D_MODEL), ("out", D_MODEL))
MIX_W = sum(w for _, w in MIX_PIECES)

F32 = jnp.float32
BF16 = jnp.bfloat16


def _rms(x, g):
    return x * lax.rsqrt(jnp.mean(x * x, axis=-1, keepdims=True) + NORM_EPS) * g


def _col_chunks(total, width):
    return [(c, min(c + width, total)) for c in range(0, total, width)]


def _const_spec(shape):
    return pl.BlockSpec(shape, lambda *_: (0,) * len(shape), pipeline_mode=pl.Buffered(1))


def _mix_piece(ref, name):
    start = 0
    for piece, width in MIX_PIECES:
        if piece == name:
            return ref.at[:, start:start + width]
        start += width
    raise KeyError(name)


def _prep_mixer_weights(win_ref, wbrm_ref, wbra_ref, wout_ref, wmix_o, wbr_o):
    starts = [0]
    for w in IN_WIDTHS:
        starts.append(starts[-1] + w)
    col = lambda i: win_ref[:, starts[i]:starts[i + 1]]
    gate_pad = jnp.zeros((win_ref.shape[0], LANES - GATE_ROWS), F32)
    pieces = dict(
        qkg=jnp.concatenate([col(0) * (MLSTM_QK_DIM ** -0.5), col(1), col(4), col(5), gate_pad], axis=1),
        v=col(2), o=col(3), qa=col(6) * (ATTN_HEAD_DIM ** -0.5),
        kva=jnp.concatenate([col(7), col(8)], axis=1), gm=col(9), ga=col(10), out=wout_ref[...])
    for name, _ in MIX_PIECES:
        _mix_piece(wmix_o, name)[...] = pieces[name].astype(BF16)
    wbr_o[:, 0:D_MODEL] = wbrm_ref[...].astype(BF16)
    wbr_o[:, D_MODEL:2 * D_MODEL] = wbra_ref[...].astype(BF16)


def _stream_cast(src_hbm, dst_ref, stage_ref, sem_ref, chunk_rows):
    n_chunks = src_hbm.shape[0] // chunk_rows
    ahead = STAGE_SLOTS - 1

    def copy(k):
        slot = k % STAGE_SLOTS
        return pltpu.make_async_copy(src_hbm.at[k * chunk_rows:(k + 1) * chunk_rows], stage_ref.at[slot],
                                     sem_ref.at[slot])
    for k in range(min(ahead, n_chunks)):
        copy(k).start()
    for k in range(n_chunks):
        if k + ahead < n_chunks:
            copy(k + ahead).start()
        copy(k).wait()
        dst_ref[k * chunk_rows:(k + 1) * chunk_rows, :] = stage_ref[k % STAGE_SLOTS].astype(BF16)


def _ffn_body(*refs, final_norm, prep, f32_weights):
    n_in = 5 + int(final_norm) + (4 if prep else 0)
    x_ref, g_ref, wg_ref, wu_ref, wd_ref = refs[:5]
    o_ref = refs[n_in]
    act_ref = refs[-1]
    if f32_weights:
        wg_scr, wu_scr, wd_scr, stage_in, stage_out, sems = refs[-7:-1]

        @pl.when(pl.program_id(0) == 0)
        def _():
            _stream_cast(wg_ref, wg_scr, stage_in, sems, STAGE_ROWS_IN)
            _stream_cast(wu_ref, wu_scr, stage_in, sems, STAGE_ROWS_IN)
            _stream_cast(wd_ref, wd_scr, stage_out, sems, STAGE_ROWS_OUT)
        wg_ref, wu_ref, wd_ref = wg_scr, wu_scr, wd_scr
    x = x_ref[...]
    h = _rms(x, g_ref[...]).astype(BF16)
    for c0, c1 in _col_chunks(D_FF, FFN_COL_CHUNK):
        g = jnp.dot(h, wg_ref[:, c0:c1], preferred_element_type=F32)
        u = jnp.dot(h, wu_ref[:, c0:c1], preferred_element_type=F32)
        act_ref[:, c0:c1] = (g * jax.nn.sigmoid(g) * u).astype(BF16)
    if prep:
        _prep_mixer_weights(*refs[n_in - 4:n_in], *refs[n_in + 1:n_in + 3])
    r = x + 0.5 * jnp.dot(act_ref[...], wd_ref[...], preferred_element_type=F32)
    if final_norm:
        r = _rms(r, refs[5][...])
    o_ref[...] = r


def _ffn(x2, norm_g, w_gate, w_up, w_down, final_g=None, mixer_weights=None):
    n, d = x2.shape
    assert n % FFN_ROWS == 0 and d == D_MODEL and w_gate.shape == (d, D_FF) and w_down.shape == (D_FF, d)
    steps = n // FFN_ROWS
    final_norm = final_g is not None
    prep = mixer_weights is not None
    f32_weights = w_gate.dtype == F32
    assert w_up.dtype == w_gate.dtype and w_down.dtype == w_gate.dtype
    row_spec = pl.BlockSpec((FFN_ROWS, d), lambda i: (i, 0))
    weight_specs = ([pl.BlockSpec(memory_space=pl.ANY)] * 3 if f32_weights else
                    [_const_spec((d, D_FF)), _const_spec((d, D_FF)), _const_spec((D_FF, d))])
    in_specs = [row_spec, _const_spec((1, d))] + weight_specs
    args = [x2, norm_g.reshape(1, d), w_gate, w_up, w_down]
    out_specs = [row_spec]
    out_shape = [jax.ShapeDtypeStruct((n, d), F32)]
    if final_norm:
        in_specs.append(_const_spec((1, d)))
        args.append(final_g.reshape(1, d))
    if prep:
        slab = lambda arr: pl.BlockSpec((arr.shape[0] // steps, arr.shape[1]), lambda i: (i, 0))
        for arr in mixer_weights:
            in_specs.append(slab(arr))
            args.append(arr)
        packed = [jax.ShapeDtypeStruct((d, MIX_W), BF16), jax.ShapeDtypeStruct((MLSTM_V_W, 2 * d), BF16)]
        out_shape += packed
        out_specs += [slab(p) for p in packed]
    scratch = []
    if f32_weights:
        assert d % STAGE_ROWS_IN == 0 and D_FF % STAGE_ROWS_OUT == 0
        scratch = [pltpu.VMEM((d, D_FF), BF16), pltpu.VMEM((d, D_FF), BF16), pltpu.VMEM((D_FF, d), BF16),
                   pltpu.VMEM((STAGE_SLOTS, STAGE_ROWS_IN, D_FF), F32),
                   pltpu.VMEM((STAGE_SLOTS, STAGE_ROWS_OUT, d), F32),
                   pltpu.SemaphoreType.DMA((STAGE_SLOTS,))]
    scratch.append(pltpu.VMEM((FFN_ROWS, D_FF), BF16))
    vmem_eighths = 7 if f32_weights else 6
    res = pl.pallas_call(
        functools.partial(_ffn_body, final_norm=final_norm, prep=prep, f32_weights=f32_weights),
        grid=(steps,),
        in_specs=in_specs,
        out_specs=out_specs,
        out_shape=out_shape,
        scratch_shapes=scratch,
        compiler_params=pltpu.CompilerParams(
            dimension_semantics=("arbitrary",), vmem_limit_bytes=(vmem_eighths * VMEM_BYTES_V7X) // 8),
        name="ffn_final" if final_norm else "ffn",
    )(*args)
    return (res[0], res[1:]) if prep else res[0]


Slot = collections.namedtuple("Slot", "xs h q kT vext so qaT kc vcT g")


def _slot_shapes(rows, d):
    return Slot(
        xs=pltpu.VMEM((rows, d), F32),
        h=pltpu.VMEM((rows, d), BF16),
        q=pltpu.VMEM((rows, MLSTM_QK_W + PITCH_PAD), BF16),
        kT=pltpu.VMEM((MLSTM_QK_W, rows + PITCH_PAD), F32),
        vext=pltpu.VMEM((rows, MLSTM_HEADS * VEXT_W + PITCH_PAD), BF16),
        so=pltpu.VMEM((rows, MLSTM_V_W + PITCH_PAD), F32),
        qaT=pltpu.VMEM((ATTN_Q_W, rows + PITCH_PAD), BF16),
        kc=pltpu.VMEM((rows, LANES), BF16),
        vcT=pltpu.VMEM((LANES, rows + PITCH_PAD), BF16),
        g=pltpu.VMEM((GATE_ROWS, rows), F32),
    )


def _softcap(a):
    return GATE_SOFTCAP * jnp.tanh(a / GATE_SOFTCAP)


def _log_sigmoid(a):
    return jnp.minimum(a, 0.0) - jnp.log1p(jnp.exp(-jnp.abs(a)))


def _lane_cumsum(a):
    lane = lax.broadcasted_iota(jnp.int32, a.shape, 1)
    d = 1
    while d < LANES:
        a = a + jnp.where(lane >= d, pltpu.roll(a, d, 1), 0.0)
        d *= 2
    return a


def _rope_tables(pos_row, inv_col):
    ang = inv_col * pos_row
    c, s = jnp.cos(ang), jnp.sin(ang)
    one, zero = jnp.ones_like(c), jnp.zeros_like(c)
    per_head = ATTN_HEAD_DIM // SUBLANES

    def tile(first, second, rest):
        rows = []
        for _ in range(LANES // ATTN_HEAD_DIM):
            rows += [first, second] + [rest] * (per_head - 2)
        return jnp.concatenate(rows, axis=0).T
    return tile(c, c, one), tile(zero, s, zero), tile(-s, zero, zero)


def _rope(a, tables):
    cos_t, sin_up, sin_dn = tables
    half = ROPE_DIM // 2
    return a * cos_t + pltpu.roll(a, half, 1) * sin_up + pltpu.roll(a, LANES - half, 1) * sin_dn


def _rep(a, shape):
    return jnp.broadcast_to(a, shape)


def _mixer_step(t_cur, ins, cur, nxt, shared, outs):
    x_ref, pos_ref, ng_ref, inv_ref, bif_ref, ong_ref, sink_ref, wmix_ref, wbr_ref = ins[:N_MIXER_IN]
    wqk_ref, wv_ref, wo_ref, wqa_ref, wkva_ref, wgm_ref, wga_ref, wout_ref = (
        _mix_piece(wmix_ref, name) for name, _ in MIX_PIECES)
    wbrm_ref, wbra_ref = wbr_ref.at[:, 0:D_MODEL], wbr_ref.at[:, D_MODEL:2 * D_MODEL]
    cast_in = ins[N_MIXER_IN:]
    o_ref, cast_out = outs[0], outs[1:]
    pk_scr, pvT_scr, sgm_scr, sga_scr, hm_scr, oa_scr, mg_scr, c_scr, m_scr = shared
    rows = MIX_ROWS
    n_chunks = rows // CHUNK
    n_blocks = rows // WINDOW
    group = ATTN_Q_HEADS // ATTN_KV_HEADS

    shared_vals = {}

    def p_norm():
        x_new = x_ref[0]
        nxt.xs[...] = x_new
        nxt.h[...] = _rms(x_new, ng_ref[...]).astype(BF16)

    def side_cast():
        for src, dst in zip(cast_in, cast_out):
            dst[...] = src[...].astype(BF16)

    def p_qk():
        qkg = jnp.dot(nxt.h[...], wqk_ref[...], preferred_element_type=F32)
        nxt.q[:, 0:MLSTM_QK_W] = qkg[:, 0:MLSTM_QK_W].astype(BF16)
        nxt.kT[:, 0:rows] = qkg[:, MLSTM_QK_W:2 * MLSTM_QK_W].T
        gates_t = qkg[:, 2 * MLSTM_QK_W:QKG_W].T
        nxt.g[...] = _softcap(gates_t[0:GATE_ROWS] + bif_ref[...])

    def p_v():
        v = jnp.dot(nxt.h[...], wv_ref[...], preferred_element_type=F32)
        for hd in range(MLSTM_HEADS):
            nxt.vext[:, hd * VEXT_W:hd * VEXT_W + MLSTM_V_DIM] = (
                v[:, hd * MLSTM_V_DIM:(hd + 1) * MLSTM_V_DIM].astype(BF16))
            nxt.vext[:, hd * VEXT_W + MLSTM_V_DIM:(hd + 1) * VEXT_W] = jnp.ones((rows, MLSTM_V_DIM), BF16)

    def p_so():
        nxt.so[:, 0:MLSTM_V_W] = jax.nn.sigmoid(jnp.dot(nxt.h[...], wo_ref[...], preferred_element_type=F32))

    def p_qa():
        shared_vals["tables"] = _rope_tables(pos_ref[0].astype(F32), inv_ref[...])
        qa = jnp.dot(nxt.h[...], wqa_ref[...], preferred_element_type=F32)
        for r0 in range(0, rows, LANES):
            tables_blk = tuple(tab[r0:r0 + LANES] for tab in shared_vals["tables"])
            for p in range(ATTN_Q_W // LANES):
                nxt.qaT[p * LANES:(p + 1) * LANES, r0:r0 + LANES] = _rope(
                    qa[r0:r0 + LANES, p * LANES:(p + 1) * LANES], tables_blk).T.astype(BF16)

    def p_kva():
        kva = jnp.dot(nxt.h[...], wkva_ref[...], preferred_element_type=F32)
        ka = _rope(kva[:, 0:ATTN_KV_W], shared_vals["tables"])
        va = kva[:, ATTN_KV_W:2 * ATTN_KV_W]
        nxt.kc[...] = ka.astype(BF16)
        nxt.vcT[:, 0:rows] = va.T.astype(BF16)

    def gate_piece(idx):
        per_gate = D_MODEL // MXU_COLS
        gref, wref = (sgm_scr, wgm_ref) if idx < per_gate else (sga_scr, wga_ref)
        c0 = (idx % per_gate) * MXU_COLS
        gref[:, c0:c0 + MXU_COLS] = jax.nn.sigmoid(
            jnp.dot(cur.h[...], wref[:, c0:c0 + MXU_COLS], preferred_element_type=F32))

    qi = lax.broadcasted_iota(jnp.int32, (CHUNK, CHUNK), 0)
    si = lax.broadcasted_iota(jnp.int32, (CHUNK, CHUNK), 1)
    tri = si <= qi
    zeros_c = jnp.zeros((MLSTM_QK_DIM, VEXT_W), BF16)
    ong = ong_ref[...]
    tile_shape = (CHUNK, CHUNK)
    rep_shape = (MLSTM_HEADS, LANES)

    chunk_stats = []
    m_vec = m_scr[0:MLSTM_HEADS, :]
    for c in range(n_chunks):
        g_c = cur.g[:, c * CHUNK:(c + 1) * CHUNK]
        i_r = g_c[0:MLSTM_HEADS]
        logf_r = _log_sigmoid(g_c[MLSTM_HEADS:GATE_ROWS])
        b_r = _lane_cumsum(logf_r)
        b_last = _rep(b_r[:, CHUNK - 1:CHUNK], rep_shape)
        a_r = b_last - b_r + i_r
        a_max = _rep(jnp.max(a_r, axis=1, keepdims=True), rep_shape)
        m_new = jnp.maximum(b_last + m_vec, a_max)
        chunk_stats.append(dict(
            logf_r=logf_r, ib_r=i_r - b_r, m_prev=m_vec,
            decay=jnp.exp(b_last + m_vec - m_new), scale=jnp.exp(a_max - m_new),
            w_loc=jnp.exp(a_r - a_max)))
        m_vec = m_new
    c_state = [c_scr[hd] for hd in range(MLSTM_HEADS)]
    c_next = [None] * MLSTM_HEADS
    s_qk = {}

    def mlstm_scores(c):
        rs = slice(c * CHUNK, (c + 1) * CHUNK)
        for pair in range(MLSTM_HEADS // 2):
            q2 = cur.q[rs, pair * PAIR_W:(pair + 1) * PAIR_W]
            k_even, k_odd = (cur.kT[hd * MLSTM_QK_DIM:(hd + 1) * MLSTM_QK_DIM, rs].astype(BF16)
                             for hd in (2 * pair, 2 * pair + 1))
            zk = jnp.zeros_like(k_even)
            k_both = jnp.concatenate([jnp.concatenate([k_even, zk], axis=1),
                                      jnp.concatenate([zk, k_odd], axis=1)], axis=0)
            s2 = jnp.dot(q2, k_both, preferred_element_type=F32)
            s_qk[c, 2 * pair] = s2[:, 0:CHUNK]
            s_qk[c, 2 * pair + 1] = s2[:, CHUNK:2 * CHUNK]

    def mlstm_state(c):
        rs = slice(c * CHUNK, (c + 1) * CHUNK)
        st = chunk_stats[c]
        for hd in range(MLSTM_HEADS):
            vext = cur.vext[rs, hd * VEXT_W:(hd + 1) * VEXT_W]
            ktw = (cur.kT[hd * MLSTM_QK_DIM:(hd + 1) * MLSTM_QK_DIM, rs] * st["w_loc"][hd:hd + 1]).astype(BF16)
            u = jnp.dot(ktw, vext, preferred_element_type=F32)
            decay = jnp.concatenate([st["decay"][hd:hd + 1]] * 2, axis=1)
            scale = jnp.concatenate([st["scale"][hd:hd + 1]] * 2, axis=1)
            c_next[hd] = decay * c_state[hd] + scale * u

    def mlstm_out(c):
        rs = slice(c * CHUNK, (c + 1) * CHUNK)
        st = chunk_stats[c]
        for hd in range(MLSTM_HEADS):
            pair, odd = divmod(hd, 2)
            q2 = cur.q[rs, pair * PAIR_W:(pair + 1) * PAIR_W]
            vext = cur.vext[rs, hd * VEXT_W:(hd + 1) * VEXT_W]
            b_col = _rep(jnp.sum(jnp.where(tri, st["logf_r"][hd:hd + 1], 0.0), axis=1, keepdims=True),
                         tile_shape)
            log_d = jnp.where(tri, b_col + st["ib_r"][hd:hd + 1], -jnp.inf)
            inter_log = b_col + st["m_prev"][hd:hd + 1]
            m_j = jnp.maximum(inter_log, _rep(jnp.max(log_d, axis=1, keepdims=True), tile_shape))
            sqk = (s_qk.pop((c, hd)) * jnp.exp(log_d - m_j)).astype(BF16)
            c_bf = c_state[hd].astype(BF16)
            c_pad = jnp.concatenate([zeros_c, c_bf] if odd else [c_bf, zeros_c], axis=0)
            q_inter = (q2 * jnp.exp(inter_log - m_j)).astype(BF16)
            tot = jnp.dot(jnp.concatenate([sqk, q_inter], axis=1), jnp.concatenate([vext, c_pad], axis=0),
                          preferred_element_type=F32)
            num = tot[:, 0:MLSTM_V_DIM]
            den = tot[:, MLSTM_V_DIM:VEXT_W]
            hh = num / jnp.maximum(jnp.abs(den), jnp.exp(-m_j))
            vs = slice(hd * MLSTM_V_DIM, (hd + 1) * MLSTM_V_DIM)
            hn = hh * lax.rsqrt(jnp.mean(hh * hh, axis=1, keepdims=True) + NORM_EPS) * ong[:, vs]
            hm_scr[rs, vs] = (hn * cur.so[rs, vs]).astype(BF16)
            c_state[hd] = c_next[hd]

    si2 = lax.broadcasted_iota(jnp.int32, (2 * WINDOW, WINDOW), 0)
    qi2 = lax.broadcasted_iota(jnp.int32, (2 * WINDOW, WINDOW), 1)
    band = (si2 > qi2) & (si2 <= qi2 + WINDOW)
    band_first = band & ((si2 >= WINDOW) | (t_cur > 0))
    zero_q = jnp.zeros((ATTN_HEAD_DIM, group * WINDOW), BF16)
    ones_rows = jnp.ones((SUM_ROWS, 2 * WINDOW), BF16)
    attn = {}

    def attn_scores(nb):
        qs = slice(nb * WINDOW, (nb + 1) * WINDOW)
        mask = band_first if nb == 0 else band
        if nb == 0:
            k_win = jnp.concatenate([pk_scr[...], cur.kc[0:WINDOW, :]], axis=0)
        else:
            k_win = cur.kc[(nb - 1) * WINDOW:(nb + 1) * WINDOW, :]
        for kh in range(ATTN_KV_HEADS):
            q_t = jnp.concatenate([cur.qaT[(kh * group + j) * ATTN_HEAD_DIM:(kh * group + j + 1) * ATTN_HEAD_DIM, qs]
                                   for j in range(group)], axis=1)
            rhs = jnp.concatenate([q_t, zero_q] if kh == 0 else [zero_q, q_t], axis=0)
            s_t = jnp.dot(k_win, rhs, preferred_element_type=F32)
            p_cols, sk_cols = [], []
            for j in range(group):
                sink = sink_ref[kh * group + j]
                sc = jnp.where(mask, s_t[:, j * WINDOW:(j + 1) * WINDOW], -jnp.inf)
                mx = jnp.maximum(jnp.max(sc, axis=0, keepdims=True), sink)
                p_cols.append(jnp.exp(sc - mx).astype(BF16))
                sk_cols.append(jnp.exp(sink - mx))
            attn[nb, kh] = (jnp.concatenate(p_cols, axis=1), jnp.concatenate(sk_cols, axis=1))

    def attn_values(nb):
        qs = slice(nb * WINDOW, (nb + 1) * WINDOW)
        if nb == 0:
            v_win = jnp.concatenate([pvT_scr[...], cur.vcT[:, 0:WINDOW]], axis=1)
        else:
            v_win = cur.vcT[:, (nb - 1) * WINDOW:(nb + 1) * WINDOW]
        for kh in range(ATTN_KV_HEADS):
            probs_t, sink_term = attn.pop((nb, kh))
            lhs = jnp.concatenate([v_win[kh * ATTN_HEAD_DIM:(kh + 1) * ATTN_HEAD_DIM], ones_rows], axis=0)
            o_t = jnp.dot(lhs, probs_t, preferred_element_type=F32)
            on_t = o_t[0:ATTN_HEAD_DIM] / (o_t[ATTN_HEAD_DIM:ATTN_HEAD_DIM + 1] + sink_term)
            for j2 in range(group // 2):
                pair_t = jnp.concatenate([on_t[:, (2 * j2) * WINDOW:(2 * j2 + 1) * WINDOW],
                                          on_t[:, (2 * j2 + 1) * WINDOW:(2 * j2 + 2) * WINDOW]], axis=0)
                pair = kh * (group // 2) + j2
                oa_scr[qs, pair * LANES:(pair + 1) * LANES] = pair_t.T.astype(BF16)

    def branch_attn(idx):
        c0 = idx * MIX_COL_CHUNK
        y_a = jnp.dot(oa_scr[:, 0:ATTN_Q_W], wbra_ref[:, c0:c0 + MIX_COL_CHUNK], preferred_element_type=F32)
        sga_scr[:, c0:c0 + MIX_COL_CHUNK] = sga_scr[:, c0:c0 + MIX_COL_CHUNK] * y_a

    gate_piece(0)
    p_norm()
    side_cast()
    attn_scores(0); gate_piece(1); mlstm_scores(0); mlstm_state(0)
    p_qk()
    attn_values(0); gate_piece(2); mlstm_out(0)
    attn_scores(1); gate_piece(3); mlstm_scores(1); mlstm_state(1)
    p_v()
    attn_values(1); gate_piece(4); mlstm_out(1)
    attn_scores(2); p_so(); mlstm_scores(2); mlstm_state(2)
    attn_values(2); gate_piece(5); mlstm_out(2)
    attn_scores(3); p_qa(); mlstm_scores(3); mlstm_state(3)
    attn_values(3); gate_piece(6)
    p_kva()
    mlstm_out(3); gate_piece(7)
    branch_attn(0); branch_attn(1)

    for hd in range(MLSTM_HEADS):
        c_scr[hd] = c_state[hd]
    m_scr[0:MLSTM_HEADS, :] = m_vec
    pk_scr[...] = cur.kc[rows - WINDOW:rows, :]
    pvT_scr[...] = cur.vcT[:, rows - WINDOW:rows]

    for c0, c1 in _col_chunks(D_MODEL, MIX_COL_CHUNK):
        y_m = jnp.dot(hm_scr[:, 0:MLSTM_V_W], wbrm_ref[:, c0:c1], preferred_element_type=F32)
        mg_scr[:, c0:c1] = (sgm_scr[:, c0:c1] * y_m + sga_scr[:, c0:c1]).astype(BF16)
    o_ref[0] = cur.xs[...] + jnp.dot(mg_scr[...], wout_ref[...], preferred_element_type=F32)


def _mixer_body(*refs, tiles_per_seq, n_cast):
    n_in = N_MIXER_IN + n_cast
    n_out = 1 + n_cast
    n_slot = len(Slot._fields)
    ins = refs[:n_in]
    outs = refs[n_in:n_in + n_out]
    scr = refs[n_in + n_out:]
    slot_a, slot_b = Slot(*scr[:n_slot]), Slot(*scr[n_slot:2 * n_slot])
    shared = scr[2 * n_slot:]
    prev_blocks = shared[0:2]
    c_scr, m_scr = shared[-2:]
    s = pl.program_id(0)
    t_cur = lax.rem(s + tiles_per_seq - 1, tiles_per_seq)

    @pl.when(s == 0)
    def _():
        for ref in list(slot_b) + list(shared):
            ref[...] = jnp.zeros_like(ref)

    @pl.when(t_cur == 0)
    def _():
        c_scr[...] = jnp.zeros_like(c_scr)
        m_scr[...] = jnp.zeros_like(m_scr)
        for ref in prev_blocks:
            ref[...] = jnp.zeros_like(ref)

    even = lax.rem(s, 2) == 0

    @pl.when(even)
    def _():
        _mixer_step(t_cur, ins, slot_b, slot_a, shared, outs)

    @pl.when(jnp.logical_not(even))
    def _():
        _mixer_step(t_cur, ins, slot_a, slot_b, shared, outs)


def _cast_slabs(arr, steps):
    rows = arr.shape[0]
    slab = 16
    while rows % slab or rows // slab > steps:
        slab += 16
    n_slabs = rows // slab
    return pl.BlockSpec((slab, arr.shape[1]), lambda i: (jnp.minimum(i, n_slabs - 1), 0))


def _mixer(x, positions, norm_g, b_i, b_f, out_norm_g, sinks, weights, cast_weights):
    b, s, d = x.shape
    assert s % MIX_ROWS == 0 and d == D_MODEL and positions.shape == (b, s)
    w_mix, w_br = weights
    b_if = jnp.concatenate([b_i, b_f]).reshape(GATE_ROWS, 1)
    half = ROPE_DIM // 2
    inv = (ROPE_THETA ** (-jnp.arange(half, dtype=F32) * 2.0 / ROPE_DIM)).reshape(half, 1)

    rows = MIX_ROWS
    tiles_per_seq = s // rows
    n_tiles = b * tiles_per_seq
    def in_tile(i):
        j = jnp.minimum(i, n_tiles - 1)
        return j // tiles_per_seq, j % tiles_per_seq

    def out_tile(i):
        j = jnp.maximum(i - 1, 0)
        return j // tiles_per_seq, j % tiles_per_seq

    in_specs = [
        pl.BlockSpec((1, rows, d), lambda i: (*in_tile(i), 0)),
        pl.BlockSpec((1, 1, rows), lambda i: (in_tile(i)[0], 0, in_tile(i)[1])),
        _const_spec((1, d)), _const_spec((half, 1)), _const_spec((GATE_ROWS, 1)),
        _const_spec((1, MLSTM_V_W)),
        pl.BlockSpec(memory_space=pltpu.SMEM),
        _const_spec((d, MIX_W)), _const_spec((MLSTM_V_W, 2 * d)),
    ]
    slot = list(_slot_shapes(rows, d))
    shared = [
        pltpu.VMEM((WINDOW, LANES), BF16),
        pltpu.VMEM((LANES, WINDOW), BF16),
        pltpu.VMEM((rows, d), F32),
        pltpu.VMEM((rows, d), F32),
        pltpu.VMEM((rows, MLSTM_V_W + PITCH_PAD), BF16),
        pltpu.VMEM((rows, ATTN_Q_W + PITCH_PAD), BF16),
        pltpu.VMEM((rows, d), BF16),
        pltpu.VMEM((MLSTM_HEADS, MLSTM_QK_DIM, VEXT_W), F32),
        pltpu.VMEM((SUBLANES, LANES), F32),
    ]
    assert len(in_specs) == N_MIXER_IN
    steps = n_tiles + 1
    cast_specs = [_cast_slabs(arr, steps) for arr in cast_weights]
    res = pl.pallas_call(
        functools.partial(_mixer_body, tiles_per_seq=tiles_per_seq, n_cast=len(cast_weights)),
        grid=(steps,),
        in_specs=in_specs + cast_specs,
        out_specs=[pl.BlockSpec((1, rows, d), lambda i: (*out_tile(i), 0))] + cast_specs,
        out_shape=[jax.ShapeDtypeStruct((b, s, d), F32)]
        + [jax.ShapeDtypeStruct(arr.shape, BF16) for arr in cast_weights],
        scratch_shapes=slot + slot + shared,
        compiler_params=pltpu.CompilerParams(
            dimension_semantics=("arbitrary",), vmem_limit_bytes=(7 * VMEM_BYTES_V7X) // 8),
        name="mixer",
    )(x, positions.reshape(b, 1, s), norm_g.reshape(1, d), inv, b_if, out_norm_g.reshape(1, MLSTM_V_W),
      sinks, w_mix, w_br, *cast_weights)
    return res[0], res[1:]


def kernel(x, positions, ffn1_norm_g, ffn1_w_gate, ffn1_w_up, ffn1_w_down, mix_norm_g, w_in, mlstm_b_i, mlstm_b_f, mlstm_out_norm_g, attn_sinks, w_branch_mlstm, w_branch_attn, w_out, ffn2_norm_g, ffn2_w_gate, ffn2_w_up, ffn2_w_down, final_norm_g):
    b, s, d = x.shape
    depth = w_in.shape[0]
    for l in range(depth):
        last = l == depth - 1
        x2, mixer_w = _ffn(x.reshape(b * s, d), ffn1_norm_g[l], ffn1_w_gate[l], ffn1_w_up[l], ffn1_w_down[l],
                           mixer_weights=(w_in[l], w_branch_mlstm[l], w_branch_attn[l], w_out[l]))
        x3, ffn2_w = _mixer(x2.reshape(b, s, d), positions, mix_norm_g[l], mlstm_b_i[l], mlstm_b_f[l],
                            mlstm_out_norm_g[l], attn_sinks[l], mixer_w,
                            (ffn2_w_gate[l], ffn2_w_up[l], ffn2_w_down[l]))
        x = _ffn(x3.reshape(b * s, d), ffn2_norm_g[l], *ffn2_w,
                 final_g=final_norm_g if last else None).reshape(b, s, d)
    return x
```

```python
import collections
import functools

import jax
import jax.numpy as jnp
from jax import lax
from jax.experimental import pallas as pl
from jax.experimental.pallas import tpu as pltpu

D_MODEL = 1024
D_FF = 2816
MLSTM_HEADS = 4
MLSTM_QK_DIM = 64
MLSTM_V_DIM = 128
GATE_SOFTCAP = 15.0
ATTN_Q_HEADS = 8
ATTN_KV_HEADS = 2
ATTN_HEAD_DIM = 64
WINDOW = 128
ROPE_DIM = ATTN_HEAD_DIM // 4
ROPE_THETA = 500000.0
NORM_EPS = 1e-6

MLSTM_QK_W = MLSTM_HEADS * MLSTM_QK_DIM
MLSTM_V_W = MLSTM_HEADS * MLSTM_V_DIM
ATTN_Q_W = ATTN_Q_HEADS * ATTN_HEAD_DIM
ATTN_KV_W = ATTN_KV_HEADS * ATTN_HEAD_DIM
IN_WIDTHS = (MLSTM_QK_W, MLSTM_QK_W, MLSTM_V_W, MLSTM_V_W, MLSTM_HEADS, MLSTM_HEADS,
             ATTN_Q_W, ATTN_KV_W, ATTN_KV_W, D_MODEL, D_MODEL)

LANES = 128
SUBLANES = 8
MXU_COLS = 256
VMEM_BYTES_V7X = 64 * 1024 * 1024

FFN_ROWS = 1024
FFN_COL_CHUNK = 2 * MXU_COLS
STAGE_SLOTS = 4
STAGE_ROWS_IN = 64
STAGE_ROWS_OUT = 176
MIX_ROWS = 512
MIX_COL_CHUNK = 2 * MXU_COLS
CHUNK = LANES
PAIR_W = 2 * MLSTM_QK_DIM
VEXT_W = 2 * MLSTM_V_DIM
SUM_ROWS = 16
GATE_ROWS = 2 * MLSTM_HEADS
QKG_W = 2 * MLSTM_QK_W + LANES
N_MIXER_IN = 9
PITCH_PAD = LANES
MIX_PIECES = (("qkg", QKG_W), ("v", MLSTM_V_W), ("o", MLSTM_V_W), ("qa", ATTN_Q_W), ("kva", 2 * ATTN_KV_W),
              ("gm", D_MODEL), ("ga", D_MODEL), ("out", D_MODEL))
MIX_W = sum(w for _, w in MIX_PIECES)

F32 = jnp.float32
BF16 = jnp.bfloat16


def _rms(x, g):
    return x * lax.rsqrt(jnp.mean(x * x, axis=-1, keepdims=True) + NORM_EPS) * g


def _col_chunks(total, width):
    return [(c, min(c + width, total)) for c in range(0, total, width)]


def _const_spec(shape):
    return pl.BlockSpec(shape, lambda *_: (0,) * len(shape), pipeline_mode=pl.Buffered(1))


def _mix_piece(ref, name):
    start = 0
    for piece, width in MIX_PIECES:
        if piece == name:
            return ref.at[:, start:start + width]
        start += width
    raise KeyError(name)


def _prep_mixer_weights(win_ref, wbrm_ref, wbra_ref, wout_ref, wmix_o, wbr_o):
    starts = [0]
    for w in IN_WIDTHS:
        starts.append(starts[-1] + w)
    col = lambda i: win_ref[:, starts[i]:starts[i + 1]]
    gate_pad = jnp.zeros((win_ref.shape[0], LANES - GATE_ROWS), F32)
    pieces = dict(
        qkg=jnp.concatenate([col(0) * (MLSTM_QK_DIM ** -0.5), col(1), col(4), col(5), gate_pad], axis=1),
        v=col(2), o=col(3), qa=col(6) * (ATTN_HEAD_DIM ** -0.5),
        kva=jnp.concatenate([col(7), col(8)], axis=1), gm=col(9), ga=col(10), out=wout_ref[...])
    for name, _ in MIX_PIECES:
        _mix_piece(wmix_o, name)[...] = pieces[name].astype(BF16)
    wbr_o[:, 0:D_MODEL] = wbrm_ref[...].astype(BF16)
    wbr_o[:, D_MODEL:2 * D_MODEL] = wbra_ref[...].astype(BF16)


def _stream_cast(src_hbm, dst_ref, stage_ref, sem_ref, chunk_rows):
    n_chunks = src_hbm.shape[0] // chunk_rows
    ahead = STAGE_SLOTS - 1

    def copy(k):
        slot = k % STAGE_SLOTS
        return pltpu.make_async_copy(src_hbm.at[k * chunk_rows:(k + 1) * chunk_rows], stage_ref.at[slot],
                                     sem_ref.at[slot])
    for k in range(min(ahead, n_chunks)):
        copy(k).start()
    for k in range(n_chunks):
        if k + ahead < n_chunks:
            copy(k + ahead).start()
        copy(k).wait()
        dst_ref[k * chunk_rows:(k + 1) * chunk_rows, :] = stage_ref[k % STAGE_SLOTS].astype(BF16)


def _ffn_body(*refs, final_norm, prep, f32_weights):
    n_in = 5 + int(final_norm) + (4 if prep else 0)
    x_ref, g_ref, wg_ref, wu_ref, wd_ref = refs[:5]
    o_ref = refs[n_in]
    act_ref = refs[-1]
    if f32_weights:
        wg_scr, wu_scr, wd_scr, stage_in, stage_out, sems = refs[-7:-1]

        @pl.when(pl.program_id(0) == 0)
        def _():
            _stream_cast(wg_ref, wg_scr, stage_in, sems, STAGE_ROWS_IN)
            _stream_cast(wu_ref, wu_scr, stage_in, sems, STAGE_ROWS_IN)
            _stream_cast(wd_ref, wd_scr, stage_out, sems, STAGE_ROWS_OUT)
        wg_ref, wu_ref, wd_ref = wg_scr, wu_scr, wd_scr
    x = x_ref[...]
    h = _rms(x, g_ref[...]).astype(BF16)
    for c0, c1 in _col_chunks(D_FF, FFN_COL_CHUNK):
        g = jnp.dot(h, wg_ref[:, c0:c1], preferred_element_type=F32)
        u = jnp.dot(h, wu_ref[:, c0:c1], preferred_element_type=F32)
        act_ref[:, c0:c1] = (g * jax.nn.sigmoid(g) * u).astype(BF16)
    if prep:
        _prep_mixer_weights(*refs[n_in - 4:n_in], *refs[n_in + 1:n_in + 3])
    r = x + 0.5 * jnp.dot(act_ref[...], wd_ref[...], preferred_element_type=F32)
    if final_norm:
        r = _rms(r, refs[5][...])
    o_ref[...] = r


def _ffn(x2, norm_g, w_gate, w_up, w_down, final_g=None, mixer_weights=None):
    n, d = x2.shape
    assert n % FFN_ROWS == 0 and d == D_MODEL and w_gate.shape == (d, D_FF) and w_down.shape == (D_FF, d)
    steps = n // FFN_ROWS
    final_norm = final_g is not None
    prep = mixer_weights is not None
    f32_weights = w_gate.dtype == F32
    assert w_up.dtype == w_gate.dtype and w_down.dtype == w_gate.dtype
    row_spec = pl.BlockSpec((FFN_ROWS, d), lambda i: (i, 0))
    weight_specs = ([pl.BlockSpec(memory_space=pl.ANY)] * 3 if f32_weights else
                    [_const_spec((d, D_FF)), _const_spec((d, D_FF)), _const_spec((D_FF, d))])
    in_specs = [row_spec, _const_spec((1, d))] + weight_specs
    args = [x2, norm_g.reshape(1, d), w_gate, w_up, w_down]
    out_specs = [row_spec]
    out_shape = [jax.ShapeDtypeStruct((n, d), F32)]
    if final_norm:
        in_specs.append(_const_spec((1, d)))
        args.append(final_g.reshape(1, d))
    if prep:
        slab = lambda arr: pl.BlockSpec((arr.shape[0] // steps, arr.shape[1]), lambda i: (i, 0))
        for arr in mixer_weights:
            in_specs.append(slab(arr))
            args.append(arr)
        packed = [jax.ShapeDtypeStruct((d, MIX_W), BF16), jax.ShapeDtypeStruct((MLSTM_V_W, 2 * d), BF16)]
        out_shape += packed
        out_specs += [slab(p) for p in packed]
    scratch = []
    if f32_weights:
        assert d % STAGE_ROWS_IN == 0 and D_FF % STAGE_ROWS_OUT == 0
        scratch = [pltpu.VMEM((d, D_FF), BF16), pltpu.VMEM((d, D_FF), BF16), pltpu.VMEM((D_FF, d), BF16),
                   pltpu.VMEM((STAGE_SLOTS, STAGE_ROWS_IN, D_FF), F32),
                   pltpu.VMEM((STAGE_SLOTS, STAGE_ROWS_OUT, d), F32),
                   pltpu.SemaphoreType.DMA((STAGE_SLOTS,))]
    scratch.append(pltpu.VMEM((FFN_ROWS, D_FF), BF16))
    vmem_eighths = 7 if f32_weights else 6
    res = pl.pallas_call(
        functools.partial(_ffn_body, final_norm=final_norm, prep=prep, f32_weights=f32_weights),
        grid=(steps,),
        in_specs=in_specs,
        out_specs=out_specs,
        out_shape=out_shape,
        scratch_shapes=scratch,
        compiler_params=pltpu.CompilerParams(
            dimension_semantics=("arbitrary",), vmem_limit_bytes=(vmem_eighths * VMEM_BYTES_V7X) // 8),
        name="ffn_final" if final_norm else "ffn",
    )(*args)
    return (res[0], res[1:]) if prep else res[0]


Slot = collections.namedtuple("Slot", "xs h q kT vext so qaT kc vcT g")


def _slot_shapes(rows, d):
    return Slot(
        xs=pltpu.VMEM((rows, d), F32),
        h=pltpu.VMEM((rows, d), BF16),
        q=pltpu.VMEM((rows, MLSTM_QK_W + PITCH_PAD), BF16),
        kT=pltpu.VMEM((MLSTM_QK_W, rows + PITCH_PAD), F32),
        vext=pltpu.VMEM((rows, MLSTM_HEADS * VEXT_W + PITCH_PAD), BF16),
        so=pltpu.VMEM((rows, MLSTM_V_W + PITCH_PAD), F32),
        qaT=pltpu.VMEM((ATTN_Q_W, rows + PITCH_PAD), BF16),
        kc=pltpu.VMEM((rows, LANES), BF16),
        vcT=pltpu.VMEM((LANES, rows + PITCH_PAD), BF16),
        g=pltpu.VMEM((GATE_ROWS, rows), F32),
    )


def _softcap(a):
    return GATE_SOFTCAP * jnp.tanh(a / GATE_SOFTCAP)


def _log_sigmoid(a):
    return jnp.minimum(a, 0.0) - jnp.log1p(jnp.exp(-jnp.abs(a)))


def _lane_cumsum(a):
    lane = lax.broadcasted_iota(jnp.int32, a.shape, 1)
    d = 1
    while d < LANES:
        a = a + jnp.where(lane >= d, pltpu.roll(a, d, 1), 0.0)
        d *= 2
    return a


def _rope_tables(pos_row, inv_col):
    ang = inv_col * pos_row
    c, s = jnp.cos(ang), jnp.sin(ang)
    one, zero = jnp.ones_like(c), jnp.zeros_like(c)
    per_head = ATTN_HEAD_DIM // SUBLANES

    def tile(first, second, rest):
        rows = []
        for _ in range(LANES // ATTN_HEAD_DIM):
            rows += [first, second] + [rest] * (per_head - 2)
        return jnp.concatenate(rows, axis=0).T
    return tile(c, c, one), tile(zero, s, zero), tile(-s, zero, zero)


def _rope(a, tables):
    cos_t, sin_up, sin_dn = tables
    half = ROPE_DIM // 2
    return a * cos_t + pltpu.roll(a, half, 1) * sin_up + pltpu.roll(a, LANES - half, 1) * sin_dn


def _rep(a, shape):
    return jnp.broadcast_to(a, shape)


def _mixer_step(t_cur, ins, cur, nxt, shared, outs):
    x_ref, pos_ref, ng_ref, inv_ref, bif_ref, ong_ref, sink_ref, wmix_ref, wbr_ref = ins[:N_MIXER_IN]
    wqk_ref, wv_ref, wo_ref, wqa_ref, wkva_ref, wgm_ref, wga_ref, wout_ref = (
        _mix_piece(wmix_ref, name) for name, _ in MIX_PIECES)
    wbrm_ref, wbra_ref = wbr_ref.at[:, 0:D_MODEL], wbr_ref.at[:, D_MODEL:2 * D_MODEL]
    cast_in = ins[N_MIXER_IN:]
    o_ref, cast_out = outs[0], outs[1:]
    pk_scr, pvT_scr, sgm_scr, sga_scr, hm_scr, oa_scr, mg_scr, c_scr, m_scr = shared
    rows = MIX_ROWS
    n_chunks = rows // CHUNK
    n_blocks = rows // WINDOW
    group = ATTN_Q_HEADS // ATTN_KV_HEADS

    shared_vals = {}

    def p_norm():
        x_new = x_ref[0]
        nxt.xs[...] = x_new
        nxt.h[...] = _rms(x_new, ng_ref[...]).astype(BF16)

    def side_cast():
        for src, dst in zip(cast_in, cast_out):
            dst[...] = src[...].astype(BF16)

    def p_qk():
        qkg = jnp.dot(nxt.h[...], wqk_ref[...], preferred_element_type=F32)
        nxt.q[:, 0:MLSTM_QK_W] = qkg[:, 0:MLSTM_QK_W].astype(BF16)
        nxt.kT[:, 0:rows] = qkg[:, MLSTM_QK_W:2 * MLSTM_QK_W].T
        gates_t = qkg[:, 2 * MLSTM_QK_W:QKG_W].T
        nxt.g[...] = _softcap(gates_t[0:GATE_ROWS] + bif_ref[...])

    def p_v():
        v = jnp.dot(nxt.h[...], wv_ref[...], preferred_element_type=F32)
        for hd in range(MLSTM_HEADS):
            nxt.vext[:, hd * VEXT_W:hd * VEXT_W + MLSTM_V_DIM] = (
                v[:, hd * MLSTM_V_DIM:(hd + 1) * MLSTM_V_DIM].astype(BF16))
            nxt.vext[:, hd * VEXT_W + MLSTM_V_DIM:(hd + 1) * VEXT_W] = jnp.ones((rows, MLSTM_V_DIM), BF16)

    def p_so():
        nxt.so[:, 0:MLSTM_V_W] = jax.nn.sigmoid(jnp.dot(nxt.h[...], wo_ref[...], preferred_element_type=F32))

    def p_qa():
        shared_vals["tables"] = _rope_tables(pos_ref[0].astype(F32), inv_ref[...])
        qa = jnp.dot(nxt.h[...], wqa_ref[...], preferred_element_type=F32)
        for r0 in range(0, rows, LANES):
            tables_blk = tuple(tab[r0:r0 + LANES] for tab in shared_vals["tables"])
            for p in range(ATTN_Q_W // LANES):
                nxt.qaT[p * LANES:(p + 1) * LANES, r0:r0 + LANES] = _rope(
                    qa[r0:r0 + LANES, p * LANES:(p + 1) * LANES], tables_blk).T.astype(BF16)

    def p_kva():
        kva = jnp.dot(nxt.h[...], wkva_ref[...], preferred_element_type=F32)
        ka = _rope(kva[:, 0:ATTN_KV_W], shared_vals["tables"])
        va = kva[:, ATTN_KV_W:2 * ATTN_KV_W]
        nxt.kc[...] = ka.astype(BF16)
        nxt.vcT[:, 0:rows] = va.T.astype(BF16)

    def gate_piece(idx):
        per_gate = D_MODEL // MXU_COLS
        gref, wref = (sgm_scr, wgm_ref) if idx < per_gate else (sga_scr, wga_ref)
        c0 = (idx % per_gate) * MXU_COLS
        gref[:, c0:c0 + MXU_COLS] = jax.nn.sigmoid(
            jnp.dot(cur.h[...], wref[:, c0:c0 + MXU_COLS], preferred_element_type=F32))

    qi = lax.broadcasted_iota(jnp.int32, (CHUNK, CHUNK), 0)
    si = lax.broadcasted_iota(jnp.int32, (CHUNK, CHUNK), 1)
    tri = si <= qi
    zeros_c = jnp.zeros((MLSTM_QK_DIM, VEXT_W), BF16)
    ong = ong_ref[...]
    tile_shape = (CHUNK, CHUNK)
    rep_shape = (MLSTM_HEADS, LANES)

    chunk_stats = []
    m_vec = m_scr[0:MLSTM_HEADS, :]
    for c in range(n_chunks):
        g_c = cur.g[:, c * CHUNK:(c + 1) * CHUNK]
        i_r = g_c[0:MLSTM_HEADS]
        logf_r = _log_sigmoid(g_c[MLSTM_HEADS:GATE_ROWS])
        b_r = _lane_cumsum(logf_r)
        b_last = _rep(b_r[:, CHUNK - 1:CHUNK], rep_shape)
        a_r = b_last - b_r + i_r
        a_max = _rep(jnp.max(a_r, axis=1, keepdims=True), rep_shape)
        m_new = jnp.maximum(b_last + m_vec, a_max)
        chunk_stats.append(dict(
            logf_r=logf_r, ib_r=i_r - b_r, m_prev=m_vec,
            decay=jnp.exp(b_last + m_vec - m_new), scale=jnp.exp(a_max - m_new),
            w_loc=jnp.exp(a_r - a_max)))
        m_vec = m_new
    c_state = [c_scr[hd] for hd in range(MLSTM_HEADS)]
    c_next = [None] * MLSTM_HEADS
    s_qk = {}

    def mlstm_scores(c):
        rs = slice(c * CHUNK, (c + 1) * CHUNK)
        for pair in range(MLSTM_HEADS // 2):
            q2 = cur.q[rs, pair * PAIR_W:(pair + 1) * PAIR_W]
            k_even, k_odd = (cur.kT[hd * MLSTM_QK_DIM:(hd + 1) * MLSTM_QK_DIM, rs].astype(BF16)
                             for hd in (2 * pair, 2 * pair + 1))
            zk = jnp.zeros_like(k_even)
            k_both = jnp.concatenate([jnp.concatenate([k_even, zk], axis=1),
                                      jnp.concatenate([zk, k_odd], axis=1)], axis=0)
            s2 = jnp.dot(q2, k_both, preferred_element_type=F32)
            s_qk[c, 2 * pair] = s2[:, 0:CHUNK]
            s_qk[c, 2 * pair + 1] = s2[:, CHUNK:2 * CHUNK]

    def mlstm_state(c):
        rs = slice(c * CHUNK, (c + 1) * CHUNK)
        st = chunk_stats[c]
        for hd in range(MLSTM_HEADS):
            vext = cur.vext[rs, hd * VEXT_W:(hd + 1) * VEXT_W]
            ktw = (cur.kT[hd * MLSTM_QK_DIM:(hd + 1) * MLSTM_QK_DIM, rs] * st["w_loc"][hd:hd + 1]).astype(BF16)
            u = jnp.dot(ktw, vext, preferred_element_type=F32)
            decay = jnp.concatenate([st["decay"][hd:hd + 1]] * 2, axis=1)
            scale = jnp.concatenate([st["scale"][hd:hd + 1]] * 2, axis=1)
            c_next[hd] = decay * c_state[hd] + scale * u

    def mlstm_out(c):
        rs = slice(c * CHUNK, (c + 1) * CHUNK)
        st = chunk_stats[c]
        for hd in range(MLSTM_HEADS):
            pair, odd = divmod(hd, 2)
            q2 = cur.q[rs, pair * PAIR_W:(pair + 1) * PAIR_W]
            vext = cur.vext[rs, hd * VEXT_W:(hd + 1) * VEXT_W]
            b_col = _rep(jnp.sum(jnp.where(tri, st["logf_r"][hd:hd + 1], 0.0), axis=1, keepdims=True),
                         tile_shape)
            log_d = jnp.where(tri, b_col + st["ib_r"][hd:hd + 1], -jnp.inf)
            inter_log = b_col + st["m_prev"][hd:hd + 1]
            m_j = jnp.maximum(inter_log, _rep(jnp.max(log_d, axis=1, keepdims=True), tile_shape))
            sqk = (s_qk.pop((c, hd)) * jnp.exp(log_d - m_j)).astype(BF16)
            c_bf = c_state[hd].astype(BF16)
            c_pad = jnp.concatenate([zeros_c, c_bf] if odd else [c_bf, zeros_c], axis=0)
            q_inter = (q2 * jnp.exp(inter_log - m_j)).astype(BF16)
            tot = jnp.dot(jnp.concatenate([sqk, q_inter], axis=1), jnp.concatenate([vext, c_pad], axis=0),
                          preferred_element_type=F32)
            num = tot[:, 0:MLSTM_V_DIM]
            den = tot[:, MLSTM_V_DIM:VEXT_W]
            hh = num / jnp.maximum(jnp.abs(den), jnp.exp(-m_j))
            vs = slice(hd * MLSTM_V_DIM, (hd + 1) * MLSTM_V_DIM)
            hn = hh * lax.rsqrt(jnp.mean(hh * hh, axis=1, keepdims=True) + NORM_EPS) * ong[:, vs]
            hm_scr[rs, vs] = (hn * cur.so[rs, vs]).astype(BF16)
            c_state[hd] = c_next[hd]

    si2 = lax.broadcasted_iota(jnp.int32, (2 * WINDOW, WINDOW), 0)
    qi2 = lax.broadcasted_iota(jnp.int32, (2 * WINDOW, WINDOW), 1)
    band = (si2 > qi2) & (si2 <= qi2 + WINDOW)
    band_first = band & ((si2 >= WINDOW) | (t_cur > 0))
    zero_q = jnp.zeros((ATTN_HEAD_DIM, group * WINDOW), BF16)
    ones_rows = jnp.ones((SUM_ROWS, 2 * WINDOW), BF16)
    attn = {}

    def attn_scores(nb):
        qs = slice(nb * WINDOW, (nb + 1) * WINDOW)
        mask = band_first if nb == 0 else band
        if nb == 0:
            k_win = jnp.concatenate([pk_scr[...], cur.kc[0:WINDOW, :]], axis=0)
        else:
            k_win = cur.kc[(nb - 1) * WINDOW:(nb + 1) * WINDOW, :]
        for kh in range(ATTN_KV_HEADS):
            q_t = jnp.concatenate([cur.qaT[(kh * group + j) * ATTN_HEAD_DIM:(kh * group + j + 1) * ATTN_HEAD_DIM, qs]
                                   for j in range(group)], axis=1)
            rhs = jnp.concatenate([q_t, zero_q] if kh == 0 else [zero_q, q_t], axis=0)
            s_t = jnp.dot(k_win, rhs, preferred_element_type=F32)
            p_cols, sk_cols = [], []
            for j in range(group):
                sink = sink_ref[kh * group + j]
                sc = jnp.where(mask, s_t[:, j * WINDOW:(j + 1) * WINDOW], -jnp.inf)
                mx = jnp.maximum(jnp.max(sc, axis=0, keepdims=True), sink)
                p_cols.append(jnp.exp(sc - mx).astype(BF16))
                sk_cols.append(jnp.exp(sink - mx))
            attn[nb, kh] = (jnp.concatenate(p_cols, axis=1), jnp.concatenate(sk_cols, axis=1))

    def attn_values(nb):
        qs = slice(nb * WINDOW, (nb + 1) * WINDOW)
        if nb == 0:
            v_win = jnp.concatenate([pvT_scr[...], cur.vcT[:, 0:WINDOW]], axis=1)
        else:
            v_win = cur.vcT[:, (nb - 1) * WINDOW:(nb + 1) * WINDOW]
        for kh in range(ATTN_KV_HEADS):
            probs_t, sink_term = attn.pop((nb, kh))
            lhs = jnp.concatenate([v_win[kh * ATTN_HEAD_DIM:(kh + 1) * ATTN_HEAD_DIM], ones_rows], axis=0)
            o_t = jnp.dot(lhs, probs_t, preferred_element_type=F32)
            on_t = o_t[0:ATTN_HEAD_DIM] / (o_t[ATTN_HEAD_DIM:ATTN_HEAD_DIM + 1] + sink_term)
            for j2 in range(group // 2):
                pair_t = jnp.concatenate([on_t[:, (2 * j2) * WINDOW:(2 * j2 + 1) * WINDOW],
                                          on_t[:, (2 * j2 + 1) * WINDOW:(2 * j2 + 2) * WINDOW]], axis=0)
                pair = kh * (group // 2) + j2
                oa_scr[qs, pair * LANES:(pair + 1) * LANES] = pair_t.T.astype(BF16)

    def branch_attn(idx):
        c0 = idx * MIX_COL_CHUNK
        y_a = jnp.dot(oa_scr[:, 0:ATTN_Q_W], wbra_ref[:, c0:c0 + MIX_COL_CHUNK], preferred_element_type=F32)
        sga_scr[:, c0:c0 + MIX_COL_CHUNK] = sga_scr[:, c0:c0 + MIX_COL_CHUNK] * y_a

    gate_piece(0)
    p_norm()
    side_cast()
    attn_scores(0); gate_piece(1); mlstm_scores(0); mlstm_state(0)
    p_qk()
    attn_values(0); gate_piece(2); mlstm_out(0)
    attn_scores(1); gate_piece(3); mlstm_scores(1); mlstm_state(1)
    p_v()
    attn_values(1); gate_piece(4); mlstm_out(1)
    attn_scores(2); p_so(); mlstm_scores(2); mlstm_state(2)
    attn_values(2); gate_piece(5); mlstm_out(2)
    attn_scores(3); p_qa(); mlstm_scores(3); mlstm_state(3)
    attn_values(3); gate_piece(6)
    p_kva()
    mlstm_out(3); gate_piece(7)
    branch_attn(0); branch_attn(1)

    for hd in range(MLSTM_HEADS):
        c_scr[hd] = c_state[hd]
    m_scr[0:MLSTM_HEADS, :] = m_vec
    pk_scr[...] = cur.kc[rows - WINDOW:rows, :]
    pvT_scr[...] = cur.vcT[:, rows - WINDOW:rows]

    for c0, c1 in _col_chunks(D_MODEL, MIX_COL_CHUNK):
        y_m = jnp.dot(hm_scr[:, 0:MLSTM_V_W], wbrm_ref[:, c0:c1], preferred_element_type=F32)
        mg_scr[:, c0:c1] = (sgm_scr[:, c0:c1] * y_m + sga_scr[:, c0:c1]).astype(BF16)
    o_ref[0] = cur.xs[...] + jnp.dot(mg_scr[:, 0:D_MODEL], wout_ref[...], preferred_element_type=F32)


def _mixer_body(*refs, tiles_per_seq, n_cast):
    n_in = N_MIXER_IN + n_cast
    n_out = 1 + n_cast
    n_slot = len(Slot._fields)
    ins = refs[:n_in]
    outs = refs[n_in:n_in + n_out]
    scr = refs[n_in + n_out:]
    slot_a, slot_b = Slot(*scr[:n_slot]), Slot(*scr[n_slot:2 * n_slot])
    shared = scr[2 * n_slot:]
    prev_blocks = shared[0:2]
    c_scr, m_scr = shared[-2:]
    s = pl.program_id(0)
    t_cur = lax.rem(s + tiles_per_seq - 1, tiles_per_seq)

    @pl.when(s == 0)
    def _():
        for ref in list(slot_b) + list(shared):
            ref[...] = jnp.zeros_like(ref)

    @pl.when(t_cur == 0)
    def _():
        c_scr[...] = jnp.zeros_like(c_scr)
        m_scr[...] = jnp.zeros_like(m_scr)
        for ref in prev_blocks:
            ref[...] = jnp.zeros_like(ref)

    even = lax.rem(s, 2) == 0

    @pl.when(even)
    def _():
        _mixer_step(t_cur, ins, slot_b, slot_a, shared, outs)

    @pl.when(jnp.logical_not(even))
    def _():
        _mixer_step(t_cur, ins, slot_a, slot_b, shared, outs)


def _cast_slabs(arr, steps):
    rows = arr.shape[0]
    slab = 16
    while rows % slab or rows // slab > steps:
        slab += 16
    n_slabs = rows // slab
    return pl.BlockSpec((slab, arr.shape[1]), lambda i: (jnp.minimum(i, n_slabs - 1), 0))


def _mixer(x, positions, norm_g, b_i, b_f, out_norm_g, sinks, weights, cast_weights):
    b, s, d = x.shape
    assert s % MIX_ROWS == 0 and d == D_MODEL and positions.shape == (b, s)
    w_mix, w_br = weights
    b_if = jnp.concatenate([b_i, b_f]).reshape(GATE_ROWS, 1)
    half = ROPE_DIM // 2
    inv = (ROPE_THETA ** (-jnp.arange(half, dtype=F32) * 2.0 / ROPE_DIM)).reshape(half, 1)

    rows = MIX_ROWS
    tiles_per_seq = s // rows
    n_tiles = b * tiles_per_seq
    def in_tile(i):
        j = jnp.minimum(i, n_tiles - 1)
        return j // tiles_per_seq, j % tiles_per_seq

    def out_tile(i):
        j = jnp.maximum(i - 1, 0)
        return j // tiles_per_seq, j % tiles_per_seq

    in_specs = [
        pl.BlockSpec((1, rows, d), lambda i: (*in_tile(i), 0)),
        pl.BlockSpec((1, 1, rows), lambda i: (in_tile(i)[0], 0, in_tile(i)[1])),
        _const_spec((1, d)), _const_spec((half, 1)), _const_spec((GATE_ROWS, 1)),
        _const_spec((1, MLSTM_V_W)),
        pl.BlockSpec(memory_space=pltpu.SMEM),
        _const_spec((d, MIX_W)), _const_spec((MLSTM_V_W, 2 * d)),
    ]
    slot = list(_slot_shapes(rows, d))
    shared = [
        pltpu.VMEM((WINDOW, LANES), BF16),
        pltpu.VMEM((LANES, WINDOW), BF16),
        pltpu.VMEM((rows, d + PITCH_PAD), F32),
        pltpu.VMEM((rows, d + PITCH_PAD), F32),
        pltpu.VMEM((rows, MLSTM_V_W + PITCH_PAD), BF16),
        pltpu.VMEM((rows, ATTN_Q_W + PITCH_PAD), BF16),
        pltpu.VMEM((rows, d + PITCH_PAD), BF16),
        pltpu.VMEM((MLSTM_HEADS, MLSTM_QK_DIM, VEXT_W), F32),
        pltpu.VMEM((SUBLANES, LANES), F32),
    ]
    assert len(in_specs) == N_MIXER_IN
    steps = n_tiles + 1
    cast_specs = [_cast_slabs(arr, steps) for arr in cast_weights]
    res = pl.pallas_call(
        functools.partial(_mixer_body, tiles_per_seq=tiles_per_seq, n_cast=len(cast_weights)),
        grid=(steps,),
        in_specs=in_specs + cast_specs,
        out_specs=[pl.BlockSpec((1, rows, d), lambda i: (*out_tile(i), 0))] + cast_specs,
        out_shape=[jax.ShapeDtypeStruct((b, s, d), F32)]
        + [jax.ShapeDtypeStruct(arr.shape, BF16) for arr in cast_weights],
        scratch_shapes=slot + slot + shared,
        compiler_params=pltpu.CompilerParams(
            dimension_semantics=("arbitrary",), vmem_limit_bytes=(7 * VMEM_BYTES_V7X) // 8),
        name="mixer",
    )(x, positions.reshape(b, 1, s), norm_g.reshape(1, d), inv, b_if, out_norm_g.reshape(1, MLSTM_V_W),
      sinks, w_mix, w_br, *cast_weights)
    return res[0], res[1:]


def kernel(x, positions, ffn1_norm_g, ffn1_w_gate, ffn1_w_up, ffn1_w_down, mix_norm_g, w_in, mlstm_b_i, mlstm_b_f, mlstm_out_norm_g, attn_sinks, w_branch_mlstm, w_branch_attn, w_out, ffn2_norm_g, ffn2_w_gate, ffn2_w_up, ffn2_w_down, final_norm_g):
    b, s, d = x.shape
    depth = w_in.shape[0]
    for l in range(depth):
        last = l == depth - 1
        x2, mixer_w = _ffn(x.reshape(b * s, d), ffn1_norm_g[l], ffn1_w_gate[l], ffn1_w_up[l], ffn1_w_down[l],
                           mixer_weights=(w_in[l], w_branch_mlstm[l], w_branch_attn[l], w_out[l]))
        x3, ffn2_w = _mixer(x2.reshape(b, s, d), positions, mix_norm_g[l], mlstm_b_i[l], mlstm_b_f[l],
                            mlstm_out_norm_g[l], attn_sinks[l], mixer_w,
                            (ffn2_w_gate[l], ffn2_w_up[l], ffn2_w_down[l]))
        x = _ffn(x3.reshape(b * s, d), ffn2_norm_g[l], *ffn2_w,
                 final_g=final_norm_g if last else None).reshape(b, s, d)
    return x
```

```python
import collections
import functools

import jax
import jax.numpy as jnp
from jax import lax
from jax.experimental import pallas as pl
from jax.experimental.pallas import tpu as pltpu

D_MODEL = 1024
D_FF = 2816
MLSTM_HEADS = 4
MLSTM_QK_DIM = 64
MLSTM_V_DIM = 128
GATE_SOFTCAP = 15.0
ATTN_Q_HEADS = 8
ATTN_KV_HEADS = 2
ATTN_HEAD_DIM = 64
WINDOW = 128
ROPE_DIM = ATTN_HEAD_DIM // 4
ROPE_THETA = 500000.0
NORM_EPS = 1e-6

MLSTM_QK_W = MLSTM_HEADS * MLSTM_QK_DIM
MLSTM_V_W = MLSTM_HEADS * MLSTM_V_DIM
ATTN_Q_W = ATTN_Q_HEADS * ATTN_HEAD_DIM
ATTN_KV_W = ATTN_KV_HEADS * ATTN_HEAD_DIM
IN_WIDTHS = (MLSTM_QK_W, MLSTM_QK_W, MLSTM_V_W, MLSTM_V_W, MLSTM_HEADS, MLSTM_HEADS,
             ATTN_Q_W, ATTN_KV_W, ATTN_KV_W, D_MODEL, D_MODEL)

LANES = 128
SUBLANES = 8
MXU_COLS = 256
VMEM_BYTES_V7X = 64 * 1024 * 1024

FFN_ROWS = 1024
FFN_COL_CHUNK = 2 * MXU_COLS
STAGE_SLOTS = 4
STAGE_ROWS_IN = 64
STAGE_ROWS_OUT = 176
MIX_ROWS = 512
MIX_COL_CHUNK = 2 * MXU_COLS
CHUNK = LANES
PAIR_W = 2 * MLSTM_QK_DIM
VEXT_W = 2 * MLSTM_V_DIM
SUM_ROWS = 16
GATE_ROWS = 2 * MLSTM_HEADS
QKG_W = 2 * MLSTM_QK_W + LANES
N_MIXER_IN = 9
PITCH_PAD = LANES
MIX_PIECES = (("qkg", QKG_W), ("v", MLSTM_V_W), ("o", MLSTM_V_W), ("qa", ATTN_Q_W), ("kva", 2 * ATTN_KV_W),
              ("gm", D_MODEL), ("ga", D_MODEL), ("out", D_MODEL))
MIX_W = sum(w for _, w in MIX_PIECES)

F32 = jnp.float32
BF16 = jnp.bfloat16


def _rms(x, g):
    return x * lax.rsqrt(jnp.mean(x * x, axis=-1, keepdims=True) + NORM_EPS) * g


def _col_chunks(total, width):
    return [(c, min(c + width, total)) for c in range(0, total, width)]


def _const_spec(shape):
    return pl.BlockSpec(shape, lambda *_: (0,) * len(shape), pipeline_mode=pl.Buffered(1))


def _mix_piece(ref, name):
    start = 0
    for piece, width in MIX_PIECES:
        if piece == name:
            return ref.at[:, start:start + width]
        start += width
    raise KeyError(name)


def _prep_mixer_weights(win_ref, wbrm_ref, wbra_ref, wout_ref, wmix_o, wbr_o):
    starts = [0]
    for w in IN_WIDTHS:
        starts.append(starts[-1] + w)
    col = lambda i: win_ref[:, starts[i]:starts[i + 1]]
    gate_pad = jnp.zeros((win_ref.shape[0], LANES - GATE_ROWS), F32)
    pieces = dict(
        qkg=jnp.concatenate([col(0) * (MLSTM_QK_DIM ** -0.5), col(1), col(4), col(5), gate_pad], axis=1),
        v=col(2), o=col(3), qa=col(6) * (ATTN_HEAD_DIM ** -0.5),
        kva=jnp.concatenate([col(7), col(8)], axis=1), gm=col(9), ga=col(10), out=wout_ref[...])
    for name, _ in MIX_PIECES:
        _mix_piece(wmix_o, name)[...] = pieces[name].astype(BF16)
    wbr_o[:, 0:D_MODEL] = wbrm_ref[...].astype(BF16)
    wbr_o[:, D_MODEL:2 * D_MODEL] = wbra_ref[...].astype(BF16)


def _stream_cast(src_hbm, dst_ref, stage_ref, sem_ref, chunk_rows):
    n_chunks = src_hbm.shape[0] // chunk_rows
    ahead = STAGE_SLOTS - 1

    def copy(k):
        slot = k % STAGE_SLOTS
        return pltpu.make_async_copy(src_hbm.at[k * chunk_rows:(k + 1) * chunk_rows], stage_ref.at[slot],
                                     sem_ref.at[slot])
    for k in range(min(ahead, n_chunks)):
        copy(k).start()
    for k in range(n_chunks):
        if k + ahead < n_chunks:
            copy(k + ahead).start()
        copy(k).wait()
        dst_ref[k * chunk_rows:(k + 1) * chunk_rows, :] = stage_ref[k % STAGE_SLOTS].astype(BF16)


def _ffn_body(*refs, final_norm, prep, f32_weights):
    n_in = 5 + int(final_norm) + (4 if prep else 0)
    x_ref, g_ref, wg_ref, wu_ref, wd_ref = refs[:5]
    o_ref = refs[n_in]
    act_ref = refs[-1]
    if f32_weights:
        wg_scr, wu_scr, wd_scr, stage_in, stage_out, sems = refs[-7:-1]

        @pl.when(pl.program_id(0) == 0)
        def _():
            _stream_cast(wg_ref, wg_scr, stage_in, sems, STAGE_ROWS_IN)
            _stream_cast(wu_ref, wu_scr, stage_in, sems, STAGE_ROWS_IN)
            _stream_cast(wd_ref, wd_scr, stage_out, sems, STAGE_ROWS_OUT)
        wg_ref, wu_ref, wd_ref = wg_scr, wu_scr, wd_scr
    x = x_ref[...]
    h = _rms(x, g_ref[...]).astype(BF16)
    for c0, c1 in _col_chunks(D_FF, FFN_COL_CHUNK):
        g = jnp.dot(h, wg_ref[:, c0:c1], preferred_element_type=F32)
        u = jnp.dot(h, wu_ref[:, c0:c1], preferred_element_type=F32)
        act_ref[:, c0:c1] = (g * jax.nn.sigmoid(g) * u).astype(BF16)
    if prep:
        _prep_mixer_weights(*refs[n_in - 4:n_in], *refs[n_in + 1:n_in + 3])
    r = x + 0.5 * jnp.dot(act_ref[...], wd_ref[...], preferred_element_type=F32)
    if final_norm:
        r = _rms(r, refs[5][...])
    o_ref[...] = r


def _ffn(x2, norm_g, w_gate, w_up, w_down, final_g=None, mixer_weights=None):
    n, d = x2.shape
    assert n % FFN_ROWS == 0 and d == D_MODEL and w_gate.shape == (d, D_FF) and w_down.shape == (D_FF, d)
    steps = n // FFN_ROWS
    final_norm = final_g is not None
    prep = mixer_weights is not None
    f32_weights = w_gate.dtype == F32
    assert w_up.dtype == w_gate.dtype and w_down.dtype == w_gate.dtype
    row_spec = pl.BlockSpec((FFN_ROWS, d), lambda i: (i, 0))
    weight_specs = ([pl.BlockSpec(memory_space=pl.ANY)] * 3 if f32_weights else
                    [_const_spec((d, D_FF)), _const_spec((d, D_FF)), _const_spec((D_FF, d))])
    in_specs = [row_spec, _const_spec((1, d))] + weight_specs
    args = [x2, norm_g.reshape(1, d), w_gate, w_up, w_down]
    out_specs = [row_spec]
    out_shape = [jax.ShapeDtypeStruct((n, d), F32)]
    if final_norm:
        in_specs.append(_const_spec((1, d)))
        args.append(final_g.reshape(1, d))
    if prep:
        slab = lambda arr: pl.BlockSpec((arr.shape[0] // steps, arr.shape[1]), lambda i: (i, 0))
        for arr in mixer_weights:
            in_specs.append(slab(arr))
            args.append(arr)
        packed = [jax.ShapeDtypeStruct((d, MIX_W), BF16), jax.ShapeDtypeStruct((MLSTM_V_W, 2 * d), BF16)]
        out_shape += packed
        out_specs += [slab(p) for p in packed]
    scratch = []
    if f32_weights:
        assert d % STAGE_ROWS_IN == 0 and D_FF % STAGE_ROWS_OUT == 0
        scratch = [pltpu.VMEM((d, D_FF), BF16), pltpu.VMEM((d, D_FF), BF16), pltpu.VMEM((D_FF, d), BF16),
                   pltpu.VMEM((STAGE_SLOTS, STAGE_ROWS_IN, D_FF), F32),
                   pltpu.VMEM((STAGE_SLOTS, STAGE_ROWS_OUT, d), F32),
                   pltpu.SemaphoreType.DMA((STAGE_SLOTS,))]
    scratch.append(pltpu.VMEM((FFN_ROWS, D_FF), BF16))
    vmem_eighths = 7 if f32_weights else 6
    res = pl.pallas_call(
        functools.partial(_ffn_body, final_norm=final_norm, prep=prep, f32_weights=f32_weights),
        grid=(steps,),
        in_specs=in_specs,
        out_specs=out_specs,
        out_shape=out_shape,
        scratch_shapes=scratch,
        compiler_params=pltpu.CompilerParams(
            dimension_semantics=("arbitrary",), vmem_limit_bytes=(vmem_eighths * VMEM_BYTES_V7X) // 8),
        name="ffn_final" if final_norm else "ffn",
    )(*args)
    return (res[0], res[1:]) if prep else res[0]


Slot = collections.namedtuple("Slot", "xs h q kT vext so qaT kc vcT g")


def _slot_shapes(rows, d):
    return Slot(
        xs=pltpu.VMEM((rows, d + PITCH_PAD), F32),
        h=pltpu.VMEM((rows, d), BF16),
        q=pltpu.VMEM((rows, MLSTM_QK_W + PITCH_PAD), BF16),
        kT=pltpu.VMEM((MLSTM_QK_W, rows + PITCH_PAD), F32),
        vext=pltpu.VMEM((rows, MLSTM_HEADS * VEXT_W + PITCH_PAD), BF16),
        so=pltpu.VMEM((rows, MLSTM_V_W + PITCH_PAD), F32),
        qaT=pltpu.VMEM((ATTN_Q_W, rows + PITCH_PAD), BF16),
        kc=pltpu.VMEM((rows, LANES), BF16),
        vcT=pltpu.VMEM((LANES, rows + PITCH_PAD), BF16),
        g=pltpu.VMEM((GATE_ROWS, rows), F32),
    )


def _softcap(a):
    return GATE_SOFTCAP * jnp.tanh(a / GATE_SOFTCAP)


def _log_sigmoid(a):
    return jnp.minimum(a, 0.0) - jnp.log1p(jnp.exp(-jnp.abs(a)))


def _lane_cumsum(a):
    lane = lax.broadcasted_iota(jnp.int32, a.shape, 1)
    d = 1
    while d < LANES:
        a = a + jnp.where(lane >= d, pltpu.roll(a, d, 1), 0.0)
        d *= 2
    return a


def _rope_tables(pos_row, inv_col):
    ang = inv_col * pos_row
    c, s = jnp.cos(ang), jnp.sin(ang)
    one, zero = jnp.ones_like(c), jnp.zeros_like(c)
    per_head = ATTN_HEAD_DIM // SUBLANES

    def tile(first, second, rest):
        rows = []
        for _ in range(LANES // ATTN_HEAD_DIM):
            rows += [first, second] + [rest] * (per_head - 2)
        return jnp.concatenate(rows, axis=0).T
    return tile(c, c, one), tile(zero, s, zero), tile(-s, zero, zero)


def _rope(a, tables):
    cos_t, sin_up, sin_dn = tables
    half = ROPE_DIM // 2
    return a * cos_t + pltpu.roll(a, half, 1) * sin_up + pltpu.roll(a, LANES - half, 1) * sin_dn


def _rep(a, shape):
    return jnp.broadcast_to(a, shape)


def _mixer_step(t_cur, ins, cur, nxt, shared, outs):
    x_ref, pos_ref, ng_ref, inv_ref, bif_ref, ong_ref, sink_ref, wmix_ref, wbr_ref = ins[:N_MIXER_IN]
    wqk_ref, wv_ref, wo_ref, wqa_ref, wkva_ref, wgm_ref, wga_ref, wout_ref = (
        _mix_piece(wmix_ref, name) for name, _ in MIX_PIECES)
    wbrm_ref, wbra_ref = wbr_ref.at[:, 0:D_MODEL], wbr_ref.at[:, D_MODEL:2 * D_MODEL]
    cast_in = ins[N_MIXER_IN:]
    o_ref, cast_out = outs[0], outs[1:]
    pk_scr, pvT_scr, sgm_scr, sga_scr, hm_scr, oa_scr, mg_scr, c_scr, m_scr = shared
    rows = MIX_ROWS
    n_chunks = rows // CHUNK
    n_blocks = rows // WINDOW
    group = ATTN_Q_HEADS // ATTN_KV_HEADS

    shared_vals = {}

    def p_norm():
        x_new = x_ref[0]
        nxt.xs[:, 0:D_MODEL] = x_new
        nxt.h[...] = _rms(x_new, ng_ref[...]).astype(BF16)

    def side_cast():
        for src, dst in zip(cast_in, cast_out):
            dst[...] = src[...].astype(BF16)

    def p_qk():
        qkg = jnp.dot(nxt.h[...], wqk_ref[...], preferred_element_type=F32)
        nxt.q[:, 0:MLSTM_QK_W] = qkg[:, 0:MLSTM_QK_W].astype(BF16)
        nxt.kT[:, 0:rows] = qkg[:, MLSTM_QK_W:2 * MLSTM_QK_W].T
        gates_t = qkg[:, 2 * MLSTM_QK_W:QKG_W].T
        nxt.g[...] = _softcap(gates_t[0:GATE_ROWS] + bif_ref[...])

    def p_v():
        v = jnp.dot(nxt.h[...], wv_ref[...], preferred_element_type=F32)
        for hd in range(MLSTM_HEADS):
            nxt.vext[:, hd * VEXT_W:hd * VEXT_W + MLSTM_V_DIM] = (
                v[:, hd * MLSTM_V_DIM:(hd + 1) * MLSTM_V_DIM].astype(BF16))
            nxt.vext[:, hd * VEXT_W + MLSTM_V_DIM:(hd + 1) * VEXT_W] = jnp.ones((rows, MLSTM_V_DIM), BF16)

    def p_so():
        nxt.so[:, 0:MLSTM_V_W] = jax.nn.sigmoid(jnp.dot(nxt.h[...], wo_ref[...], preferred_element_type=F32))

    def p_qa():
        shared_vals["tables"] = _rope_tables(pos_ref[0].astype(F32), inv_ref[...])
        qa = jnp.dot(nxt.h[...], wqa_ref[...], preferred_element_type=F32)
        for r0 in range(0, rows, LANES):
            tables_blk = tuple(tab[r0:r0 + LANES] for tab in shared_vals["tables"])
            for p in range(ATTN_Q_W // LANES):
                nxt.qaT[p * LANES:(p + 1) * LANES, r0:r0 + LANES] = _rope(
                    qa[r0:r0 + LANES, p * LANES:(p + 1) * LANES], tables_blk).T.astype(BF16)

    def p_kva():
        kva = jnp.dot(nxt.h[...], wkva_ref[...], preferred_element_type=F32)
        ka = _rope(kva[:, 0:ATTN_KV_W], shared_vals["tables"])
        va = kva[:, ATTN_KV_W:2 * ATTN_KV_W]
        nxt.kc[...] = ka.astype(BF16)
        nxt.vcT[:, 0:rows] = va.T.astype(BF16)

    def gate_piece(idx):
        per_gate = D_MODEL // MXU_COLS
        gref, wref = (sgm_scr, wgm_ref) if idx < per_gate else (sga_scr, wga_ref)
        c0 = (idx % per_gate) * MXU_COLS
        gref[:, c0:c0 + MXU_COLS] = jax.nn.sigmoid(
            jnp.dot(cur.h[...], wref[:, c0:c0 + MXU_COLS], preferred_element_type=F32))

    qi = lax.broadcasted_iota(jnp.int32, (CHUNK, CHUNK), 0)
    si = lax.broadcasted_iota(jnp.int32, (CHUNK, CHUNK), 1)
    tri = si <= qi
    zeros_c = jnp.zeros((MLSTM_QK_DIM, VEXT_W), BF16)
    ong = ong_ref[...]
    tile_shape = (CHUNK, CHUNK)
    rep_shape = (MLSTM_HEADS, LANES)

    chunk_stats = []
    m_vec = m_scr[0:MLSTM_HEADS, :]
    for c in range(n_chunks):
        g_c = cur.g[:, c * CHUNK:(c + 1) * CHUNK]
        i_r = g_c[0:MLSTM_HEADS]
        logf_r = _log_sigmoid(g_c[MLSTM_HEADS:GATE_ROWS])
        b_r = _lane_cumsum(logf_r)
        b_last = _rep(b_r[:, CHUNK - 1:CHUNK], rep_shape)
        a_r = b_last - b_r + i_r
        a_max = _rep(jnp.max(a_r, axis=1, keepdims=True), rep_shape)
        m_new = jnp.maximum(b_last + m_vec, a_max)
        chunk_stats.append(dict(
            logf_r=logf_r, ib_r=i_r - b_r, m_prev=m_vec,
            decay=jnp.exp(b_last + m_vec - m_new), scale=jnp.exp(a_max - m_new),
            w_loc=jnp.exp(a_r - a_max)))
        m_vec = m_new
    c_state = [c_scr[hd] for hd in range(MLSTM_HEADS)]
    c_next = [None] * MLSTM_HEADS
    s_qk = {}

    def mlstm_scores(c):
        rs = slice(c * CHUNK, (c + 1) * CHUNK)
        for pair in range(MLSTM_HEADS // 2):
            q2 = cur.q[rs, pair * PAIR_W:(pair + 1) * PAIR_W]
            k_even, k_odd = (cur.kT[hd * MLSTM_QK_DIM:(hd + 1) * MLSTM_QK_DIM, rs].astype(BF16)
                             for hd in (2 * pair, 2 * pair + 1))
            zk = jnp.zeros_like(k_even)
            k_both = jnp.concatenate([jnp.concatenate([k_even, zk], axis=1),
                                      jnp.concatenate([zk, k_odd], axis=1)], axis=0)
            s2 = jnp.dot(q2, k_both, preferred_element_type=F32)
            s_qk[c, 2 * pair] = s2[:, 0:CHUNK]
            s_qk[c, 2 * pair + 1] = s2[:, CHUNK:2 * CHUNK]

    def mlstm_state(c):
        rs = slice(c * CHUNK, (c + 1) * CHUNK)
        st = chunk_stats[c]
        for hd in range(MLSTM_HEADS):
            vext = cur.vext[rs, hd * VEXT_W:(hd + 1) * VEXT_W]
            ktw = (cur.kT[hd * MLSTM_QK_DIM:(hd + 1) * MLSTM_QK_DIM, rs] * st["w_loc"][hd:hd + 1]).astype(BF16)
            u = jnp.dot(ktw, vext, preferred_element_type=F32)
            decay = jnp.concatenate([st["decay"][hd:hd + 1]] * 2, axis=1)
            scale = jnp.concatenate([st["scale"][hd:hd + 1]] * 2, axis=1)
            c_next[hd] = decay * c_state[hd] + scale * u

    def mlstm_out(c):
        rs = slice(c * CHUNK, (c + 1) * CHUNK)
        st = chunk_stats[c]
        for hd in range(MLSTM_HEADS):
            pair, odd = divmod(hd, 2)
            q2 = cur.q[rs, pair * PAIR_W:(pair + 1) * PAIR_W]
            vext = cur.vext[rs, hd * VEXT_W:(hd + 1) * VEXT_W]
            b_col = _rep(jnp.sum(jnp.where(tri, st["logf_r"][hd:hd + 1], 0.0), axis=1, keepdims=True),
                         tile_shape)
            log_d = jnp.where(tri, b_col + st["ib_r"][hd:hd + 1], -jnp.inf)
            inter_log = b_col + st["m_prev"][hd:hd + 1]
            m_j = jnp.maximum(inter_log, _rep(jnp.max(log_d, axis=1, keepdims=True), tile_shape))
            sqk = (s_qk.pop((c, hd)) * jnp.exp(log_d - m_j)).astype(BF16)
            c_bf = c_state[hd].astype(BF16)
            c_pad = jnp.concatenate([zeros_c, c_bf] if odd else [c_bf, zeros_c], axis=0)
            q_inter = (q2 * jnp.exp(inter_log - m_j)).astype(BF16)
            tot = jnp.dot(jnp.concatenate([sqk, q_inter], axis=1), jnp.concatenate([vext, c_pad], axis=0),
                          preferred_element_type=F32)
            num = tot[:, 0:MLSTM_V_DIM]
            den = tot[:, MLSTM_V_DIM:VEXT_W]
            hh = num / jnp.maximum(jnp.abs(den), jnp.exp(-m_j))
            vs = slice(hd * MLSTM_V_DIM, (hd + 1) * MLSTM_V_DIM)
            hn = hh * lax.rsqrt(jnp.mean(hh * hh, axis=1, keepdims=True) + NORM_EPS) * ong[:, vs]
            hm_scr[rs, vs] = (hn * cur.so[rs, vs]).astype(BF16)
            c_state[hd] = c_next[hd]

    si2 = lax.broadcasted_iota(jnp.int32, (2 * WINDOW, WINDOW), 0)
    qi2 = lax.broadcasted_iota(jnp.int32, (2 * WINDOW, WINDOW), 1)
    band = (si2 > qi2) & (si2 <= qi2 + WINDOW)
    band_first = band & ((si2 >= WINDOW) | (t_cur > 0))
    zero_q = jnp.zeros((ATTN_HEAD_DIM, group * WINDOW), BF16)
    ones_rows = jnp.ones((SUM_ROWS, 2 * WINDOW), BF16)
    attn = {}

    def attn_scores(nb):
        qs = slice(nb * WINDOW, (nb + 1) * WINDOW)
        mask = band_first if nb == 0 else band
        if nb == 0:
            k_win = jnp.concatenate([pk_scr[...], cur.kc[0:WINDOW, :]], axis=0)
        else:
            k_win = cur.kc[(nb - 1) * WINDOW:(nb + 1) * WINDOW, :]
        for kh in range(ATTN_KV_HEADS):
            q_t = jnp.concatenate([cur.qaT[(kh * group + j) * ATTN_HEAD_DIM:(kh * group + j + 1) * ATTN_HEAD_DIM, qs]
                                   for j in range(group)], axis=1)
            rhs = jnp.concatenate([q_t, zero_q] if kh == 0 else [zero_q, q_t], axis=0)
            s_t = jnp.dot(k_win, rhs, preferred_element_type=F32)
            p_cols, sk_cols = [], []
            for j in range(group):
                sink = sink_ref[kh * group + j]
                sc = jnp.where(mask, s_t[:, j * WINDOW:(j + 1) * WINDOW], -jnp.inf)
                mx = jnp.maximum(jnp.max(sc, axis=0, keepdims=True), sink)
                p_cols.append(jnp.exp(sc - mx).astype(BF16))
                sk_cols.append(jnp.exp(sink - mx))
            attn[nb, kh] = (jnp.concatenate(p_cols, axis=1), jnp.concatenate(sk_cols, axis=1))

    def attn_values(nb):
        qs = slice(nb * WINDOW, (nb + 1) * WINDOW)
        if nb == 0:
            v_win = jnp.concatenate([pvT_scr[...], cur.vcT[:, 0:WINDOW]], axis=1)
        else:
            v_win = cur.vcT[:, (nb - 1) * WINDOW:(nb + 1) * WINDOW]
        for kh in range(ATTN_KV_HEADS):
            probs_t, sink_term = attn.pop((nb, kh))
            lhs = jnp.concatenate([v_win[kh * ATTN_HEAD_DIM:(kh + 1) * ATTN_HEAD_DIM], ones_rows], axis=0)
            o_t = jnp.dot(lhs, probs_t, preferred_element_type=F32)
            on_t = o_t[0:ATTN_HEAD_DIM] / (o_t[ATTN_HEAD_DIM:ATTN_HEAD_DIM + 1] + sink_term)
            for j2 in range(group // 2):
                pair_t = jnp.concatenate([on_t[:, (2 * j2) * WINDOW:(2 * j2 + 1) * WINDOW],
                                          on_t[:, (2 * j2 + 1) * WINDOW:(2 * j2 + 2) * WINDOW]], axis=0)
                pair = kh * (group // 2) + j2
                oa_scr[qs, pair * LANES:(pair + 1) * LANES] = pair_t.T.astype(BF16)

    def branch_attn(idx):
        c0 = idx * MIX_COL_CHUNK
        y_a = jnp.dot(oa_scr[:, 0:ATTN_Q_W], wbra_ref[:, c0:c0 + MIX_COL_CHUNK], preferred_element_type=F32)
        sga_scr[:, c0:c0 + MIX_COL_CHUNK] = sga_scr[:, c0:c0 + MIX_COL_CHUNK] * y_a

    gate_piece(0)
    p_norm()
    side_cast()
    attn_scores(0); gate_piece(1); mlstm_scores(0); mlstm_state(0)
    p_qk()
    attn_values(0); gate_piece(2); mlstm_out(0)
    attn_scores(1); gate_piece(3); mlstm_scores(1); mlstm_state(1)
    p_v()
    attn_values(1); gate_piece(4); mlstm_out(1)
    attn_scores(2); p_so(); mlstm_scores(2); mlstm_state(2)
    attn_values(2); gate_piece(5); mlstm_out(2)
    attn_scores(3); p_qa(); mlstm_scores(3); mlstm_state(3)
    attn_values(3); gate_piece(6)
    p_kva()
    mlstm_out(3); gate_piece(7)
    branch_attn(0); branch_attn(1)

    for hd in range(MLSTM_HEADS):
        c_scr[hd] = c_state[hd]
    m_scr[0:MLSTM_HEADS, :] = m_vec
    pk_scr[...] = cur.kc[rows - WINDOW:rows, :]
    pvT_scr[...] = cur.vcT[:, rows - WINDOW:rows]

    for c0, c1 in _col_chunks(D_MODEL, MIX_COL_CHUNK):
        y_m = jnp.dot(hm_scr[:, 0:MLSTM_V_W], wbrm_ref[:, c0:c1], preferred_element_type=F32)
        mg_scr[:, c0:c1] = (sgm_scr[:, c0:c1] * y_m + sga_scr[:, c0:c1]).astype(BF16)
    o_ref[0] = cur.xs[:, 0:D_MODEL] + jnp.dot(mg_scr[:, 0:D_MODEL], wout_ref[...], preferred_element_type=F32)


def _mixer_body(*refs, tiles_per_seq, n_cast):
    n_in = N_MIXER_IN + n_cast
    n_out = 1 + n_cast
    n_slot = len(Slot._fields)
    ins = refs[:n_in]
    outs = refs[n_in:n_in + n_out]
    scr = refs[n_in + n_out:]
    slot_a, slot_b = Slot(*scr[:n_slot]), Slot(*scr[n_slot:2 * n_slot])
    shared = scr[2 * n_slot:]
    prev_blocks = shared[0:2]
    c_scr, m_scr = shared[-2:]
    s = pl.program_id(0)
    t_cur = lax.rem(s + tiles_per_seq - 1, tiles_per_seq)

    @pl.when(s == 0)
    def _():
        for ref in list(slot_b) + list(shared):
            ref[...] = jnp.zeros_like(ref)

    @pl.when(t_cur == 0)
    def _():
        c_scr[...] = jnp.zeros_like(c_scr)
        m_scr[...] = jnp.zeros_like(m_scr)
        for ref in prev_blocks:
            ref[...] = jnp.zeros_like(ref)

    even = lax.rem(s, 2) == 0

    @pl.when(even)
    def _():
        _mixer_step(t_cur, ins, slot_b, slot_a, shared, outs)

    @pl.when(jnp.logical_not(even))
    def _():
        _mixer_step(t_cur, ins, slot_a, slot_b, shared, outs)


def _cast_slabs(arr, steps):
    rows = arr.shape[0]
    slab = 16
    while rows % slab or rows // slab > steps:
        slab += 16
    n_slabs = rows // slab
    return pl.BlockSpec((slab, arr.shape[1]), lambda i: (jnp.minimum(i, n_slabs - 1), 0))


def _mixer(x, positions, norm_g, b_i, b_f, out_norm_g, sinks, weights, cast_weights):
    b, s, d = x.shape
    assert s % MIX_ROWS == 0 and d == D_MODEL and positions.shape == (b, s)
    w_mix, w_br = weights
    b_if = jnp.concatenate([b_i, b_f]).reshape(GATE_ROWS, 1)
    half = ROPE_DIM // 2
    inv = (ROPE_THETA ** (-jnp.arange(half, dtype=F32) * 2.0 / ROPE_DIM)).reshape(half, 1)

    rows = MIX_ROWS
    tiles_per_seq = s // rows
    n_tiles = b * tiles_per_seq
    def in_tile(i):
        j = jnp.minimum(i, n_tiles - 1)
        return j // tiles_per_seq, j % tiles_per_seq

    def out_tile(i):
        j = jnp.maximum(i - 1, 0)
        return j // tiles_per_seq, j % tiles_per_seq

    in_specs = [
        pl.BlockSpec((1, rows, d), lambda i: (*in_tile(i), 0)),
        pl.BlockSpec((1, 1, rows), lambda i: (in_tile(i)[0], 0, in_tile(i)[1])),
        _const_spec((1, d)), _const_spec((half, 1)), _const_spec((GATE_ROWS, 1)),
        _const_spec((1, MLSTM_V_W)),
        pl.BlockSpec(memory_space=pltpu.SMEM),
        _const_spec((d, MIX_W)), _const_spec((MLSTM_V_W, 2 * d)),
    ]
    slot = list(_slot_shapes(rows, d))
    shared = [
        pltpu.VMEM((WINDOW, LANES), BF16),
        pltpu.VMEM((LANES, WINDOW), BF16),
        pltpu.VMEM((rows, d + PITCH_PAD), F32),
        pltpu.VMEM((rows, d + PITCH_PAD), F32),
        pltpu.VMEM((rows, MLSTM_V_W + PITCH_PAD), BF16),
        pltpu.VMEM((rows, ATTN_Q_W + PITCH_PAD), BF16),
        pltpu.VMEM((rows, d + PITCH_PAD), BF16),
        pltpu.VMEM((MLSTM_HEADS, MLSTM_QK_DIM, VEXT_W), F32),
        pltpu.VMEM((SUBLANES, LANES), F32),
    ]
    assert len(in_specs) == N_MIXER_IN
    steps = n_tiles + 1
    cast_specs = [_cast_slabs(arr, steps) for arr in cast_weights]
    res = pl.pallas_call(
        functools.partial(_mixer_body, tiles_per_seq=tiles_per_seq, n_cast=len(cast_weights)),
        grid=(steps,),
        in_specs=in_specs + cast_specs,
        out_specs=[pl.BlockSpec((1, rows, d), lambda i: (*out_tile(i), 0))] + cast_specs,
        out_shape=[jax.ShapeDtypeStruct((b, s, d), F32)]
        + [jax.ShapeDtypeStruct(arr.shape, BF16) for arr in cast_weights],
        scratch_shapes=slot + slot + shared,
        compiler_params=pltpu.CompilerParams(
            dimension_semantics=("arbitrary",), vmem_limit_bytes=(7 * VMEM_BYTES_V7X) // 8),
        name="mixer",
    )(x, positions.reshape(b, 1, s), norm_g.reshape(1, d), inv, b_if, out_norm_g.reshape(1, MLSTM_V_W),
      sinks, w_mix, w_br, *cast_weights)
    return res[0], res[1:]


def kernel(x, positions, ffn1_norm_g, ffn1_w_gate, ffn1_w_up, ffn1_w_down, mix_norm_g, w_in, mlstm_b_i, mlstm_b_f, mlstm_out_norm_g, attn_sinks, w_branch_mlstm, w_branch_attn, w_out, ffn2_norm_g, ffn2_w_gate, ffn2_w_up, ffn2_w_down, final_norm_g):
    b, s, d = x.shape
    depth = w_in.shape[0]
    for l in range(depth):
        last = l == depth - 1
        x2, mixer_w = _ffn(x.reshape(b * s, d), ffn1_norm_g[l], ffn1_w_gate[l], ffn1_w_up[l], ffn1_w_down[l],
                           mixer_weights=(w_in[l], w_branch_mlstm[l], w_branch_attn[l], w_out[l]))
        x3, ffn2_w = _mixer(x2.reshape(b, s, d), positions, mix_norm_g[l], mlstm_b_i[l], mlstm_b_f[l],
                            mlstm_out_norm_g[l], attn_sinks[l], mixer_w,
                            (ffn2_w_gate[l], ffn2_w_up[l], ffn2_w_down[l]))
        x = _ffn(x3.reshape(b * s, d), ffn2_norm_g[l], *ffn2_w,
                 final_g=final_norm_g if last else None).reshape(b, s, d)
    return x
```

```python
import collections
import functools

import jax
import jax.numpy as jnp
from jax import lax
from jax.experimental import pallas as pl
from jax.experimental.pallas import tpu as pltpu

D_MODEL = 1024
D_FF = 2816
MLSTM_HEADS = 4
MLSTM_QK_DIM = 64
MLSTM_V_DIM = 128
GATE_SOFTCAP = 15.0
ATTN_Q_HEADS = 8
ATTN_KV_HEADS = 2
ATTN_HEAD_DIM = 64
WINDOW = 128
ROPE_DIM = ATTN_HEAD_DIM // 4
ROPE_THETA = 500000.0
NORM_EPS = 1e-6

MLSTM_QK_W = MLSTM_HEADS * MLSTM_QK_DIM
MLSTM_V_W = MLSTM_HEADS * MLSTM_V_DIM
ATTN_Q_W = ATTN_Q_HEADS * ATTN_HEAD_DIM
ATTN_KV_W = ATTN_KV_HEADS * ATTN_HEAD_DIM
IN_WIDTHS = (MLSTM_QK_W, MLSTM_QK_W, MLSTM_V_W, MLSTM_V_W, MLSTM_HEADS, MLSTM_HEADS,
             ATTN_Q_W, ATTN_KV_W, ATTN_KV_W, D_MODEL, D_MODEL)

LANES = 128
SUBLANES = 8
MXU_COLS = 256
VMEM_BYTES_V7X = 64 * 1024 * 1024

FFN_ROWS = 1024
FFN_COL_CHUNK = 2 * MXU_COLS
STAGE_SLOTS = 4
STAGE_ROWS_IN = 64
STAGE_ROWS_OUT = 176
MIX_ROWS = 512
MIX_COL_CHUNK = 2 * MXU_COLS
CHUNK = LANES
PAIR_W = 2 * MLSTM_QK_DIM
VEXT_W = 2 * MLSTM_V_DIM
SUM_ROWS = 16
GATE_ROWS = 2 * MLSTM_HEADS
QKG_W = 2 * MLSTM_QK_W + LANES
N_MIXER_IN = 9
PITCH_PAD = LANES
MIX_PIECES = (("qkg", QKG_W), ("v", MLSTM_V_W), ("o", MLSTM_V_W), ("qa", ATTN_Q_W), ("kva", 2 * ATTN_KV_W),
              ("gm", D_MODEL), ("ga", D_MODEL), ("out", D_MODEL))
MIX_W = sum(w for _, w in MIX_PIECES)

F32 = jnp.float32
BF16 = jnp.bfloat16


def _rms(x, g):
    return x * lax.rsqrt(jnp.mean(x * x, axis=-1, keepdims=True) + NORM_EPS) * g


def _col_chunks(total, width):
    return [(c, min(c + width, total)) for c in range(0, total, width)]


def _const_spec(shape):
    return pl.BlockSpec(shape, lambda *_: (0,) * len(shape), pipeline_mode=pl.Buffered(1))


def _mix_piece(ref, name):
    start = 0
    for piece, width in MIX_PIECES:
        if piece == name:
            return ref.at[:, start:start + width]
        start += width
    raise KeyError(name)


def _prep_mixer_weights(win_ref, wbrm_ref, wbra_ref, wout_ref, wmix_o, wbr_o):
    starts = [0]
    for w in IN_WIDTHS:
        starts.append(starts[-1] + w)
    col = lambda i: win_ref[:, starts[i]:starts[i + 1]]
    gate_pad = jnp.zeros((win_ref.shape[0], LANES - GATE_ROWS), F32)
    pieces = dict(
        qkg=jnp.concatenate([col(0) * (MLSTM_QK_DIM ** -0.5), col(1), col(4), col(5), gate_pad], axis=1),
        v=col(2), o=col(3), qa=col(6) * (ATTN_HEAD_DIM ** -0.5),
        kva=jnp.concatenate([col(7), col(8)], axis=1), gm=col(9), ga=col(10), out=wout_ref[...])
    for name, _ in MIX_PIECES:
        _mix_piece(wmix_o, name)[...] = pieces[name].astype(BF16)
    wbr_o[:, 0:D_MODEL] = wbrm_ref[...].astype(BF16)
    wbr_o[:, D_MODEL:2 * D_MODEL] = wbra_ref[...].astype(BF16)


def _stream_cast(src_hbm, dst_ref, stage_ref, sem_ref, chunk_rows):
    n_chunks = src_hbm.shape[0] // chunk_rows
    ahead = STAGE_SLOTS - 1

    def copy(k):
        slot = k % STAGE_SLOTS
        return pltpu.make_async_copy(src_hbm.at[k * chunk_rows:(k + 1) * chunk_rows], stage_ref.at[slot],
                                     sem_ref.at[slot])
    for k in range(min(ahead, n_chunks)):
        copy(k).start()
    for k in range(n_chunks):
        if k + ahead < n_chunks:
            copy(k + ahead).start()
        copy(k).wait()
        dst_ref[k * chunk_rows:(k + 1) * chunk_rows, :] = stage_ref[k % STAGE_SLOTS].astype(BF16)


def _ffn_body(*refs, final_norm, prep, f32_weights):
    n_in = 5 + int(final_norm) + (4 if prep else 0)
    x_ref, g_ref, wg_ref, wu_ref, wd_ref = refs[:5]
    o_ref = refs[n_in]
    act_ref = refs[-1]
    if f32_weights:
        wg_scr, wu_scr, wd_scr, stage_in, stage_out, sems = refs[-7:-1]

        @pl.when(pl.program_id(0) == 0)
        def _():
            _stream_cast(wg_ref, wg_scr, stage_in, sems, STAGE_ROWS_IN)
            _stream_cast(wu_ref, wu_scr, stage_in, sems, STAGE_ROWS_IN)
            _stream_cast(wd_ref, wd_scr, stage_out, sems, STAGE_ROWS_OUT)
        wg_ref, wu_ref, wd_ref = wg_scr, wu_scr, wd_scr
    half = FFN_ROWS // 2
    for r0 in (0, half):
        rs = slice(r0, r0 + half)
        x = x_ref[rs, :]
        h = _rms(x, g_ref[...]).astype(BF16)
        for c0, c1 in _col_chunks(D_FF, FFN_COL_CHUNK):
            g = jnp.dot(h, wg_ref[:, c0:c1], preferred_element_type=F32)
            u = jnp.dot(h, wu_ref[:, c0:c1], preferred_element_type=F32)
            act_ref[rs, c0:c1] = (g * jax.nn.sigmoid(g) * u).astype(BF16)
        if prep and r0 == 0:
            _prep_mixer_weights(*refs[n_in - 4:n_in], *refs[n_in + 1:n_in + 3])
        r = x + 0.5 * jnp.dot(act_ref[rs, :], wd_ref[...], preferred_element_type=F32)
        if final_norm:
            r = _rms(r, refs[5][...])
        o_ref[rs, :] = r


def _ffn(x2, norm_g, w_gate, w_up, w_down, final_g=None, mixer_weights=None):
    n, d = x2.shape
    assert n % FFN_ROWS == 0 and d == D_MODEL and w_gate.shape == (d, D_FF) and w_down.shape == (D_FF, d)
    steps = n // FFN_ROWS
    final_norm = final_g is not None
    prep = mixer_weights is not None
    f32_weights = w_gate.dtype == F32
    assert w_up.dtype == w_gate.dtype and w_down.dtype == w_gate.dtype
    row_spec = pl.BlockSpec((FFN_ROWS, d), lambda i: (i, 0))
    weight_specs = ([pl.BlockSpec(memory_space=pl.ANY)] * 3 if f32_weights else
                    [_const_spec((d, D_FF)), _const_spec((d, D_FF)), _const_spec((D_FF, d))])
    in_specs = [row_spec, _const_spec((1, d))] + weight_specs
    args = [x2, norm_g.reshape(1, d), w_gate, w_up, w_down]
    out_specs = [row_spec]
    out_shape = [jax.ShapeDtypeStruct((n, d), F32)]
    if final_norm:
        in_specs.append(_const_spec((1, d)))
        args.append(final_g.reshape(1, d))
    if prep:
        slab = lambda arr: pl.BlockSpec((arr.shape[0] // steps, arr.shape[1]), lambda i: (i, 0))
        for arr in mixer_weights:
            in_specs.append(slab(arr))
            args.append(arr)
        packed = [jax.ShapeDtypeStruct((d, MIX_W), BF16), jax.ShapeDtypeStruct((MLSTM_V_W, 2 * d), BF16)]
        out_shape += packed
        out_specs += [slab(p) for p in packed]
    scratch = []
    if f32_weights:
        assert d % STAGE_ROWS_IN == 0 and D_FF % STAGE_ROWS_OUT == 0
        scratch = [pltpu.VMEM((d, D_FF), BF16), pltpu.VMEM((d, D_FF), BF16), pltpu.VMEM((D_FF, d), BF16),
                   pltpu.VMEM((STAGE_SLOTS, STAGE_ROWS_IN, D_FF), F32),
                   pltpu.VMEM((STAGE_SLOTS, STAGE_ROWS_OUT, d), F32),
                   pltpu.SemaphoreType.DMA((STAGE_SLOTS,))]
    scratch.append(pltpu.VMEM((FFN_ROWS, D_FF), BF16))
    vmem_eighths = 7 if f32_weights else 6
    res = pl.pallas_call(
        functools.partial(_ffn_body, final_norm=final_norm, prep=prep, f32_weights=f32_weights),
        grid=(steps,),
        in_specs=in_specs,
        out_specs=out_specs,
        out_shape=out_shape,
        scratch_shapes=scratch,
        compiler_params=pltpu.CompilerParams(
            dimension_semantics=("arbitrary",), vmem_limit_bytes=(vmem_eighths * VMEM_BYTES_V7X) // 8),
        name="ffn_final" if final_norm else "ffn",
    )(*args)
    return (res[0], res[1:]) if prep else res[0]


Slot = collections.namedtuple("Slot", "xs h q kT vext so qaT kc vcT g")


def _slot_shapes(rows, d):
    return Slot(
        xs=pltpu.VMEM((rows, d), F32),
        h=pltpu.VMEM((rows, d), BF16),
        q=pltpu.VMEM((rows, MLSTM_QK_W + PITCH_PAD), BF16),
        kT=pltpu.VMEM((MLSTM_QK_W, rows + PITCH_PAD), F32),
        vext=pltpu.VMEM((rows, MLSTM_HEADS * VEXT_W + PITCH_PAD), BF16),
        so=pltpu.VMEM((rows, MLSTM_V_W + PITCH_PAD), F32),
        qaT=pltpu.VMEM((ATTN_Q_W, rows + PITCH_PAD), BF16),
        kc=pltpu.VMEM((rows, LANES), BF16),
        vcT=pltpu.VMEM((LANES, rows + PITCH_PAD), BF16),
        g=pltpu.VMEM((GATE_ROWS, rows), F32),
    )


def _softcap(a):
    return GATE_SOFTCAP * jnp.tanh(a / GATE_SOFTCAP)


def _log_sigmoid(a):
    return jnp.minimum(a, 0.0) - jnp.log1p(jnp.exp(-jnp.abs(a)))


def _lane_cumsum(a):
    lane = lax.broadcasted_iota(jnp.int32, a.shape, 1)
    d = 1
    while d < LANES:
        a = a + jnp.where(lane >= d, pltpu.roll(a, d, 1), 0.0)
        d *= 2
    return a


def _rope_tables(pos_row, inv_col):
    ang = inv_col * pos_row
    c, s = jnp.cos(ang), jnp.sin(ang)
    one, zero = jnp.ones_like(c), jnp.zeros_like(c)
    per_head = ATTN_HEAD_DIM // SUBLANES

    def tile(first, second, rest):
        rows = []
        for _ in range(LANES // ATTN_HEAD_DIM):
            rows += [first, second] + [rest] * (per_head - 2)
        return jnp.concatenate(rows, axis=0).T
    return tile(c, c, one), tile(zero, s, zero), tile(-s, zero, zero)


def _rope(a, tables):
    cos_t, sin_up, sin_dn = tables
    half = ROPE_DIM // 2
    return a * cos_t + pltpu.roll(a, half, 1) * sin_up + pltpu.roll(a, LANES - half, 1) * sin_dn


def _rep(a, shape):
    return jnp.broadcast_to(a, shape)


def _mixer_step(t_cur, ins, cur, nxt, shared, outs):
    x_ref, pos_ref, ng_ref, inv_ref, bif_ref, ong_ref, sink_ref, wmix_ref, wbr_ref = ins[:N_MIXER_IN]
    wqk_ref, wv_ref, wo_ref, wqa_ref, wkva_ref, wgm_ref, wga_ref, wout_ref = (
        _mix_piece(wmix_ref, name) for name, _ in MIX_PIECES)
    wbrm_ref, wbra_ref = wbr_ref.at[:, 0:D_MODEL], wbr_ref.at[:, D_MODEL:2 * D_MODEL]
    cast_in = ins[N_MIXER_IN:]
    o_ref, cast_out = outs[0], outs[1:]
    pk_scr, pvT_scr, sgm_scr, sga_scr, hm_scr, oa_scr, mg_scr, c_scr, m_scr = shared
    rows = MIX_ROWS
    n_chunks = rows // CHUNK
    n_blocks = rows // WINDOW
    group = ATTN_Q_HEADS // ATTN_KV_HEADS

    shared_vals = {}

    def p_norm():
        x_new = x_ref[0]
        nxt.xs[...] = x_new
        nxt.h[...] = _rms(x_new, ng_ref[...]).astype(BF16)

    def side_cast():
        for src, dst in zip(cast_in, cast_out):
            dst[...] = src[...].astype(BF16)

    def p_qk():
        qkg = jnp.dot(nxt.h[...], wqk_ref[...], preferred_element_type=F32)
        nxt.q[:, 0:MLSTM_QK_W] = qkg[:, 0:MLSTM_QK_W].astype(BF16)
        nxt.kT[:, 0:rows] = qkg[:, MLSTM_QK_W:2 * MLSTM_QK_W].T
        gates_t = qkg[:, 2 * MLSTM_QK_W:QKG_W].T
        nxt.g[...] = _softcap(gates_t[0:GATE_ROWS] + bif_ref[...])

    def p_v():
        v = jnp.dot(nxt.h[...], wv_ref[...], preferred_element_type=F32)
        for hd in range(MLSTM_HEADS):
            nxt.vext[:, hd * VEXT_W:hd * VEXT_W + MLSTM_V_DIM] = (
                v[:, hd * MLSTM_V_DIM:(hd + 1) * MLSTM_V_DIM].astype(BF16))
            nxt.vext[:, hd * VEXT_W + MLSTM_V_DIM:(hd + 1) * VEXT_W] = jnp.ones((rows, MLSTM_V_DIM), BF16)

    def p_so():
        nxt.so[:, 0:MLSTM_V_W] = jax.nn.sigmoid(jnp.dot(nxt.h[...], wo_ref[...], preferred_element_type=F32))

    def p_qa():
        shared_vals["tables"] = _rope_tables(pos_ref[0].astype(F32), inv_ref[...])
        qa = jnp.dot(nxt.h[...], wqa_ref[...], preferred_element_type=F32)
        for r0 in range(0, rows, LANES):
            tables_blk = tuple(tab[r0:r0 + LANES] for tab in shared_vals["tables"])
            for p in range(ATTN_Q_W // LANES):
                nxt.qaT[p * LANES:(p + 1) * LANES, r0:r0 + LANES] = _rope(
                    qa[r0:r0 + LANES, p * LANES:(p + 1) * LANES], tables_blk).T.astype(BF16)

    def p_kva():
        kva = jnp.dot(nxt.h[...], wkva_ref[...], preferred_element_type=F32)
        ka = _rope(kva[:, 0:ATTN_KV_W], shared_vals["tables"])
        va = kva[:, ATTN_KV_W:2 * ATTN_KV_W]
        nxt.kc[...] = ka.astype(BF16)
        nxt.vcT[:, 0:rows] = va.T.astype(BF16)

    def gate_piece(idx):
        per_gate = D_MODEL // MXU_COLS
        gref, wref = (sgm_scr, wgm_ref) if idx < per_gate else (sga_scr, wga_ref)
        c0 = (idx % per_gate) * MXU_COLS
        gref[:, c0:c0 + MXU_COLS] = jax.nn.sigmoid(
            jnp.dot(cur.h[...], wref[:, c0:c0 + MXU_COLS], preferred_element_type=F32))

    qi = lax.broadcasted_iota(jnp.int32, (CHUNK, CHUNK), 0)
    si = lax.broadcasted_iota(jnp.int32, (CHUNK, CHUNK), 1)
    tri = si <= qi
    zeros_c = jnp.zeros((MLSTM_QK_DIM, VEXT_W), BF16)
    ong = ong_ref[...]
    tile_shape = (CHUNK, CHUNK)
    rep_shape = (MLSTM_HEADS, LANES)

    chunk_stats = []
    m_vec = m_scr[0:MLSTM_HEADS, :]
    for c in range(n_chunks):
        g_c = cur.g[:, c * CHUNK:(c + 1) * CHUNK]
        i_r = g_c[0:MLSTM_HEADS]
        logf_r = _log_sigmoid(g_c[MLSTM_HEADS:GATE_ROWS])
        b_r = _lane_cumsum(logf_r)
        b_last = _rep(b_r[:, CHUNK - 1:CHUNK], rep_shape)
        a_r = b_last - b_r + i_r
        a_max = _rep(jnp.max(a_r, axis=1, keepdims=True), rep_shape)
        m_new = jnp.maximum(b_last + m_vec, a_max)
        chunk_stats.append(dict(
            logf_r=logf_r, ib_r=i_r - b_r, m_prev=m_vec,
            decay=jnp.exp(b_last + m_vec - m_new), scale=jnp.exp(a_max - m_new),
            w_loc=jnp.exp(a_r - a_max)))
        m_vec = m_new
    c_state = [c_scr[hd] for hd in range(MLSTM_HEADS)]
    c_next = [None] * MLSTM_HEADS
    s_qk = {}

    def mlstm_scores(c):
        rs = slice(c * CHUNK, (c + 1) * CHUNK)
        for pair in range(MLSTM_HEADS // 2):
            q2 = cur.q[rs, pair * PAIR_W:(pair + 1) * PAIR_W]
            k_even, k_odd = (cur.kT[hd * MLSTM_QK_DIM:(hd + 1) * MLSTM_QK_DIM, rs].astype(BF16)
                             for hd in (2 * pair, 2 * pair + 1))
            zk = jnp.zeros_like(k_even)
            k_both = jnp.concatenate([jnp.concatenate([k_even, zk], axis=1),
                                      jnp.concatenate([zk, k_odd], axis=1)], axis=0)
            s2 = jnp.dot(q2, k_both, preferred_element_type=F32)
            s_qk[c, 2 * pair] = s2[:, 0:CHUNK]
            s_qk[c, 2 * pair + 1] = s2[:, CHUNK:2 * CHUNK]

    def mlstm_state(c):
        rs = slice(c * CHUNK, (c + 1) * CHUNK)
        st = chunk_stats[c]
        for hd in range(MLSTM_HEADS):
            vext = cur.vext[rs, hd * VEXT_W:(hd + 1) * VEXT_W]
            ktw = (cur.kT[hd * MLSTM_QK_DIM:(hd + 1) * MLSTM_QK_DIM, rs] * st["w_loc"][hd:hd + 1]).astype(BF16)
            u = jnp.dot(ktw, vext, preferred_element_type=F32)
            decay = jnp.concatenate([st["decay"][hd:hd + 1]] * 2, axis=1)
            scale = jnp.concatenate([st["scale"][hd:hd + 1]] * 2, axis=1)
            c_next[hd] = decay * c_state[hd] + scale * u

    def mlstm_out(c):
        rs = slice(c * CHUNK, (c + 1) * CHUNK)
        st = chunk_stats[c]
        for hd in range(MLSTM_HEADS):
            pair, odd = divmod(hd, 2)
            q2 = cur.q[rs, pair * PAIR_W:(pair + 1) * PAIR_W]
            vext = cur.vext[rs, hd * VEXT_W:(hd + 1) * VEXT_W]
            b_col = _rep(jnp.sum(jnp.where(tri, st["logf_r"][hd:hd + 1], 0.0), axis=1, keepdims=True),
                         tile_shape)
            log_d = jnp.where(tri, b_col + st["ib_r"][hd:hd + 1], -jnp.inf)
            inter_log = b_col + st["m_prev"][hd:hd + 1]
            m_j = jnp.maximum(inter_log, _rep(jnp.max(log_d, axis=1, keepdims=True), tile_shape))
            sqk = (s_qk.pop((c, hd)) * jnp.exp(log_d - m_j)).astype(BF16)
            c_bf = c_state[hd].astype(BF16)
            c_pad = jnp.concatenate([zeros_c, c_bf] if odd else [c_bf, zeros_c], axis=0)
            q_inter = (q2 * jnp.exp(inter_log - m_j)).astype(BF16)
            tot = jnp.dot(jnp.concatenate([sqk, q_inter], axis=1), jnp.concatenate([vext, c_pad], axis=0),
                          preferred_element_type=F32)
            num = tot[:, 0:MLSTM_V_DIM]
            den = tot[:, MLSTM_V_DIM:VEXT_W]
            hh = num / jnp.maximum(jnp.abs(den), jnp.exp(-m_j))
            vs = slice(hd * MLSTM_V_DIM, (hd + 1) * MLSTM_V_DIM)
            hn = hh * lax.rsqrt(jnp.mean(hh * hh, axis=1, keepdims=True) + NORM_EPS) * ong[:, vs]
            hm_scr[rs, vs] = (hn * cur.so[rs, vs]).astype(BF16)
            c_state[hd] = c_next[hd]

    si2 = lax.broadcasted_iota(jnp.int32, (2 * WINDOW, WINDOW), 0)
    qi2 = lax.broadcasted_iota(jnp.int32, (2 * WINDOW, WINDOW), 1)
    band = (si2 > qi2) & (si2 <= qi2 + WINDOW)
    band_first = band & ((si2 >= WINDOW) | (t_cur > 0))
    zero_q = jnp.zeros((ATTN_HEAD_DIM, group * WINDOW), BF16)
    ones_rows = jnp.ones((SUM_ROWS, 2 * WINDOW), BF16)
    attn = {}

    def attn_scores(nb):
        qs = slice(nb * WINDOW, (nb + 1) * WINDOW)
        mask = band_first if nb == 0 else band
        if nb == 0:
            k_win = jnp.concatenate([pk_scr[...], cur.kc[0:WINDOW, :]], axis=0)
        else:
            k_win = cur.kc[(nb - 1) * WINDOW:(nb + 1) * WINDOW, :]
        for kh in range(ATTN_KV_HEADS):
            q_t = jnp.concatenate([cur.qaT[(kh * group + j) * ATTN_HEAD_DIM:(kh * group + j + 1) * ATTN_HEAD_DIM, qs]
                                   for j in range(group)], axis=1)
            rhs = jnp.concatenate([q_t, zero_q] if kh == 0 else [zero_q, q_t], axis=0)
            s_t = jnp.dot(k_win, rhs, preferred_element_type=F32)
            p_cols, sk_cols = [], []
            for j in range(group):
                sink = sink_ref[kh * group + j]
                sc = jnp.where(mask, s_t[:, j * WINDOW:(j + 1) * WINDOW], -jnp.inf)
                mx = jnp.maximum(jnp.max(sc, axis=0, keepdims=True), sink)
                p_cols.append(jnp.exp(sc - mx).astype(BF16))
                sk_cols.append(jnp.exp(sink - mx))
            attn[nb, kh] = (jnp.concatenate(p_cols, axis=1), jnp.concatenate(sk_cols, axis=1))

    def attn_values(nb):
        qs = slice(nb * WINDOW, (nb + 1) * WINDOW)
        if nb == 0:
            v_win = jnp.concatenate([pvT_scr[...], cur.vcT[:, 0:WINDOW]], axis=1)
        else:
            v_win = cur.vcT[:, (nb - 1) * WINDOW:(nb + 1) * WINDOW]
        for kh in range(ATTN_KV_HEADS):
            probs_t, sink_term = attn.pop((nb, kh))
            lhs = jnp.concatenate([v_win[kh * ATTN_HEAD_DIM:(kh + 1) * ATTN_HEAD_DIM], ones_rows], axis=0)
            o_t = jnp.dot(lhs, probs_t, preferred_element_type=F32)
            on_t = o_t[0:ATTN_HEAD_DIM] / (o_t[ATTN_HEAD_DIM:ATTN_HEAD_DIM + 1] + sink_term)
            for j2 in range(group // 2):
                pair_t = jnp.concatenate([on_t[:, (2 * j2) * WINDOW:(2 * j2 + 1) * WINDOW],
                                          on_t[:, (2 * j2 + 1) * WINDOW:(2 * j2 + 2) * WINDOW]], axis=0)
                pair = kh * (group // 2) + j2
                oa_scr[qs, pair * LANES:(pair + 1) * LANES] = pair_t.T.astype(BF16)

    def branch_attn(idx):
        c0 = idx * MIX_COL_CHUNK
        y_a = jnp.dot(oa_scr[:, 0:ATTN_Q_W], wbra_ref[:, c0:c0 + MIX_COL_CHUNK], preferred_element_type=F32)
        sga_scr[:, c0:c0 + MIX_COL_CHUNK] = sga_scr[:, c0:c0 + MIX_COL_CHUNK] * y_a

    gate_piece(0)
    p_norm()
    side_cast()
    attn_scores(0); gate_piece(1); mlstm_scores(0); mlstm_state(0)
    p_qk()
    attn_values(0); gate_piece(2); mlstm_out(0)
    attn_scores(1); gate_piece(3); mlstm_scores(1); mlstm_state(1)
    p_v()
    attn_values(1); gate_piece(4); mlstm_out(1)
    attn_scores(2); p_so(); mlstm_scores(2); mlstm_state(2)
    attn_values(2); gate_piece(5); mlstm_out(2)
    attn_scores(3); p_qa(); mlstm_scores(3); mlstm_state(3)
    attn_values(3); gate_piece(6)
    p_kva()
    mlstm_out(3); gate_piece(7)
    branch_attn(0); branch_attn(1)

    for hd in range(MLSTM_HEADS):
        c_scr[hd] = c_state[hd]
    m_scr[0:MLSTM_HEADS, :] = m_vec
    pk_scr[...] = cur.kc[rows - WINDOW:rows, :]
    pvT_scr[...] = cur.vcT[:, rows - WINDOW:rows]

    for c0, c1 in _col_chunks(D_MODEL, MIX_COL_CHUNK):
        y_m = jnp.dot(hm_scr[:, 0:MLSTM_V_W], wbrm_ref[:, c0:c1], preferred_element_type=F32)
        mg_scr[:, c0:c1] = (sgm_scr[:, c0:c1] * y_m + sga_scr[:, c0:c1]).astype(BF16)
    o_ref[0] = cur.xs[...] + jnp.dot(mg_scr[:, 0:D_MODEL], wout_ref[...], preferred_element_type=F32)


def _mixer_body(*refs, tiles_per_seq, n_cast):
    n_in = N_MIXER_IN + n_cast
    n_out = 1 + n_cast
    n_slot = len(Slot._fields)
    ins = refs[:n_in]
    outs = refs[n_in:n_in + n_out]
    scr = refs[n_in + n_out:]
    slot_a, slot_b = Slot(*scr[:n_slot]), Slot(*scr[n_slot:2 * n_slot])
    shared = scr[2 * n_slot:]
    prev_blocks = shared[0:2]
    c_scr, m_scr = shared[-2:]
    s = pl.program_id(0)
    t_cur = lax.rem(s + tiles_per_seq - 1, tiles_per_seq)

    @pl.when(s == 0)
    def _():
        for ref in list(slot_b) + list(shared):
            ref[...] = jnp.zeros_like(ref)

    @pl.when(t_cur == 0)
    def _():
        c_scr[...] = jnp.zeros_like(c_scr)
        m_scr[...] = jnp.zeros_like(m_scr)
        for ref in prev_blocks:
            ref[...] = jnp.zeros_like(ref)

    even = lax.rem(s, 2) == 0

    @pl.when(even)
    def _():
        _mixer_step(t_cur, ins, slot_b, slot_a, shared, outs)

    @pl.when(jnp.logical_not(even))
    def _():
        _mixer_step(t_cur, ins, slot_a, slot_b, shared, outs)


def _cast_slabs(arr, steps):
    rows = arr.shape[0]
    slab = 16
    while rows % slab or rows // slab > steps:
        slab += 16
    n_slabs = rows // slab
    return pl.BlockSpec((slab, arr.shape[1]), lambda i: (jnp.minimum(i, n_slabs - 1), 0))


def _mixer(x, positions, norm_g, b_i, b_f, out_norm_g, sinks, weights, cast_weights):
    b, s, d = x.shape
    assert s % MIX_ROWS == 0 and d == D_MODEL and positions.shape == (b, s)
    w_mix, w_br = weights
    b_if = jnp.concatenate([b_i, b_f]).reshape(GATE_ROWS, 1)
    half = ROPE_DIM // 2
    inv = (ROPE_THETA ** (-jnp.arange(half, dtype=F32) * 2.0 / ROPE_DIM)).reshape(half, 1)

    rows = MIX_ROWS
    tiles_per_seq = s // rows
    n_tiles = b * tiles_per_seq
    def in_tile(i):
        j = jnp.minimum(i, n_tiles - 1)
        return j // tiles_per_seq, j % tiles_per_seq

    def out_tile(i):
        j = jnp.maximum(i - 1, 0)
        return j // tiles_per_seq, j % tiles_per_seq

    in_specs = [
        pl.BlockSpec((1, rows, d), lambda i: (*in_tile(i), 0)),
        pl.BlockSpec((1, 1, rows), lambda i: (in_tile(i)[0], 0, in_tile(i)[1])),
        _const_spec((1, d)), _const_spec((half, 1)), _const_spec((GATE_ROWS, 1)),
        _const_spec((1, MLSTM_V_W)),
        pl.BlockSpec(memory_space=pltpu.SMEM),
        _const_spec((d, MIX_W)), _const_spec((MLSTM_V_W, 2 * d)),
    ]
    slot = list(_slot_shapes(rows, d))
    shared = [
        pltpu.VMEM((WINDOW, LANES), BF16),
        pltpu.VMEM((LANES, WINDOW), BF16),
        pltpu.VMEM((rows, d + PITCH_PAD), F32),
        pltpu.VMEM((rows, d + PITCH_PAD), F32),
        pltpu.VMEM((rows, MLSTM_V_W + PITCH_PAD), BF16),
        pltpu.VMEM((rows, ATTN_Q_W + PITCH_PAD), BF16),
        pltpu.VMEM((rows, d + PITCH_PAD), BF16),
        pltpu.VMEM((MLSTM_HEADS, MLSTM_QK_DIM, VEXT_W), F32),
        pltpu.VMEM((SUBLANES, LANES), F32),
    ]
    assert len(in_specs) == N_MIXER_IN
    steps = n_tiles + 1
    cast_specs = [_cast_slabs(arr, steps) for arr in cast_weights]
    res = pl.pallas_call(
        functools.partial(_mixer_body, tiles_per_seq=tiles_per_seq, n_cast=len(cast_weights)),
        grid=(steps,),
        in_specs=in_specs + cast_specs,
        out_specs=[pl.BlockSpec((1, rows, d), lambda i: (*out_tile(i), 0))] + cast_specs,
        out_shape=[jax.ShapeDtypeStruct((b, s, d), F32)]
        + [jax.ShapeDtypeStruct(arr.shape, BF16) for arr in cast_weights],
        scratch_shapes=slot + slot + shared,
        compiler_params=pltpu.CompilerParams(
            dimension_semantics=("arbitrary",), vmem_limit_bytes=(7 * VMEM_BYTES_V7X) // 8),
        name="mixer",
    )(x, positions.reshape(b, 1, s), norm_g.reshape(1, d), inv, b_if, out_norm_g.reshape(1, MLSTM_V_W),
      sinks, w_mix, w_br, *cast_weights)
    return res[0], res[1:]


def kernel(x, positions, ffn1_norm_g, ffn1_w_gate, ffn1_w_up, ffn1_w_down, mix_norm_g, w_in, mlstm_b_i, mlstm_b_f, mlstm_out_norm_g, attn_sinks, w_branch_mlstm, w_branch_attn, w_out, ffn2_norm_g, ffn2_w_gate, ffn2_w_up, ffn2_w_down, final_norm_g):
    b, s, d = x.shape
    depth = w_in.shape[0]
    for l in range(depth):
        last = l == depth - 1
        x2, mixer_w = _ffn(x.reshape(b * s, d), ffn1_norm_g[l], ffn1_w_gate[l], ffn1_w_up[l], ffn1_w_down[l],
                           mixer_weights=(w_in[l], w_branch_mlstm[l], w_branch_attn[l], w_out[l]))
        x3, ffn2_w = _mixer(x2.reshape(b, s, d), positions, mix_norm_g[l], mlstm_b_i[l], mlstm_b_f[l],
                            mlstm_out_norm_g[l], attn_sinks[l], mixer_w,
                            (ffn2_w_gate[l], ffn2_w_up[l], ffn2_w_down[l]))
        x = _ffn(x3.reshape(b * s, d), ffn2_norm_g[l], *ffn2_w,
                 final_g=final_norm_g if last else None).reshape(b, s, d)
    return x
```
